```python
import jax, jax.numpy as jnp
from jax import lax
import numpy as np

D_MODEL = 1024
BATCH = 16
SEQ = 256
DEPTH = 4
DEC_BATCH = 4
DEC_SEQ = 4096
PAST_LEN = 512

GRID_W = 64
N_MIXERS = 2
N_CONV_LAYERS = (DEPTH + 1) // 2
N_SSD_LAYERS = DEPTH // 2
CONV_K = 31
SSM_INNER = 2 * D_MODEL
SSM_HEAD_DIM = 64
SSM_HEADS = SSM_INNER // SSM_HEAD_DIM
SSM_GROUPS = 4
HEADS_PER_GROUP = SSM_HEADS // SSM_GROUPS
D_STATE = 128
SSM_CONV_K = 5
SSM_CONV_DIM = SSM_INNER + 2 * SSM_GROUPS * D_STATE
SSM_IN_DIM = SSM_INNER + SSM_CONV_DIM + 2 * SSM_HEADS
SSD_CHUNK = 128
N_EXPERTS = 16
EXPERT_FF = D_MODEL
EC_CAPACITY_FACTOR = 2
ALPHA = (2 * DEPTH) ** 0.25
BETA = (8 * DEPTH) ** -0.25
LN_EPS = 1e-5

kernel_name = "hybrid_conv_ssd_ecmoe_diffusion_step"


def layer_norm(x, g, b):
    x32 = x.astype(jnp.float32)
    mu = jnp.mean(x32, -1, keepdims=True)
    var = jnp.mean(jnp.square(x32 - mu), -1, keepdims=True)
    return ((x32 - mu) * lax.rsqrt(var + LN_EPS) * g + b).astype(x.dtype)


def ada_mod(cvec, w, b):
    m = jax.nn.silu(cvec) @ w + b
    return jnp.split(m[..., None, :], 6, axis=-1)


def depthwise_conv2d(x, w, b, kh, kw):
    C = x.shape[-1]
    taps = w.reshape(kh, kw, 1, C).astype(x.dtype)
    y = lax.conv_general_dilated(x, taps, window_strides=(1, 1),
                                 padding=((kh // 2, kh // 2), (kw // 2, kw // 2)),
                                 dimension_numbers=("NHWC", "HWIO", "NHWC"),
                                 feature_group_count=C)
    return y + b


def conformer_conv(h, pw1_w, pw1_b, dw_w, dw_b, ln_g, ln_b, pw2_w, pw2_b, axis):
    u = h @ pw1_w + pw1_b
    a, g = jnp.split(u, 2, axis=-1)
    u = a * jax.nn.sigmoid(g)
    bsz, L, dm = u.shape
    if axis == "seq":
        grid, kh, kw = u[:, None], 1, CONV_K
    elif axis == "row":
        grid, kh, kw = u.reshape(bsz, L // GRID_W, GRID_W, dm), 1, CONV_K
    else:
        grid, kh, kw = u.reshape(bsz, L // GRID_W, GRID_W, dm), CONV_K, 1
    v = depthwise_conv2d(grid, dw_w, dw_b, kh, kw).reshape(bsz, L, dm)
    v = jax.nn.silu(layer_norm(v, ln_g, ln_b))
    return v @ pw2_w + pw2_b


def segsum(a):
    T = a.shape[-1]
    cs = jnp.cumsum(a, axis=-1)
    diff = cs[..., :, None] - cs[..., None, :]
    mask = jnp.tril(jnp.ones((T, T), dtype=bool))
    return jnp.where(mask, diff, -jnp.inf)


def ssd_scan(x, dt, A, B, C, init):
    b, L = x.shape[:2]
    nc = L // SSD_CHUNK
    X = (x * dt[..., None]).reshape(b, nc, SSD_CHUNK, SSM_GROUPS, HEADS_PER_GROUP, SSM_HEAD_DIM)
    a = (dt * A).reshape(b, nc, SSD_CHUNK, SSM_GROUPS, HEADS_PER_GROUP).transpose(0, 3, 4, 1, 2)
    Bc = B.reshape(b, nc, SSD_CHUNK, SSM_GROUPS, D_STATE)
    Cc = C.reshape(b, nc, SSD_CHUNK, SSM_GROUPS, D_STATE)
    a_cum = jnp.cumsum(a, axis=-1)
    Lm = jnp.exp(segsum(a))
    y_diag = jnp.einsum("bclgn,bcsgn,bgjcls,bcsgjp->bclgjp", Cc, Bc, Lm, X)
    decay_states = jnp.exp(a_cum[..., -1:] - a_cum)
    states = jnp.einsum("bclgn,bgjcl,bclgjp->bcgjpn", Bc, decay_states, X)
    init_g = init.reshape(b, SSM_GROUPS, HEADS_PER_GROUP, SSM_HEAD_DIM, D_STATE)
    states = jnp.concatenate([init_g[:, None], states], axis=1)
    chunk_a = jnp.pad(a_cum[..., -1], ((0, 0), (0, 0), (0, 0), (1, 0)))
    decay_chunk = jnp.exp(segsum(chunk_a))
    new_states = jnp.einsum("bgjzc,bcgjpn->bzgjpn", decay_chunk, states)
    prev_states, final = new_states[:, :-1], new_states[:, -1]
    y_off = jnp.einsum("bclgn,bcgjpn,bgjcl->bclgjp", Cc, prev_states, jnp.exp(a_cum))
    y = (y_diag + y_off).reshape(b, L, SSM_HEADS, SSM_HEAD_DIM)
    return y, final.reshape(b, SSM_HEADS, SSM_HEAD_DIM, D_STATE)


def ssd_mixer(h, in_w, conv_w, conv_b, dt_bias, a_log, d_skip, norm_w, out_w, init_f, init_b):
    bsz, L, _ = h.shape
    zxbcdt = h @ in_w
    z = zxbcdt[..., :SSM_INNER]
    xbc = zxbcdt[..., SSM_INNER:SSM_INNER + SSM_CONV_DIM]
    dt_raw = zxbcdt[..., SSM_INNER + SSM_CONV_DIM:]
    xbc = jax.nn.silu(depthwise_conv2d(xbc[:, None], conv_w, conv_b, 1, SSM_CONV_K)[:, 0])
    gn = SSM_GROUPS * D_STATE
    xs = xbc[..., :SSM_INNER].astype(jnp.float32).reshape(bsz, L, SSM_HEADS, SSM_HEAD_DIM)
    Bm = xbc[..., SSM_INNER:SSM_INNER + gn].astype(jnp.float32).reshape(bsz, L, SSM_GROUPS, D_STATE)
    Cm = xbc[..., SSM_INNER + gn:].astype(jnp.float32).reshape(bsz, L, SSM_GROUPS, D_STATE)
    dt = jax.nn.softplus(dt_raw.astype(jnp.float32).reshape(bsz, L, 2, SSM_HEADS) + dt_bias.astype(jnp.float32))
    A = -jnp.exp(a_log.astype(jnp.float32))
    y_f, s_f = ssd_scan(xs, dt[:, :, 0], A[0], Bm, Cm, init_f.astype(jnp.float32))
    flip = lambda t: t[:, ::-1]
    y_b, s_b = ssd_scan(flip(xs), flip(dt[:, :, 1]), A[1], flip(Bm), flip(Cm), init_b.astype(jnp.float32))
    y = y_f + flip(y_b) + d_skip.astype(jnp.float32)[:, None] * xs
    y = y.reshape(bsz, L, SSM_INNER) * jax.nn.silu(z.astype(jnp.float32))
    yg = y.reshape(bsz, L, SSM_GROUPS, SSM_INNER // SSM_GROUPS)
    yg = yg * lax.rsqrt(jnp.mean(jnp.square(yg), -1, keepdims=True) + LN_EPS)
    y = yg.reshape(bsz, L, SSM_INNER) * norm_w
    return y.astype(h.dtype) @ out_w, s_f, s_b


def ec_moe(h, router_w, w_gate, w_up, w_down):
    T = h.shape[0]
    cap = EC_CAPACITY_FACTOR * T // N_EXPERTS
    aff = jax.nn.softmax((h @ router_w).astype(jnp.float32), axis=-1)
    gates, idx = lax.top_k(aff.T, cap)
    xin = h[idx]
    hid = jax.nn.silu(jnp.einsum("ecd,edf->ecf", xin, w_gate)) * jnp.einsum("ecd,edf->ecf", xin, w_up)
    out = jnp.einsum("ecf,efd->ecd", hid, w_down) * gates[..., None].astype(h.dtype)
    return jnp.zeros_like(h).at[idx.reshape(-1)].add(out.reshape(-1, h.shape[-1]))


def setup_inputs(seed: int = 0) -> dict:
    key = jax.random.key(seed)
    ks = iter(jax.random.split(key, 40))
    f32 = jnp.float32
    nrm = lambda shape, s: jax.random.normal(next(ks), shape, f32) * s
    D, NC, NS = D_MODEL, N_CONV_LAYERS, N_SSD_LAYERS
    gate_offset = jnp.repeat(jnp.array([0., 0., 1., 0., 0., 1.], f32), D)
    dt0 = jnp.exp(jax.random.uniform(next(ks), (NS, 2, SSM_HEADS), f32) * (np.log(0.1) - np.log(0.001)) + np.log(0.001))
    return {
        "x_prompt": nrm((BATCH, SEQ, D), 1.0),
        "x_sample": nrm((DEC_BATCH, DEC_SEQ, D), 1.0),
        "state_ssd": nrm((DEC_BATCH, NS, 2, SSM_HEADS, SSM_HEAD_DIM, D_STATE), 0.1),
        "c": nrm((DEC_BATCH, D), 1.0),
        "c_ctx": nrm((D,), 1.0),
        "ada_w": nrm((DEPTH, D, 6 * D), D ** -0.5),
        "ada_b": nrm((DEPTH, 6 * D), 0.02) + gate_offset,
        "ln_mix_g": 1.0 + nrm((DEPTH, D), 0.02),
        "ln_mix_b": nrm((DEPTH, D), 0.02),
        "ln_ffn_g": 1.0 + nrm((DEPTH, D), 0.02),
        "ln_ffn_b": nrm((DEPTH, D), 0.02),
        "conv_pw1_w": nrm((NC, D, 2 * D), D ** -0.5),
        "conv_pw1_b": nrm((NC, 2 * D), 0.02),
        "conv_dw_w": nrm((NC, CONV_K, D), CONV_K ** -0.5),
        "conv_dw_b": nrm((NC, D), 0.02),
        "conv_ln_g": 1.0 + nrm((NC, D), 0.02),
        "conv_ln_b": nrm((NC, D), 0.02),
        "conv_pw2_w": nrm((NC, D, D), BETA * D ** -0.5),
        "conv_pw2_b": nrm((NC, D), 0.02),
        "ssd_in_w": nrm((NS, D, SSM_IN_DIM), D ** -0.5),
        "ssd_conv_w": nrm((NS, SSM_CONV_K, SSM_CONV_DIM), SSM_CONV_K ** -0.5),
        "ssd_conv_b": nrm((NS, SSM_CONV_DIM), 0.02),
        "ssd_dt_bias": dt0 + jnp.log(-jnp.expm1(-dt0)),
        "ssd_a_log": jnp.log(jax.random.uniform(next(ks), (NS, 2, SSM_HEADS), f32, 1.0, 16.0)),
        "ssd_d_skip": 1.0 + nrm((NS, SSM_HEADS), 0.1),
        "ssd_norm_w": 1.0 + nrm((NS, SSM_INNER), 0.02),
        "ssd_out_w": nrm((NS, SSM_INNER, D), BETA * SSM_INNER ** -0.5),
        "router_w": nrm((DEPTH, D, N_EXPERTS), D ** -0.5),
        "moe_w_gate": nrm((DEPTH, N_EXPERTS, D, EXPERT_FF), D ** -0.5),
        "moe_w_up": nrm((DEPTH, N_EXPERTS, D, EXPERT_FF), D ** -0.5),
        "moe_w_down": nrm((DEPTH, N_EXPERTS, EXPERT_FF, D), BETA * EXPERT_FF ** -0.5),
    }


def reference(x_prompt, x_sample, state_ssd, c, c_ctx, ada_w, ada_b, ln_mix_g, ln_mix_b,
              ln_ffn_g, ln_ffn_b, conv_pw1_w, conv_pw1_b, conv_dw_w, conv_dw_b, conv_ln_g,
              conv_ln_b, conv_pw2_w, conv_pw2_b, ssd_in_w, ssd_conv_w, ssd_conv_b, ssd_dt_bias,
              ssd_a_log, ssd_d_skip, ssd_norm_w, ssd_out_w, router_w, moe_w_gate, moe_w_up,
              moe_w_down):
    yp, ys = x_prompt, x_sample
    new_states = []
    for i in range(DEPTH):
        sh_mp, sc_mp, g_mp, sh_fp, sc_fp, g_fp = ada_mod(c_ctx, ada_w[i], ada_b[i])
        sh_ms, sc_ms, g_ms, sh_fs, sc_fs, g_fs = ada_mod(c, ada_w[i], ada_b[i])
        hp = yp * (1 + sc_mp) + sh_mp
        hs = ys * (1 + sc_ms) + sh_ms
        k = i // N_MIXERS
        if i % N_MIXERS == 0:
            cp = (conv_pw1_w[k], conv_pw1_b[k], conv_dw_w[k], conv_dw_b[k],
                  conv_ln_g[k], conv_ln_b[k], conv_pw2_w[k], conv_pw2_b[k])
            mp = conformer_conv(hp, *cp, axis="seq")
            ms = conformer_conv(hs, *cp, axis="row" if k % 2 == 0 else "col")
        else:
            sp = (ssd_in_w[k], ssd_conv_w[k], ssd_conv_b[k], ssd_dt_bias[k], ssd_a_log[k],
                  ssd_d_skip[k], ssd_norm_w[k], ssd_out_w[k])
            zero = jnp.zeros((yp.shape[0], SSM_HEADS, SSM_HEAD_DIM, D_STATE), jnp.float32)
            mp, s_f, s_b = ssd_mixer(hp, *sp, zero, zero)
            ms, _, _ = ssd_mixer(hs, *sp, state_ssd[:, k, 0], state_ssd[:, k, 1])
            new_states.append(jnp.stack([s_f, s_b], axis=1).astype(x_prompt.dtype))
        yp = layer_norm(ALPHA * yp + g_mp * mp, ln_mix_g[i], ln_mix_b[i])
        ys = layer_norm(ALPHA * ys + g_ms * ms, ln_mix_g[i], ln_mix_b[i])
        moe_p = (router_w[i], moe_w_gate[i], moe_w_up[i], moe_w_down[i])
        hp = yp * (1 + sc_fp) + sh_fp
        hs = ys * (1 + sc_fs) + sh_fs
        fp = ec_moe(hp.reshape(-1, D_MODEL), *moe_p).reshape(yp.shape)
        fs = ec_moe(hs.reshape(-1, D_MODEL), *moe_p).reshape(ys.shape)
        yp = layer_norm(ALPHA * yp + g_fp * fp, ln_ffn_g[i], ln_ffn_b[i])
        ys = layer_norm(ALPHA * ys + g_fs * fs, ln_ffn_g[i], ln_ffn_b[i])
    new_state_ssd = jnp.stack(new_states, axis=1)
    return (yp, ys, new_state_ssd)
```

```python
import functools

import jax
import jax.numpy as jnp
from jax import lax
from jax.experimental import pallas as pl
from jax.experimental.pallas import tpu as pltpu

F32 = jnp.float32
BF16 = jnp.bfloat16
HIGHEST = lax.Precision.HIGHEST

D_MODEL = 1024
DEPTH = 4
GRID_W = 64
CONV_K = 31
SSM_INNER = 2 * D_MODEL
SSM_HEAD_DIM = 64
SSM_HEADS = SSM_INNER // SSM_HEAD_DIM
SSM_GROUPS = 4
D_STATE = 128
SSM_CONV_K = 5
SSM_BC = SSM_GROUPS * D_STATE
SSM_CONV_DIM = SSM_INNER + 2 * SSM_BC
SSD_CHUNK = 128
N_EXPERTS = 16
EC_CAPACITY_FACTOR = 2
ALPHA = (2 * DEPTH) ** 0.25
LN_EPS = 1e-5

LANES = 128
SUBLANES = 8
VMEM_LIMIT = 56 * 1024 * 1024
MOD_ROWS = 8
CONV_PAD = 16
MOE_TOKEN_BLOCK = 2048
MOE_SUB = 256
MOE_ROWS = 128


def _params(*sem):
    return pltpu.CompilerParams(dimension_semantics=sem, vmem_limit_bytes=VMEM_LIMIT)


def _layer_norm(x, g, b):
    mu = jnp.mean(x, axis=-1, keepdims=True)
    xc = x - mu
    var = jnp.mean(xc * xc, axis=-1, keepdims=True)
    return xc * lax.rsqrt(var + LN_EPS) * g + b


def _silu(x):
    return x * jax.nn.sigmoid(x)


def _softplus(x):
    return jnp.maximum(x, 0.0) + jnp.log1p(jnp.exp(-jnp.abs(x)))


def _ada_kernel(c_ref, w_ref, b_ref, o_ref):
    x = _silu(c_ref[...])
    o_ref[0] = jnp.dot(x, w_ref[0], precision=HIGHEST, preferred_element_type=F32) + b_ref[0]


def _ada_mod(cond, ada_w, ada_b):
    depth, d, n = ada_w.shape
    tn = 1536
    return pl.pallas_call(
        _ada_kernel,
        grid=(depth, n // tn),
        in_specs=[
            pl.BlockSpec((MOD_ROWS, d), lambda i, j: (0, 0)),
            pl.BlockSpec((1, d, tn), lambda i, j: (i, 0, j)),
            pl.BlockSpec((1, 1, tn), lambda i, j: (i, 0, j)),
        ],
        out_specs=pl.BlockSpec((1, MOD_ROWS, tn), lambda i, j: (i, 0, j)),
        out_shape=jax.ShapeDtypeStruct((depth, MOD_ROWS, n), F32),
        compiler_params=_params("parallel", "parallel"),
        name="ada_mod",
    )(cond, ada_w, ada_b.reshape(depth, 1, n))


def _mod_index(mod):
    if mod.shape[0] == 1:
        return lambda b: 0
    return lambda b: b


def _pw1_glu_kernel(x_ref, mod_ref, w_ref, b_ref, o_ref):
    d = x_ref.shape[-1]
    x = x_ref[0]
    h = (x * (1.0 + mod_ref[0, 1:2, :]) + mod_ref[0, 0:1, :]).astype(BF16)
    u = jnp.dot(h, w_ref[...], preferred_element_type=F32) + b_ref[...]
    o_ref[0] = u[:, :d] * jax.nn.sigmoid(u[:, d:])


def _pw1_glu(x, mod, w, b, tl):
    bsz, L, d = x.shape
    mi = _mod_index(mod)
    return pl.pallas_call(
        _pw1_glu_kernel,
        grid=(bsz, L // tl),
        in_specs=[
            pl.BlockSpec((1, tl, d), lambda i, j: (i, j, 0)),
            pl.BlockSpec((1, MOD_ROWS, d), lambda i, j: (mi(i), 0, 0)),
            pl.BlockSpec((d, 2 * d), lambda i, j: (0, 0)),
            pl.BlockSpec((1, 2 * d), lambda i, j: (0, 0)),
        ],
        out_specs=pl.BlockSpec((1, tl, d), lambda i, j: (i, j, 0)),
        out_shape=jax.ShapeDtypeStruct((bsz, L, d), F32),
        compiler_params=_params("parallel", "parallel"),
        name="conv_pw1_glu",
    )(x, mod, w, b)


def _conv_tail(v, y, mod_ref, lng_ref, lnb_ref, w2_ref, b2_ref, mixg_ref, mixb_ref):
    v = _silu(_layer_norm(v, lng_ref[...], lnb_ref[...]))
    m = jnp.dot(v.astype(BF16), w2_ref[...], preferred_element_type=F32) + b2_ref[...]
    return _layer_norm(ALPHA * y + mod_ref[0, 2:3, :] * m, mixg_ref[...], mixb_ref[...])


def _dwconv_seq_kernel(u_ref, y_ref, mod_ref, dww_ref, dwb_ref, lng_ref, lnb_ref, w2_ref, b2_ref,
                       mixg_ref, mixb_ref, o_ref, pad_ref, v_ref, *, seg, nseg):
    d = u_ref.shape[-1]
    sp = seg + 2 * CONV_PAD
    zeros = jnp.zeros((CONV_PAD, d), F32)
    for s in range(nseg):
        pad_ref[s * sp:s * sp + CONV_PAD, :] = zeros
        pad_ref[s * sp + CONV_PAD:s * sp + CONV_PAD + seg, :] = u_ref[0, s * seg:(s + 1) * seg, :]
        pad_ref[s * sp + CONV_PAD + seg:(s + 1) * sp, :] = zeros
    first = CONV_PAD - CONV_K // 2
    rows = min(seg, 64)
    for s in range(nseg):
        for r in range(seg // rows):
            for c in range(d // LANES):
                ln = slice(c * LANES, (c + 1) * LANES)
                base = s * sp + first + r * rows
                acc = jnp.zeros((rows, LANES), F32) + dwb_ref[:, ln]
                for k in range(CONV_K):
                    acc = acc + dww_ref[k:k + 1, ln] * pad_ref[base + k:base + k + rows, ln]
                v_ref[s * seg + r * rows:s * seg + (r + 1) * rows, ln] = acc
    o_ref[0] = _conv_tail(v_ref[...], y_ref[0], mod_ref, lng_ref, lnb_ref, w2_ref, b2_ref,
                          mixg_ref, mixb_ref)


def _dwconv_col_kernel(u_ref, y_ref, mod_ref, dww_ref, dwb_ref, lng_ref, lnb_ref, w2_ref, b2_ref,
                       mixg_ref, mixb_ref, o_ref, pad_ref, v_ref):
    hh, wt, d = u_ref.shape[1:]
    zeros = jnp.zeros((CONV_PAD, wt, d), F32)
    pad_ref[0:CONV_PAD] = zeros
    pad_ref[CONV_PAD:CONV_PAD + hh] = u_ref[0]
    pad_ref[CONV_PAD + hh:CONV_PAD + hh + CONV_PAD] = zeros
    first = CONV_PAD - CONV_K // 2
    rows = 32
    for r in range(hh // rows):
        for c in range(d // LANES):
            ln = slice(c * LANES, (c + 1) * LANES)
            base = first + r * rows
            acc = jnp.zeros((rows, wt, LANES), F32) + dwb_ref[:, ln]
            for k in range(CONV_K):
                acc = acc + dww_ref[k:k + 1, ln] * pad_ref[base + k:base + k + rows, :, ln]
            v_ref[r * rows:(r + 1) * rows, :, ln] = acc
    out = _conv_tail(v_ref[...].reshape(hh * wt, d), y_ref[0].reshape(hh * wt, d), mod_ref,
                     lng_ref, lnb_ref, w2_ref, b2_ref, mixg_ref, mixb_ref)
    o_ref[0] = out.reshape(hh, wt, d)


def _conv_weights_specs(d):
    vec = pl.BlockSpec((1, d), lambda i, j: (0, 0))
    return [
        pl.BlockSpec((CONV_K, d), lambda i, j: (0, 0)),
        vec, vec, vec,
        pl.BlockSpec((d, d), lambda i, j: (0, 0)),
        vec, vec, vec,
    ]


def _dwconv_seq(u, y, mod, weights, seg, nseg):
    bsz, L, d = u.shape
    tl = seg * nseg
    mi = _mod_index(mod)
    tok = pl.BlockSpec((1, tl, d), lambda i, j: (i, j, 0))
    return pl.pallas_call(
        functools.partial(_dwconv_seq_kernel, seg=seg, nseg=nseg),
        grid=(bsz, L // tl),
        in_specs=[tok, tok, pl.BlockSpec((1, MOD_ROWS, d), lambda i, j: (mi(i), 0, 0))]
        + _conv_weights_specs(d),
        out_specs=tok,
        out_shape=jax.ShapeDtypeStruct((bsz, L, d), F32),
        scratch_shapes=[pltpu.VMEM((nseg * (seg + 2 * CONV_PAD), d), F32), pltpu.VMEM((tl, d), F32)],
        compiler_params=_params("parallel", "parallel"),
        name="conv_dw_seq",
    )(u, y, mod, *weights)


def _dwconv_col(u, y, mod, weights):
    bsz, L, d = u.shape
    hh = L // GRID_W
    u4 = u.reshape(bsz, hh, GRID_W, d)
    y4 = y.reshape(bsz, hh, GRID_W, d)
    mi = _mod_index(mod)
    tok = pl.BlockSpec((1, hh, SUBLANES, d), lambda i, j: (i, 0, j, 0))
    out = pl.pallas_call(
        _dwconv_col_kernel,
        grid=(bsz, GRID_W // SUBLANES),
        in_specs=[tok, tok, pl.BlockSpec((1, MOD_ROWS, d), lambda i, j: (mi(i), 0, 0))]
        + _conv_weights_specs(d),
        out_specs=tok,
        out_shape=jax.ShapeDtypeStruct((bsz, hh, GRID_W, d), F32),
        scratch_shapes=[pltpu.VMEM((hh + 2 * CONV_PAD, SUBLANES, d), F32),
                        pltpu.VMEM((hh, SUBLANES, d), F32)],
        compiler_params=_params("parallel", "parallel"),
        name="conv_dw_col",
    )(u4, y4, mod, *weights)
    return out.reshape(bsz, L, d)


def _conformer_layer(y, mod, cw, axis, mixg, mixb):
    pw1_w, pw1_b, dw_w, dw_b, ln_g, ln_b, pw2_w, pw2_b = cw
    bsz, L, d = y.shape
    row = lambda a: a.reshape(1, -1)
    u = _pw1_glu(y, mod, pw1_w.astype(BF16), row(pw1_b), tl=min(L, 512))
    weights = (dw_w, row(dw_b), row(ln_g), row(ln_b), pw2_w.astype(BF16), row(pw2_b),
               row(mixg), row(mixb))
    if axis == "seq":
        return _dwconv_seq(u, y, mod, weights, seg=L, nseg=1)
    if axis == "row":
        return _dwconv_seq(u, y, mod, weights, seg=GRID_W, nseg=min(8, L // GRID_W))
    return _dwconv_col(u, y, mod, weights)


def _ssd_in_kernel(x_ref, mod_ref, wz_ref, wx_ref, wdt_ref, wdtT_ref, z_ref, xbc_ref, dt_ref, dtT_ref):
    x = x_ref[0]
    h32 = x * (1.0 + mod_ref[0, 1:2, :]) + mod_ref[0, 0:1, :]
    h = h32.astype(BF16)
    z_ref[0] = jnp.dot(h, wz_ref[...], preferred_element_type=F32)
    xbc_ref[0] = jnp.dot(h, wx_ref[...], preferred_element_type=F32)
    dt_ref[0] = jnp.dot(h32, wdt_ref[...], precision=HIGHEST, preferred_element_type=F32)
    dtT_ref[0] = lax.dot_general(wdtT_ref[...], h32, (((1,), (1,)), ((), ())), precision=HIGHEST,
                                 preferred_element_type=F32)


def _ssd_in_proj(y, mod, wz, wx, wdt, tl):
    bsz, L, d = y.shape
    mi = _mod_index(mod)
    ndt = wdt.shape[1]
    full = lambda a: pl.BlockSpec(a.shape, lambda i, j: (0, 0))
    wdtT = wdt.T
    return pl.pallas_call(
        _ssd_in_kernel,
        grid=(bsz, L // tl),
        in_specs=[
            pl.BlockSpec((1, tl, d), lambda i, j: (i, j, 0)),
            pl.BlockSpec((1, MOD_ROWS, d), lambda i, j: (mi(i), 0, 0)),
            full(wz), full(wx), full(wdt), full(wdtT),
        ],
        out_specs=[
            pl.BlockSpec((1, tl, SSM_INNER), lambda i, j: (i, j, 0)),
            pl.BlockSpec((1, tl, SSM_CONV_DIM), lambda i, j: (i, j, 0)),
            pl.BlockSpec((1, tl, ndt), lambda i, j: (i, j, 0)),
            pl.BlockSpec((1, ndt, tl), lambda i, j: (i, 0, j)),
        ],
        out_shape=[
            jax.ShapeDtypeStruct((bsz, L, SSM_INNER), F32),
            jax.ShapeDtypeStruct((bsz, L, SSM_CONV_DIM), F32),
            jax.ShapeDtypeStruct((bsz, L, ndt), F32),
            jax.ShapeDtypeStruct((bsz, ndt, L), F32),
        ],
        compiler_params=_params("parallel", "parallel"),
        name="ssd_in_proj",
    )(y, mod, wz, wx, wdt, wdtT)


def _ssd_conv_kernel(x_ref, w_ref, b_ref, o_ref, pad_ref):
    L, cb = x_ref.shape[1:]
    zeros = jnp.zeros((SUBLANES, cb), F32)
    pad_ref[0:SUBLANES, :] = zeros
    pad_ref[SUBLANES:SUBLANES + L, :] = x_ref[0]
    pad_ref[SUBLANES + L:2 * SUBLANES + L, :] = zeros
    first = SUBLANES - SSM_CONV_K // 2
    rows = min(L, 256)
    for r in range(L // rows):
        for c in range(cb // LANES):
            ln = slice(c * LANES, (c + 1) * LANES)
            base = first + r * rows
            acc = jnp.zeros((rows, LANES), F32) + b_ref[:, ln]
            for k in range(SSM_CONV_K):
                acc = acc + w_ref[k:k + 1, ln] * pad_ref[base + k:base + k + rows, ln]
            o_ref[0, r * rows:(r + 1) * rows, ln] = _silu(acc)


def _ssd_conv(xbc, w, b):
    bsz, L, cd = xbc.shape
    cb = max(w for w in range(LANES, cd + 1, LANES) if cd % w == 0 and (w * L <= (1 << 20) or w == LANES))
    return pl.pallas_call(
        _ssd_conv_kernel,
        grid=(bsz, cd // cb),
        in_specs=[
            pl.BlockSpec((1, L, cb), lambda i, j: (i, 0, j)),
            pl.BlockSpec((SSM_CONV_K, cb), lambda i, j: (0, j)),
            pl.BlockSpec((1, cb), lambda i, j: (0, j)),
        ],
        out_specs=pl.BlockSpec((1, L, cb), lambda i, j: (i, 0, j)),
        out_shape=jax.ShapeDtypeStruct((bsz, L, cd), F32),
        scratch_shapes=[pltpu.VMEM((L + 2 * SUBLANES, cb), F32)],
        compiler_params=_params("parallel", "parallel"),
        name="ssd_conv",
    )(xbc, w, b.reshape(1, cd))


def _split3(x):
    hi = x.astype(BF16)
    r1 = x - hi.astype(F32)
    mid = r1.astype(BF16)
    lo = (r1 - mid.astype(F32)).astype(BF16)
    return jnp.concatenate([hi, mid, lo], axis=1)


def _ssd_scan_kernel(xs_ref, b_ref, c_ref, dt_ref, dtT_ref, bias_row_ref, bias_col_ref, alog_row_ref,
                     alog_col_ref, tri_ref, e3_ref, init_ref, y_ref, fin_ref, s_ref):
    d = pl.program_id(0)
    c = pl.program_id(2)
    q = SSD_CHUNK
    nh = SSM_HEADS
    gw = SSM_INNER // SSM_GROUPS

    @pl.when(c == 0)
    def _():
        s_ref[...] = init_ref[0, 0]

    is_f = d == 0
    tri = tri_ref[0]
    mask = tri > 0.5
    dt2 = _softplus(dt_ref[0] + bias_row_ref[...])
    a2 = dt2 * -jnp.exp(alog_row_ref[...])
    dt2T = _softplus(dtT_ref[0] + bias_col_ref[...])
    a2T = dt2T * -jnp.exp(alog_col_ref[...])
    cum2 = jnp.dot(tri, a2, precision=HIGHEST, preferred_element_type=F32)
    cumT2 = lax.dot_general(a2T, tri, (((1,), (1,)), ((), ())), precision=HIGHEST,
                            preferred_element_type=F32)
    cum = jnp.where(is_f, cum2[:, :nh], cum2[:, nh:2 * nh])
    cumT = jnp.where(is_f, cumT2[:nh], cumT2[nh:2 * nh])
    dtT = jnp.where(is_f, dt2T[:nh], dt2T[nh:2 * nh])
    tot2 = jnp.where(is_f, cum2[q - 1:q], cum2[0:1])
    small = jnp.concatenate([dt2 * jnp.exp(jnp.minimum(tot2 - cum2, 0.0)), jnp.exp(cum2),
                             jnp.broadcast_to(jnp.exp(tot2), (SUBLANES, LANES))], axis=0)
    wide = jnp.dot(_split3(small), e3_ref[0], preferred_element_type=F32)
    w_state = wide[0:q]
    w_off = wide[q:2 * q]
    w_tot = wide[2 * q:2 * q + 1]

    xs = xs_ref[0]
    xs_b = xs.astype(BF16)
    xdec = (xs * w_state).astype(BF16)
    lane = lax.broadcasted_iota(jnp.int32, (q, LANES), 1)
    for g in range(SSM_GROUPS):
        gs = slice(g * gw, (g + 1) * gw)
        bg = b_ref[0, :, g * D_STATE:(g + 1) * D_STATE]
        cg = c_ref[0, :, g * D_STATE:(g + 1) * D_STATE].astype(BF16)
        cb = lax.dot_general(cg, bg.astype(BF16), (((1,), (1,)), ((), ())), preferred_element_type=F32)
        s_g = s_ref[:, gs]
        y_off = jnp.dot(cg, s_g.astype(BF16), preferred_element_type=F32) * w_off[:, gs]
        s_ref[:, gs] = w_tot[:, gs] * s_g + jnp.dot(bg.T.astype(BF16), xdec[:, gs],
                                                    preferred_element_type=F32)
        pairs = gw // LANES
        for p in range(pairs):
            ls = slice(g * gw + p * LANES, g * gw + (p + 1) * LANES)
            x_pair = xs_b[:, ls]
            outs = []
            for j in range(LANES // SSM_HEAD_DIM):
                h = (g * gw + p * LANES) // SSM_HEAD_DIM + j
                diff = cum[:, h:h + 1] - cumT[h:h + 1, :]
                lm = jnp.exp(jnp.where(mask, diff, -1e30))
                m = (cb * lm * dtT[h:h + 1, :]).astype(BF16)
                outs.append(jnp.dot(m, x_pair, preferred_element_type=F32))
            y_diag = jnp.where(lane < SSM_HEAD_DIM, outs[0], outs[1])
            y_ref[0, 0, :, ls] = y_diag + y_off[:, p * LANES:(p + 1) * LANES]

    @pl.when(c == pl.num_programs(2) - 1)
    def _():
        fin_ref[0, 0] = s_ref[...]


def _ssd_scan(xbc, dt_raw, dt_rawT, dt_bias, a_log, init):
    bsz, L, _ = xbc.shape
    nc = L // SSD_CHUNK
    q = SSD_CHUNK
    ndt = LANES
    chunk = lambda d, c: c + d * (nc - 1 - 2 * c)
    r = jnp.arange(q)
    tri = jnp.stack([r[:, None] >= r[None, :], r[:, None] <= r[None, :]]).astype(F32)
    src = jnp.arange(ndt)[None, :, None] - SSM_HEADS * jnp.arange(2)[:, None, None]
    e1 = (src == (jnp.arange(SSM_INNER) // SSM_HEAD_DIM)[None, None, :]).astype(BF16)
    e3 = jnp.concatenate([e1, e1, e1], axis=1)
    pad = lambda a: jnp.pad(a.reshape(-1), (0, ndt - a.size))
    dt_bias, a_log = pad(dt_bias), pad(a_log)
    nblk = SSM_INNER // SSM_BC
    const = lambda shape: pl.BlockSpec(shape, lambda d, b, c: tuple(0 for _ in shape))
    return pl.pallas_call(
        _ssd_scan_kernel,
        grid=(2, bsz, nc),
        in_specs=[
            pl.BlockSpec((1, q, SSM_INNER), lambda d, b, c: (b, chunk(d, c), 0)),
            pl.BlockSpec((1, q, SSM_BC), lambda d, b, c: (b, chunk(d, c), nblk)),
            pl.BlockSpec((1, q, SSM_BC), lambda d, b, c: (b, chunk(d, c), nblk + 1)),
            pl.BlockSpec((1, q, ndt), lambda d, b, c: (b, chunk(d, c), 0)),
            pl.BlockSpec((1, ndt, q), lambda d, b, c: (b, 0, chunk(d, c))),
            const((1, ndt)), const((ndt, 1)), const((1, ndt)), const((ndt, 1)),
            pl.BlockSpec((1, q, q), lambda d, b, c: (d, 0, 0)),
            pl.BlockSpec((1, 3 * ndt, SSM_INNER), lambda d, b, c: (d, 0, 0)),
            pl.BlockSpec((1, 1, D_STATE, SSM_INNER), lambda d, b, c: (d, b, 0, 0)),
        ],
        out_specs=[
            pl.BlockSpec((1, 1, q, SSM_INNER), lambda d, b, c: (d, b, chunk(d, c), 0)),
            pl.BlockSpec((1, 1, D_STATE, SSM_INNER), lambda d, b, c: (d, b, 0, 0)),
        ],
        out_shape=[
            jax.ShapeDtypeStruct((2, bsz, L, SSM_INNER), F32),
            jax.ShapeDtypeStruct((2, bsz, D_STATE, SSM_INNER), F32),
        ],
        scratch_shapes=[pltpu.VMEM((D_STATE, SSM_INNER), F32)],
        compiler_params=_params("parallel", "parallel", "arbitrary"),
        name="ssd_scan",
    )(xbc, xbc, xbc, dt_raw, dt_rawT, dt_bias.reshape(1, ndt), dt_bias.reshape(ndt, 1),
      a_log.reshape(1, ndt), a_log.reshape(ndt, 1), tri, e3, init)


def _ssd_out_kernel(yf_ref, yb_ref, xs_ref, z_ref, res_ref, mod_ref, dskip_ref, nw_ref, wo_ref,
                    mixg_ref, mixb_ref, o_ref):
    y = yf_ref[0, 0] + yb_ref[0, 0] + dskip_ref[...] * xs_ref[0]
    y = y * _silu(z_ref[0])
    gw = SSM_INNER // SSM_GROUPS
    parts = []
    for g in range(SSM_GROUPS):
        yg = y[:, g * gw:(g + 1) * gw]
        ms = jnp.mean(yg * yg, axis=-1, keepdims=True)
        parts.append(yg * lax.rsqrt(ms + LN_EPS))
    yn = jnp.concatenate(parts, axis=1) * nw_ref[...]
    m = jnp.dot(yn.astype(BF16), wo_ref[...], preferred_element_type=F32)
    o_ref[0] = _layer_norm(ALPHA * res_ref[0] + mod_ref[0, 2:3, :] * m, mixg_ref[...], mixb_ref[...])


def _ssd_out(yscan, xbc, z, y, mod, d_skip, norm_w, out_w, mixg, mixb, tl):
    bsz, L, d = y.shape
    mi = _mod_index(mod)
    inner = pl.BlockSpec((1, tl, SSM_INNER), lambda i, j: (i, j, 0))
    vec = lambda n: pl.BlockSpec((1, n), lambda i, j: (0, 0))
    return pl.pallas_call(
        _ssd_out_kernel,
        grid=(bsz, L // tl),
        in_specs=[
            pl.BlockSpec((1, 1, tl, SSM_INNER), lambda i, j: (0, i, j, 0)),
            pl.BlockSpec((1, 1, tl, SSM_INNER), lambda i, j: (1, i, j, 0)),
            inner, inner,
            pl.BlockSpec((1, tl, d), lambda i, j: (i, j, 0)),
            pl.BlockSpec((1, MOD_ROWS, d), lambda i, j: (mi(i), 0, 0)),
            vec(SSM_INNER), vec(SSM_INNER),
            pl.BlockSpec((SSM_INNER, d), lambda i, j: (0, 0)),
            vec(d), vec(d),
        ],
        out_specs=pl.BlockSpec((1, tl, d), lambda i, j: (i, j, 0)),
        out_shape=jax.ShapeDtypeStruct((bsz, L, d), F32),
        compiler_params=_params("parallel", "parallel"),
        name="ssd_out",
    )(yscan, yscan, xbc, z, y, mod, jnp.repeat(d_skip, SSM_HEAD_DIM).reshape(1, SSM_INNER),
      norm_w.reshape(1, SSM_INNER), out_w, mixg.reshape(1, d), mixb.reshape(1, d))


def _ssd_layer(y, mod, sw, init, mixg, mixb):
    in_w, conv_w, conv_b, dt_bias, a_log, d_skip, norm_w, out_w = sw
    bsz, L, d = y.shape
    wz = in_w[:, :SSM_INNER].astype(BF16)
    wx = in_w[:, SSM_INNER:SSM_INNER + SSM_CONV_DIM].astype(BF16)
    wdt = in_w[:, SSM_INNER + SSM_CONV_DIM:]
    wdt = jnp.pad(wdt, ((0, 0), (0, LANES - wdt.shape[1])))
    z, xbc, dt_raw, dt_rawT = _ssd_in_proj(y, mod, wz, wx, wdt, tl=min(L, 256))
    xbc = _ssd_conv(xbc, conv_w, conv_b)
    if init is None:
        init_t = jnp.zeros((2, bsz, D_STATE, SSM_INNER), F32)
    else:
        init_t = init.astype(F32).transpose(1, 0, 4, 2, 3).reshape(2, bsz, D_STATE, SSM_INNER)
    yscan, fin = _ssd_scan(xbc, dt_raw, dt_rawT, dt_bias, a_log, init_t)
    out = _ssd_out(yscan, xbc, z, y, mod, d_skip, norm_w, out_w.astype(BF16), mixg, mixb,
                   tl=min(L, 256))
    fin = fin.reshape(2, bsz, D_STATE, SSM_HEADS, SSM_HEAD_DIM).transpose(1, 0, 3, 4, 2)
    return out, fin


def _moe_prep_kernel(y_ref, mod_ref, rwT_ref, h_ref, affT_ref):
    h32 = y_ref[0] * (1.0 + mod_ref[0, 4:5, :]) + mod_ref[0, 3:4, :]
    h_ref[...] = h32.astype(BF16)
    logT = lax.dot_general(rwT_ref[...], h32, (((1,), (1,)), ((), ())), precision=HIGHEST,
                           preferred_element_type=F32)
    ex = jnp.exp(logT - jnp.max(logT, axis=0, keepdims=True))
    affT_ref[...] = ex / jnp.sum(ex, axis=0, keepdims=True)


def _moe_prep(y, mod, router_w, tl):
    bsz, L, d = y.shape
    mi = _mod_index(mod)
    nt = L // tl
    ne = router_w.shape[1]
    return pl.pallas_call(
        _moe_prep_kernel,
        grid=(bsz, nt),
        in_specs=[
            pl.BlockSpec((1, tl, d), lambda i, j: (i, j, 0)),
            pl.BlockSpec((1, MOD_ROWS, d), lambda i, j: (mi(i), 0, 0)),
            pl.BlockSpec((ne, d), lambda i, j: (0, 0)),
        ],
        out_specs=[
            pl.BlockSpec((tl, d), lambda i, j: (i * nt + j, 0)),
            pl.BlockSpec((ne, tl), lambda i, j: (0, i * nt + j)),
        ],
        out_shape=[
            jax.ShapeDtypeStruct((bsz * L, d), BF16),
            jax.ShapeDtypeStruct((ne, bsz * L), F32),
        ],
        compiler_params=_params("parallel", "parallel"),
        name="moe_prep",
    )(y, mod, router_w.T)


def _moe_select_kernel(aff_ref, upper_ref, pos_ref, start_ref, total_ref, *, cap, tb):
    ne, T = aff_ref.shape
    bits = lax.bitcast_convert_type(aff_ref[...], jnp.int32)

    def search(i, v):
        cand = v | lax.shift_left(jnp.int32(1), 30 - i)
        cnt = jnp.sum(jnp.where(bits >= cand, 1.0, 0.0), axis=1, keepdims=True)
        return jnp.where(cnt >= cap, cand, v)

    thr = lax.fori_loop(0, 31, search, jnp.zeros((ne, 1), jnp.int32))
    need = cap - jnp.sum(jnp.where(bits > thr, 1.0, 0.0), axis=1, keepdims=True)
    upper = upper_ref[...]
    lane = lax.broadcasted_iota(jnp.int32, (ne, LANES), 1)
    tie_rank = jnp.zeros((ne, 1), F32)
    slot = jnp.zeros((ne, 1), F32)
    starts = jnp.zeros((ne, LANES), F32)
    totals = jnp.zeros((ne, LANES), F32)
    for j in range(T // LANES):
        t0 = j * LANES
        if t0 % MOE_SUB == 0:
            starts = jnp.where(lane == t0 // MOE_SUB, slot, starts)
        blk = bits[:, t0:t0 + LANES]
        eq = jnp.where(blk == thr, 1.0, 0.0)
        eq_incl = jnp.dot(eq.astype(BF16), upper, preferred_element_type=F32)
        sel = (blk > thr) | ((blk == thr) & (eq_incl - eq + tie_rank < need))
        tie_rank = tie_rank + eq_incl[:, LANES - 1:LANES]
        m = jnp.where(sel, 1.0, 0.0)
        m_incl = jnp.dot(m.astype(BF16), upper, preferred_element_type=F32)
        pos_ref[:, t0:t0 + LANES] = jnp.where(sel, m_incl - m + slot, -1.0).astype(jnp.int32)
        slot = slot + m_incl[:, LANES - 1:LANES]
        if (t0 + LANES) % tb == 0:
            totals = jnp.where(lane == t0 // tb, slot, totals)
            slot = jnp.zeros((ne, 1), F32)
    start_ref[...] = starts.astype(jnp.int32)
    total_ref[...] = totals.astype(jnp.int32)


def _moe_select(affT, cap, tb):
    ne, T = affT.shape
    r = jnp.arange(LANES)
    upper = (r[:, None] <= r[None, :]).astype(BF16)
    return pl.pallas_call(
        functools.partial(_moe_select_kernel, cap=cap, tb=tb),
        grid=(1,),
        in_specs=[pl.BlockSpec((ne, T), lambda i: (0, 0)), pl.BlockSpec((LANES, LANES), lambda i: (0, 0))],
        out_specs=[pl.BlockSpec((ne, T), lambda i: (0, 0)), pl.BlockSpec((ne, LANES), lambda i: (0, 0)),
                   pl.BlockSpec((ne, LANES), lambda i: (0, 0))],
        out_shape=[jax.ShapeDtypeStruct((ne, T), jnp.int32), jax.ShapeDtypeStruct((ne, LANES), jnp.int32),
                   jax.ShapeDtypeStruct((ne, LANES), jnp.int32)],
        compiler_params=_params("arbitrary"),
        name="moe_select",
    )(affT, upper)


def _moe_ffn_kernel(cs_ref, h_ref, pos_ref, gate_ref, wg_ref, wu_ref, wd_ref, o_ref, x_ref, gs_ref,
                    *, nsub):
    b = pl.program_id(0)
    e = pl.program_id(1)
    ne = pl.num_programs(1)
    rows = MOE_ROWS

    @pl.when(e == 0)
    def _():
        o_ref[...] = jnp.zeros(o_ref.shape, F32)

    base = (b * ne + e) * (nsub + 1)
    n = cs_ref[base + nsub]
    slot_iota = lax.broadcasted_iota(jnp.int32, (rows, MOE_SUB), 0)

    def hits(j, lo):
        return pos_ref[0, :, j * MOE_SUB:(j + 1) * MOE_SUB] == slot_iota + lo

    def overlaps(j, lo):
        return (cs_ref[base + j + 1] > lo) & (cs_ref[base + j] < lo + rows)

    def chunk(k, carry):
        lo = k * rows
        x_ref[...] = jnp.zeros(x_ref.shape, F32)
        gs_ref[...] = jnp.zeros(gs_ref.shape, F32)
        for j in range(nsub):
            @pl.when(overlaps(j, lo))
            def _():
                hit = hits(j, lo)
                onehot = jnp.where(hit, 1.0, 0.0).astype(BF16)
                x_ref[...] += jnp.dot(onehot, h_ref[j * MOE_SUB:(j + 1) * MOE_SUB, :],
                                      preferred_element_type=F32)
                gate = gate_ref[0, :, j * MOE_SUB:(j + 1) * MOE_SUB]
                gs_ref[...] += jnp.sum(jnp.where(hit, gate, 0.0), axis=1, keepdims=True)
        x = x_ref[...].astype(BF16)
        hid = _silu(jnp.dot(x, wg_ref[0], preferred_element_type=F32)) * jnp.dot(
            x, wu_ref[0], preferred_element_type=F32)
        yk = jnp.dot(hid.astype(BF16), wd_ref[0], preferred_element_type=F32)
        yk = (yk * gs_ref[...]).astype(BF16)
        for j in range(nsub):
            @pl.when(overlaps(j, lo))
            def _():
                onehot = jnp.where(hits(j, lo), 1.0, 0.0).astype(BF16)
                o_ref[j * MOE_SUB:(j + 1) * MOE_SUB, :] += lax.dot_general(
                    onehot, yk, (((0,), (0,)), ((), ())), preferred_element_type=F32)
        return carry

    lax.fori_loop(0, (n + rows - 1) // rows, chunk, 0)


def _moe_ffn(h, pos, affT, starts, totals, wg, wu, wd, tb):
    T, d = h.shape
    ne, _, ff = wg.shape
    nb = T // tb
    nsub = tb // MOE_SUB
    cs = jnp.concatenate([starts[:, :nb * nsub].reshape(ne, nb, nsub), totals[:, :nb, None]], axis=2)
    cs = cs.transpose(1, 0, 2).reshape(-1)
    row = pl.BlockSpec((1, 1, tb), lambda b, e, cs: (e, 0, b))
    grid_spec = pltpu.PrefetchScalarGridSpec(
        num_scalar_prefetch=1,
        grid=(nb, ne),
        in_specs=[
            pl.BlockSpec((tb, d), lambda b, e, cs: (b, 0)),
            row, row,
            pl.BlockSpec((1, d, ff), lambda b, e, cs: (e, 0, 0)),
            pl.BlockSpec((1, d, ff), lambda b, e, cs: (e, 0, 0)),
            pl.BlockSpec((1, ff, d), lambda b, e, cs: (e, 0, 0)),
        ],
        out_specs=pl.BlockSpec((tb, d), lambda b, e, cs: (b, 0)),
        scratch_shapes=[pltpu.VMEM((MOE_ROWS, d), F32), pltpu.VMEM((MOE_ROWS, 1), F32)],
    )
    return pl.pallas_call(
        functools.partial(_moe_ffn_kernel, nsub=nsub),
        grid_spec=grid_spec,
        out_shape=jax.ShapeDtypeStruct((T, d), F32),
        compiler_params=_params("parallel", "arbitrary"),
        name="moe_ffn",
    )(cs, h, pos.reshape(ne, 1, T), affT.reshape(ne, 1, T), wg, wu, wd)


def _moe_out_kernel(y_ref, f_ref, mod_ref, g_ref, b_ref, o_ref):
    o_ref[0] = _layer_norm(ALPHA * y_ref[0] + mod_ref[0, 5:6, :] * f_ref[0], g_ref[...], b_ref[...])


def _moe_out(y, f, mod, g, b, tl):
    bsz, L, d = y.shape
    mi = _mod_index(mod)
    tok = pl.BlockSpec((1, tl, d), lambda i, j: (i, j, 0))
    vec = pl.BlockSpec((1, d), lambda i, j: (0, 0))
    return pl.pallas_call(
        _moe_out_kernel,
        grid=(bsz, L // tl),
        in_specs=[tok, tok, pl.BlockSpec((1, MOD_ROWS, d), lambda i, j: (mi(i), 0, 0)), vec, vec],
        out_specs=tok,
        out_shape=jax.ShapeDtypeStruct((bsz, L, d), F32),
        compiler_params=_params("parallel", "parallel"),
        name="moe_out",
    )(y, f, mod, g.reshape(1, d), b.reshape(1, d))


def _moe_layer(y, mod, router_w, wg, wu, wd, g, b):
    bsz, L, d = y.shape
    T = bsz * L
    ne = router_w.shape[1]
    cap = EC_CAPACITY_FACTOR * T // ne
    tb = min(T, MOE_TOKEN_BLOCK)
    h, affT = _moe_prep(y, mod, router_w, tl=min(L, 512))
    pos, starts, totals = _moe_select(affT, cap, tb)
    f = _moe_ffn(h, pos, affT, starts, totals, wg, wu, wd, tb)
    return _moe_out(y, f.reshape(bsz, L, d), mod, g, b, tl=min(L, 512))


def kernel(x_prompt, x_sample, state_ssd, c, c_ctx, ada_w, ada_b, ln_mix_g, ln_mix_b, ln_ffn_g, ln_ffn_b, conv_pw1_w, conv_pw1_b, conv_dw_w, conv_dw_b, conv_ln_g, conv_ln_b, conv_pw2_w, conv_pw2_b, ssd_in_w, ssd_conv_w, ssd_conv_b, ssd_dt_bias, ssd_a_log, ssd_d_skip, ssd_norm_w, ssd_out_w, router_w, moe_w_gate, moe_w_up, moe_w_down):
    d = x_prompt.shape[-1]
    nb_s = x_sample.shape[0]
    cond = jnp.concatenate([c_ctx[None], c, jnp.zeros((MOD_ROWS - 1 - nb_s, d), F32)], axis=0)
    mods = _ada_mod(cond, ada_w, ada_b).reshape(DEPTH, MOD_ROWS, 6, d)
    mods = jnp.pad(mods, ((0, 0), (0, 0), (0, MOD_ROWS - 6), (0, 0)))
    yp, ys = x_prompt, x_sample
    new_states = []
    for i in range(DEPTH):
        mod_p = mods[i, 0:1]
        mod_s = mods[i, 1:1 + nb_s]
        k = i // 2
        if i % 2 == 0:
            cw = (conv_pw1_w[k], conv_pw1_b[k], conv_dw_w[k], conv_dw_b[k], conv_ln_g[k],
                  conv_ln_b[k], conv_pw2_w[k], conv_pw2_b[k])
            yp = _conformer_layer(yp, mod_p, cw, "seq", ln_mix_g[i], ln_mix_b[i])
            ys = _conformer_layer(ys, mod_s, cw, "row" if k % 2 == 0 else "col", ln_mix_g[i],
                                  ln_mix_b[i])
        else:
            sw = (ssd_in_w[k], ssd_conv_w[k], ssd_conv_b[k], ssd_dt_bias[k], ssd_a_log[k],
                  ssd_d_skip[k], ssd_norm_w[k], ssd_out_w[k])
            yp, fin = _ssd_layer(yp, mod_p, sw, None, ln_mix_g[i], ln_mix_b[i])
            ys, _ = _ssd_layer(ys, mod_s, sw, state_ssd[:, k], ln_mix_g[i], ln_mix_b[i])
            new_states.append(fin.astype(x_prompt.dtype))
        wg = moe_w_gate[i].astype(BF16)
        wu = moe_w_up[i].astype(BF16)
        wd = moe_w_down[i].astype(BF16)
        yp = _moe_layer(yp, mod_p, router_w[i], wg, wu, wd, ln_ffn_g[i], ln_ffn_b[i])
        ys = _moe_layer(ys, mod_s, router_w[i], wg, wu, wd, ln_ffn_g[i], ln_ffn_b[i])
    return (yp, ys, jnp.stack(new_states, axis=1))
```

```python
import functools

import jax
import jax.numpy as jnp
from jax import lax
from jax.experimental import pallas as pl
from jax.experimental.pallas import tpu as pltpu

F32 = jnp.float32
BF16 = jnp.bfloat16
HIGHEST = lax.Precision.HIGHEST

D_MODEL = 1024
DEPTH = 4
GRID_W = 64
CONV_K = 31
SSM_INNER = 2 * D_MODEL
SSM_HEAD_DIM = 64
SSM_HEADS = SSM_INNER // SSM_HEAD_DIM
SSM_GROUPS = 4
D_STATE = 128
SSM_CONV_K = 5
SSM_BC = SSM_GROUPS * D_STATE
SSM_CONV_DIM = SSM_INNER + 2 * SSM_BC
SSD_CHUNK = 128
N_EXPERTS = 16
EC_CAPACITY_FACTOR = 2
ALPHA = (2 * DEPTH) ** 0.25
LN_EPS = 1e-5

LANES = 128
SUBLANES = 8
VMEM_LIMIT = 56 * 1024 * 1024
MOD_ROWS = 8
CONV_PAD = 16
MOE_TOKEN_BLOCK = 1024
MOE_ROWS = 160


def _params(*sem):
    return pltpu.CompilerParams(dimension_semantics=sem, vmem_limit_bytes=VMEM_LIMIT)


def _layer_norm(x, g, b):
    mu = jnp.mean(x, axis=-1, keepdims=True)
    xc = x - mu
    var = jnp.mean(xc * xc, axis=-1, keepdims=True)
    return xc * lax.rsqrt(var + LN_EPS) * g + b


def _silu(x):
    return x * jax.nn.sigmoid(x)


def _softplus(x):
    return jnp.maximum(x, 0.0) + jnp.log1p(jnp.exp(-jnp.abs(x)))


def _ada_kernel(c_ref, w_ref, b_ref, o_ref):
    x = _silu(c_ref[...])
    o_ref[0] = jnp.dot(x, w_ref[0], precision=HIGHEST, preferred_element_type=F32) + b_ref[0]


def _ada_mod(cond, ada_w, ada_b):
    depth, d, n = ada_w.shape
    tn = 1536
    return pl.pallas_call(
        _ada_kernel,
        grid=(depth, n // tn),
        in_specs=[
            pl.BlockSpec((MOD_ROWS, d), lambda i, j: (0, 0)),
            pl.BlockSpec((1, d, tn), lambda i, j: (i, 0, j)),
            pl.BlockSpec((1, 1, tn), lambda i, j: (i, 0, j)),
        ],
        out_specs=pl.BlockSpec((1, MOD_ROWS, tn), lambda i, j: (i, 0, j)),
        out_shape=jax.ShapeDtypeStruct((depth, MOD_ROWS, n), F32),
        compiler_params=_params("parallel", "parallel"),
        name="ada_mod",
    )(cond, ada_w, ada_b.reshape(depth, 1, n))


def _mod_index(mod):
    if mod.shape[0] == 1:
        return lambda b: 0
    return lambda b: b


def _pw1_glu_kernel(x_ref, mod_ref, w_ref, b_ref, o_ref):
    d = x_ref.shape[-1]
    x = x_ref[0]
    h = (x * (1.0 + mod_ref[0, 1:2, :]) + mod_ref[0, 0:1, :]).astype(BF16)
    u = jnp.dot(h, w_ref[...], preferred_element_type=F32) + b_ref[...]
    o_ref[0] = u[:, :d] * jax.nn.sigmoid(u[:, d:])


def _pw1_glu(x, mod, w, b, tl):
    bsz, L, d = x.shape
    mi = _mod_index(mod)
    return pl.pallas_call(
        _pw1_glu_kernel,
        grid=(bsz, L // tl),
        in_specs=[
            pl.BlockSpec((1, tl, d), lambda i, j: (i, j, 0)),
            pl.BlockSpec((1, MOD_ROWS, d), lambda i, j: (mi(i), 0, 0)),
            pl.BlockSpec((d, 2 * d), lambda i, j: (0, 0)),
            pl.BlockSpec((1, 2 * d), lambda i, j: (0, 0)),
        ],
        out_specs=pl.BlockSpec((1, tl, d), lambda i, j: (i, j, 0)),
        out_shape=jax.ShapeDtypeStruct((bsz, L, d), F32),
        compiler_params=_params("parallel", "parallel"),
        name="conv_pw1_glu",
    )(x, mod, w, b)


def _conv_tail(v, y, mod_ref, lng_ref, lnb_ref, w2_ref, b2_ref, mixg_ref, mixb_ref):
    v = _silu(_layer_norm(v, lng_ref[...], lnb_ref[...]))
    m = jnp.dot(v.astype(BF16), w2_ref[...], preferred_element_type=F32) + b2_ref[...]
    return _layer_norm(ALPHA * y + mod_ref[0, 2:3, :] * m, mixg_ref[...], mixb_ref[...])


def _dwconv_seq_kernel(u_ref, y_ref, mod_ref, dww_ref, dwb_ref, lng_ref, lnb_ref, w2_ref, b2_ref,
                       mixg_ref, mixb_ref, o_ref, pad_ref, sh_ref, v_ref, *, seg, nseg):
    d = u_ref.shape[-1]
    sp = seg + 2 * CONV_PAD
    n = nseg * sp
    zeros = jnp.zeros((CONV_PAD, d), F32)
    for s in range(nseg):
        pad_ref[s * sp:s * sp + CONV_PAD, :] = zeros
        pad_ref[s * sp + CONV_PAD:s * sp + CONV_PAD + seg, :] = u_ref[0, s * seg:(s + 1) * seg, :]
        pad_ref[s * sp + CONV_PAD + seg:(s + 1) * sp, :] = zeros
    first = CONV_PAD - CONV_K // 2
    rows = min(seg, 64)
    for c in range(d // LANES):
        ln = slice(c * LANES, (c + 1) * LANES)
        for q in range(1, SUBLANES):
            sh_ref[q, 0:n - SUBLANES, :] = pad_ref[q:q + n - SUBLANES, ln]
        for s in range(nseg):
            for r in range(seg // rows):
                acc = jnp.zeros((rows, LANES), F32) + dwb_ref[:, ln]
                for k in range(CONV_K):
                    q = (first + k) % SUBLANES
                    base = s * sp + r * rows + (first + k) - q
                    tap = pad_ref[base:base + rows, ln] if q == 0 else sh_ref[q, base:base + rows, :]
                    acc = acc + dww_ref[k:k + 1, ln] * tap
                v_ref[s * seg + r * rows:s * seg + (r + 1) * rows, ln] = acc
    o_ref[0] = _conv_tail(v_ref[...], y_ref[0], mod_ref, lng_ref, lnb_ref, w2_ref, b2_ref,
                          mixg_ref, mixb_ref)


def _dwconv_col_kernel(u_ref, y_ref, mod_ref, dww_ref, dwb_ref, lng_ref, lnb_ref, w2_ref, b2_ref,
                       mixg_ref, mixb_ref, o_ref, pad_ref, v_ref):
    hh, wt, d = u_ref.shape[1:]
    zeros = jnp.zeros((CONV_PAD, wt, d), F32)
    pad_ref[0:CONV_PAD] = zeros
    pad_ref[CONV_PAD:CONV_PAD + hh] = u_ref[0]
    pad_ref[CONV_PAD + hh:CONV_PAD + hh + CONV_PAD] = zeros
    first = CONV_PAD - CONV_K // 2
    rows = 32
    for r in range(hh // rows):
        for c in range(d // LANES):
            ln = slice(c * LANES, (c + 1) * LANES)
            base = first + r * rows
            acc = jnp.zeros((rows, wt, LANES), F32) + dwb_ref[:, ln]
            for k in range(CONV_K):
                acc = acc + dww_ref[k:k + 1, ln] * pad_ref[base + k:base + k + rows, :, ln]
            v_ref[r * rows:(r + 1) * rows, :, ln] = acc
    out = _conv_tail(v_ref[...].reshape(hh * wt, d), y_ref[0].reshape(hh * wt, d), mod_ref,
                     lng_ref, lnb_ref, w2_ref, b2_ref, mixg_ref, mixb_ref)
    o_ref[0] = out.reshape(hh, wt, d)


def _conv_weights_specs(d):
    vec = pl.BlockSpec((1, d), lambda i, j: (0, 0))
    return [
        pl.BlockSpec((CONV_K, d), lambda i, j: (0, 0)),
        vec, vec, vec,
        pl.BlockSpec((d, d), lambda i, j: (0, 0)),
        vec, vec, vec,
    ]


def _dwconv_seq(u, y, mod, weights, seg, nseg):
    bsz, L, d = u.shape
    tl = seg * nseg
    mi = _mod_index(mod)
    tok = pl.BlockSpec((1, tl, d), lambda i, j: (i, j, 0))
    return pl.pallas_call(
        functools.partial(_dwconv_seq_kernel, seg=seg, nseg=nseg),
        grid=(bsz, L // tl),
        in_specs=[tok, tok, pl.BlockSpec((1, MOD_ROWS, d), lambda i, j: (mi(i), 0, 0))]
        + _conv_weights_specs(d),
        out_specs=tok,
        out_shape=jax.ShapeDtypeStruct((bsz, L, d), F32),
        scratch_shapes=[pltpu.VMEM((nseg * (seg + 2 * CONV_PAD), d), F32),
                        pltpu.VMEM((SUBLANES, nseg * (seg + 2 * CONV_PAD), LANES), F32),
                        pltpu.VMEM((tl, d), F32)],
        compiler_params=_params("parallel", "parallel"),
        name="conv_dw_seq",
    )(u, y, mod, *weights)


def _dwconv_col(u, y, mod, weights):
    bsz, L, d = u.shape
    hh = L // GRID_W
    u4 = u.reshape(bsz, hh, GRID_W, d)
    y4 = y.reshape(bsz, hh, GRID_W, d)
    mi = _mod_index(mod)
    tok = pl.BlockSpec((1, hh, SUBLANES, d), lambda i, j: (i, 0, j, 0))
    out = pl.pallas_call(
        _dwconv_col_kernel,
        grid=(bsz, GRID_W // SUBLANES),
        in_specs=[tok, tok, pl.BlockSpec((1, MOD_ROWS, d), lambda i, j: (mi(i), 0, 0))]
        + _conv_weights_specs(d),
        out_specs=tok,
        out_shape=jax.ShapeDtypeStruct((bsz, hh, GRID_W, d), F32),
        scratch_shapes=[pltpu.VMEM((hh + 2 * CONV_PAD, SUBLANES, d), F32),
                        pltpu.VMEM((hh, SUBLANES, d), F32)],
        compiler_params=_params("parallel", "parallel"),
        name="conv_dw_col",
    )(u4, y4, mod, *weights)
    return out.reshape(bsz, L, d)


def _conformer_layer(y, mod, cw, axis, mixg, mixb):
    pw1_w, pw1_b, dw_w, dw_b, ln_g, ln_b, pw2_w, pw2_b = cw
    bsz, L, d = y.shape
    row = lambda a: a.reshape(1, -1)
    u = _pw1_glu(y, mod, pw1_w.astype(BF16), row(pw1_b), tl=min(L, 512))
    weights = (dw_w, row(dw_b), row(ln_g), row(ln_b), pw2_w.astype(BF16), row(pw2_b),
               row(mixg), row(mixb))
    if axis == "seq":
        return _dwconv_seq(u, y, mod, weights, seg=L, nseg=1)
    if axis == "row":
        return _dwconv_seq(u, y, mod, weights, seg=GRID_W, nseg=min(8, L // GRID_W))
    return _dwconv_col(u, y, mod, weights)


def _ssd_in_kernel(x_ref, mod_ref, wz_ref, wx_ref, wdt_ref, z_ref, xbc_ref, dt_ref):
    x = x_ref[0]
    h32 = x * (1.0 + mod_ref[0, 1:2, :]) + mod_ref[0, 0:1, :]
    h = h32.astype(BF16)
    z_ref[0] = jnp.dot(h, wz_ref[...], preferred_element_type=F32).astype(z_ref.dtype)
    xbc_ref[0] = jnp.dot(h, wx_ref[...], preferred_element_type=F32).astype(xbc_ref.dtype)
    h_lo = (h32 - h.astype(F32)).astype(BF16)
    ndt = dt_ref.shape[-1]
    hi = jnp.dot(h, wdt_ref[...], preferred_element_type=F32)
    dt_ref[0] = hi[:, :ndt] + hi[:, ndt:] + jnp.dot(h_lo, wdt_ref[:, :ndt], preferred_element_type=F32)


def _ssd_in_proj(y, mod, wz, wx, wdt, tl):
    bsz, L, d = y.shape
    mi = _mod_index(mod)
    ndt = wdt.shape[1]
    full = lambda a: pl.BlockSpec(a.shape, lambda i, j: (0, 0))
    wdt_hi = wdt.astype(BF16)
    wdt2 = jnp.concatenate([wdt_hi, (wdt - wdt_hi.astype(F32)).astype(BF16)], axis=1)
    return pl.pallas_call(
        _ssd_in_kernel,
        grid=(bsz, L // tl),
        in_specs=[
            pl.BlockSpec((1, tl, d), lambda i, j: (i, j, 0)),
            pl.BlockSpec((1, MOD_ROWS, d), lambda i, j: (mi(i), 0, 0)),
            full(wz), full(wx), full(wdt2),
        ],
        out_specs=[
            pl.BlockSpec((1, tl, SSM_INNER), lambda i, j: (i, j, 0)),
            pl.BlockSpec((1, tl, SSM_CONV_DIM), lambda i, j: (i, j, 0)),
            pl.BlockSpec((1, tl, ndt), lambda i, j: (i, j, 0)),
        ],
        out_shape=[
            jax.ShapeDtypeStruct((bsz, L, SSM_INNER), BF16),
            jax.ShapeDtypeStruct((bsz, L, SSM_CONV_DIM), BF16),
            jax.ShapeDtypeStruct((bsz, L, ndt), F32),
        ],
        compiler_params=_params("parallel", "parallel"),
        name="ssd_in_proj",
    )(y, mod, wz, wx, wdt2)


def _ssd_conv_kernel(x_ref, w_ref, b_ref, o_ref, pad_ref):
    L, cb = x_ref.shape[1:]
    zeros = jnp.zeros((SUBLANES, cb), F32)
    pad_ref[0:SUBLANES, :] = zeros
    pad_ref[SUBLANES:SUBLANES + L, :] = x_ref[0].astype(F32)
    pad_ref[SUBLANES + L:2 * SUBLANES + L, :] = zeros
    first = SUBLANES - SSM_CONV_K // 2
    rows = min(L, 256)
    for r in range(L // rows):
        for c in range(cb // LANES):
            ln = slice(c * LANES, (c + 1) * LANES)
            base = first + r * rows
            acc = jnp.zeros((rows, LANES), F32) + b_ref[:, ln]
            for k in range(SSM_CONV_K):
                acc = acc + w_ref[k:k + 1, ln] * pad_ref[base + k:base + k + rows, ln]
            o_ref[0, r * rows:(r + 1) * rows, ln] = _silu(acc).astype(o_ref.dtype)


def _ssd_conv(xbc, w, b):
    bsz, L, cd = xbc.shape
    cb = max(w for w in range(LANES, cd + 1, LANES) if cd % w == 0 and (w * L <= (1 << 20) or w == LANES))
    return pl.pallas_call(
        _ssd_conv_kernel,
        grid=(bsz, cd // cb),
        in_specs=[
            pl.BlockSpec((1, L, cb), lambda i, j: (i, 0, j)),
            pl.BlockSpec((SSM_CONV_K, cb), lambda i, j: (0, j)),
            pl.BlockSpec((1, cb), lambda i, j: (0, j)),
        ],
        out_specs=pl.BlockSpec((1, L, cb), lambda i, j: (i, 0, j)),
        out_shape=jax.ShapeDtypeStruct((bsz, L, cd), BF16),
        scratch_shapes=[pltpu.VMEM((L + 2 * SUBLANES, cb), F32)],
        compiler_params=_params("parallel", "parallel"),
        name="ssd_conv",
    )(xbc, w, b.reshape(1, cd))


def _split3(x):
    hi = x.astype(BF16)
    r1 = x - hi.astype(F32)
    mid = r1.astype(BF16)
    lo = (r1 - mid.astype(F32)).astype(BF16)
    return jnp.concatenate([hi, mid, lo], axis=1)


def _ssd_scan_kernel(xs_ref, b_ref, c_ref, dt_ref, bias_row_ref, alog_row_ref, tri_ref, e3_ref, init_ref,
                     y_ref, fin_ref, s_ref):
    d = pl.program_id(0)
    c = pl.program_id(2)
    q = SSD_CHUNK
    nh = SSM_HEADS
    gw = SSM_INNER // SSM_GROUPS

    @pl.when(c == 0)
    def _():
        s_ref[...] = init_ref[0, 0]

    is_f = d == 0
    tri = tri_ref[0]
    mask = tri > 0.5
    dt2 = _softplus(dt_ref[0] + bias_row_ref[...])
    a2 = dt2 * -jnp.exp(alog_row_ref[...])
    dt2T = dt2.T
    a2T = a2.T
    cum2 = jnp.dot(tri, a2, precision=HIGHEST, preferred_element_type=F32)
    cumT2 = lax.dot_general(a2T, tri, (((1,), (1,)), ((), ())), precision=HIGHEST,
                            preferred_element_type=F32)
    cum = jnp.where(is_f, cum2[:, :nh], cum2[:, nh:2 * nh])
    cumT = jnp.where(is_f, cumT2[:nh], cumT2[nh:2 * nh])
    dtT = jnp.where(is_f, dt2T[:nh], dt2T[nh:2 * nh])
    tot2 = jnp.where(is_f, cum2[q - 1:q], cum2[0:1])
    small = jnp.concatenate([dt2 * jnp.exp(jnp.minimum(tot2 - cum2, 0.0)), jnp.exp(cum2),
                             jnp.broadcast_to(jnp.exp(tot2), (SUBLANES, LANES))], axis=0)
    wide = jnp.dot(_split3(small), e3_ref[0], preferred_element_type=F32)
    w_state = wide[0:q]
    w_off = wide[q:2 * q]
    w_tot = wide[2 * q:2 * q + 1]

    xs_b = xs_ref[0]
    xs = xs_b.astype(F32)
    xdec = (xs * w_state).astype(BF16)
    lane = lax.broadcasted_iota(jnp.int32, (q, LANES), 1)
    for g in range(SSM_GROUPS):
        gs = slice(g * gw, (g + 1) * gw)
        bg = b_ref[0, :, g * D_STATE:(g + 1) * D_STATE]
        cg = c_ref[0, :, g * D_STATE:(g + 1) * D_STATE]
        cb = lax.dot_general(cg, bg, (((1,), (1,)), ((), ())), preferred_element_type=F32)
        s_g = s_ref[:, gs]
        y_off = jnp.dot(cg, s_g.astype(BF16), preferred_element_type=F32) * w_off[:, gs]
        s_ref[:, gs] = w_tot[:, gs] * s_g + jnp.dot(bg.astype(F32).T.astype(BF16), xdec[:, gs],
                                                    preferred_element_type=F32)
        pairs = gw // LANES
        for p in range(pairs):
            ls = slice(g * gw + p * LANES, g * gw + (p + 1) * LANES)
            x_pair = xs_b[:, ls]
            outs = []
            for j in range(LANES // SSM_HEAD_DIM):
                h = (g * gw + p * LANES) // SSM_HEAD_DIM + j
                diff = cum[:, h:h + 1] - cumT[h:h + 1, :]
                lm = jnp.exp(jnp.where(mask, diff, -1e30))
                m = (cb * lm * dtT[h:h + 1, :]).astype(BF16)
                outs.append(jnp.dot(m, x_pair, preferred_element_type=F32))
            y_diag = jnp.where(lane < SSM_HEAD_DIM, outs[0], outs[1])
            y_ref[0, 0, :, ls] = (y_diag + y_off[:, p * LANES:(p + 1) * LANES]).astype(y_ref.dtype)

    @pl.when(c == pl.num_programs(2) - 1)
    def _():
        fin_ref[0, 0] = s_ref[...]


def _ssd_scan(xbc, dt_raw, dt_bias, a_log, init):
    bsz, L, _ = xbc.shape
    nc = L // SSD_CHUNK
    q = SSD_CHUNK
    ndt = LANES
    chunk = lambda d, c: c + d * (nc - 1 - 2 * c)
    r = jnp.arange(q)
    tri = jnp.stack([r[:, None] >= r[None, :], r[:, None] <= r[None, :]]).astype(F32)
    src = jnp.arange(ndt)[None, :, None] - SSM_HEADS * jnp.arange(2)[:, None, None]
    e1 = (src == (jnp.arange(SSM_INNER) // SSM_HEAD_DIM)[None, None, :]).astype(BF16)
    e3 = jnp.concatenate([e1, e1, e1], axis=1)
    pad = lambda a: jnp.pad(a.reshape(-1), (0, ndt - a.size))
    dt_bias, a_log = pad(dt_bias), pad(a_log)
    nblk = SSM_INNER // SSM_BC
    const = lambda shape: pl.BlockSpec(shape, lambda d, b, c: tuple(0 for _ in shape))
    return pl.pallas_call(
        _ssd_scan_kernel,
        grid=(2, bsz, nc),
        in_specs=[
            pl.BlockSpec((1, q, SSM_INNER), lambda d, b, c: (b, chunk(d, c), 0)),
            pl.BlockSpec((1, q, SSM_BC), lambda d, b, c: (b, chunk(d, c), nblk)),
            pl.BlockSpec((1, q, SSM_BC), lambda d, b, c: (b, chunk(d, c), nblk + 1)),
            pl.BlockSpec((1, q, ndt), lambda d, b, c: (b, chunk(d, c), 0)),
            const((1, ndt)), const((1, ndt)),
            pl.BlockSpec((1, q, q), lambda d, b, c: (d, 0, 0)),
            pl.BlockSpec((1, 3 * ndt, SSM_INNER), lambda d, b, c: (d, 0, 0)),
            pl.BlockSpec((1, 1, D_STATE, SSM_INNER), lambda d, b, c: (d, b, 0, 0)),
        ],
        out_specs=[
            pl.BlockSpec((1, 1, q, SSM_INNER), lambda d, b, c: (d, b, chunk(d, c), 0)),
            pl.BlockSpec((1, 1, D_STATE, SSM_INNER), lambda d, b, c: (d, b, 0, 0)),
        ],
        out_shape=[
            jax.ShapeDtypeStruct((2, bsz, L, SSM_INNER), BF16),
            jax.ShapeDtypeStruct((2, bsz, D_STATE, SSM_INNER), F32),
        ],
        scratch_shapes=[pltpu.VMEM((D_STATE, SSM_INNER), F32)],
        compiler_params=_params("parallel", "parallel", "arbitrary"),
        name="ssd_scan",
    )(xbc, xbc, xbc, dt_raw, dt_bias.reshape(1, ndt), a_log.reshape(1, ndt), tri, e3, init)


def _ssd_out_kernel(yf_ref, yb_ref, xs_ref, z_ref, res_ref, mod_ref, dskip_ref, nw_ref, wo_ref,
                    mixg_ref, mixb_ref, o_ref):
    y = yf_ref[0, 0].astype(F32) + yb_ref[0, 0].astype(F32) + dskip_ref[...] * xs_ref[0].astype(F32)
    y = y * _silu(z_ref[0].astype(F32))
    gw = SSM_INNER // SSM_GROUPS
    parts = []
    for g in range(SSM_GROUPS):
        yg = y[:, g * gw:(g + 1) * gw]
        ms = jnp.mean(yg * yg, axis=-1, keepdims=True)
        parts.append(yg * lax.rsqrt(ms + LN_EPS))
    yn = jnp.concatenate(parts, axis=1) * nw_ref[...]
    m = jnp.dot(yn.astype(BF16), wo_ref[...], preferred_element_type=F32)
    o_ref[0] = _layer_norm(ALPHA * res_ref[0] + mod_ref[0, 2:3, :] * m, mixg_ref[...], mixb_ref[...])


def _ssd_out(yscan, xbc, z, y, mod, d_skip, norm_w, out_w, mixg, mixb, tl):
    bsz, L, d = y.shape
    mi = _mod_index(mod)
    inner = pl.BlockSpec((1, tl, SSM_INNER), lambda i, j: (i, j, 0))
    vec = lambda n: pl.BlockSpec((1, n), lambda i, j: (0, 0))
    return pl.pallas_call(
        _ssd_out_kernel,
        grid=(bsz, L // tl),
        in_specs=[
            pl.BlockSpec((1, 1, tl, SSM_INNER), lambda i, j: (0, i, j, 0)),
            pl.BlockSpec((1, 1, tl, SSM_INNER), lambda i, j: (1, i, j, 0)),
            inner, inner,
            pl.BlockSpec((1, tl, d), lambda i, j: (i, j, 0)),
            pl.BlockSpec((1, MOD_ROWS, d), lambda i, j: (mi(i), 0, 0)),
            vec(SSM_INNER), vec(SSM_INNER),
            pl.BlockSpec((SSM_INNER, d), lambda i, j: (0, 0)),
            vec(d), vec(d),
        ],
        out_specs=pl.BlockSpec((1, tl, d), lambda i, j: (i, j, 0)),
        out_shape=jax.ShapeDtypeStruct((bsz, L, d), F32),
        compiler_params=_params("parallel", "parallel"),
        name="ssd_out",
    )(yscan, yscan, xbc, z, y, mod, jnp.repeat(d_skip, SSM_HEAD_DIM).reshape(1, SSM_INNER),
      norm_w.reshape(1, SSM_INNER), out_w, mixg.reshape(1, d), mixb.reshape(1, d))


def _ssd_layer(y, mod, sw, init, mixg, mixb):
    in_w, conv_w, conv_b, dt_bias, a_log, d_skip, norm_w, out_w = sw
    bsz, L, d = y.shape
    wz = in_w[:, :SSM_INNER].astype(BF16)
    wx = in_w[:, SSM_INNER:SSM_INNER + SSM_CONV_DIM].astype(BF16)
    wdt = in_w[:, SSM_INNER + SSM_CONV_DIM:]
    wdt = jnp.pad(wdt, ((0, 0), (0, LANES - wdt.shape[1])))
    z, xbc, dt_raw = _ssd_in_proj(y, mod, wz, wx, wdt, tl=min(L, 256))
    xbc = _ssd_conv(xbc, conv_w, conv_b)
    if init is None:
        init_t = jnp.zeros((2, bsz, D_STATE, SSM_INNER), F32)
    else:
        init_t = init.astype(F32).transpose(1, 0, 4, 2, 3).reshape(2, bsz, D_STATE, SSM_INNER)
    yscan, fin = _ssd_scan(xbc, dt_raw, dt_bias, a_log, init_t)
    out = _ssd_out(yscan, xbc, z, y, mod, d_skip, norm_w, out_w.astype(BF16), mixg, mixb,
                   tl=min(L, 256))
    fin = fin.reshape(2, bsz, D_STATE, SSM_HEADS, SSM_HEAD_DIM).transpose(1, 0, 3, 4, 2)
    return out, fin


def _moe_prep_kernel(y_ref, mod_ref, rwT_ref, h_ref, affT_ref):
    h32 = y_ref[0] * (1.0 + mod_ref[0, 4:5, :]) + mod_ref[0, 3:4, :]
    h_ref[...] = h32.astype(BF16)
    logT = lax.dot_general(rwT_ref[...], h32, (((1,), (1,)), ((), ())), precision=HIGHEST,
                           preferred_element_type=F32)
    ex = jnp.exp(logT - jnp.max(logT, axis=0, keepdims=True))
    affT_ref[...] = ex / jnp.sum(ex, axis=0, keepdims=True)


def _moe_prep(y, mod, router_w, tl):
    bsz, L, d = y.shape
    mi = _mod_index(mod)
    nt = L // tl
    ne = router_w.shape[1]
    return pl.pallas_call(
        _moe_prep_kernel,
        grid=(bsz, nt),
        in_specs=[
            pl.BlockSpec((1, tl, d), lambda i, j: (i, j, 0)),
            pl.BlockSpec((1, MOD_ROWS, d), lambda i, j: (mi(i), 0, 0)),
            pl.BlockSpec((ne, d), lambda i, j: (0, 0)),
        ],
        out_specs=[
            pl.BlockSpec((tl, d), lambda i, j: (i * nt + j, 0)),
            pl.BlockSpec((ne, tl), lambda i, j: (0, i * nt + j)),
        ],
        out_shape=[
            jax.ShapeDtypeStruct((bsz * L, d), BF16),
            jax.ShapeDtypeStruct((ne, bsz * L), F32),
        ],
        compiler_params=_params("parallel", "parallel"),
        name="moe_prep",
    )(y, mod, router_w.T)


def _moe_select_kernel(aff_ref, upper_ref, pos_ref, total_ref, *, cap, tb):
    ne, T = aff_ref.shape
    bits = lax.bitcast_convert_type(aff_ref[...], jnp.int32)

    def search(i, v):
        cand = v | lax.shift_left(jnp.int32(1), 30 - i)
        cnt = jnp.sum(jnp.where(bits >= cand, 1.0, 0.0), axis=1, keepdims=True)
        return jnp.where(cnt >= cap, cand, v)

    thr = lax.fori_loop(0, 31, search, jnp.zeros((ne, 1), jnp.int32))
    need = cap - jnp.sum(jnp.where(bits > thr, 1.0, 0.0), axis=1, keepdims=True)
    upper = upper_ref[...]
    lane = lax.broadcasted_iota(jnp.int32, (ne, LANES), 1)
    tie_rank = jnp.zeros((ne, 1), F32)
    slot = jnp.zeros((ne, 1), F32)
    totals = jnp.zeros((ne, LANES), F32)
    for j in range(T // LANES):
        t0 = j * LANES
        blk = bits[:, t0:t0 + LANES]
        eq = jnp.where(blk == thr, 1.0, 0.0)
        eq_incl = jnp.dot(eq.astype(BF16), upper, preferred_element_type=F32)
        sel = (blk > thr) | ((blk == thr) & (eq_incl - eq + tie_rank < need))
        tie_rank = tie_rank + eq_incl[:, LANES - 1:LANES]
        m = jnp.where(sel, 1.0, 0.0)
        m_incl = jnp.dot(m.astype(BF16), upper, preferred_element_type=F32)
        pos_ref[:, t0:t0 + LANES] = jnp.where(sel, m_incl - m + slot, -1.0).astype(jnp.int32)
        slot = slot + m_incl[:, LANES - 1:LANES]
        if (t0 + LANES) % tb == 0:
            totals = jnp.where(lane == t0 // tb, slot, totals)
            slot = jnp.zeros((ne, 1), F32)
    total_ref[...] = totals.astype(jnp.int32)


def _moe_select(affT, cap, tb):
    ne, T = affT.shape
    r = jnp.arange(LANES)
    upper = (r[:, None] <= r[None, :]).astype(BF16)
    return pl.pallas_call(
        functools.partial(_moe_select_kernel, cap=cap, tb=tb),
        grid=(1,),
        in_specs=[pl.BlockSpec((ne, T), lambda i: (0, 0)), pl.BlockSpec((LANES, LANES), lambda i: (0, 0))],
        out_specs=[pl.BlockSpec((ne, T), lambda i: (0, 0)), pl.BlockSpec((ne, LANES), lambda i: (0, 0))],
        out_shape=[jax.ShapeDtypeStruct((ne, T), jnp.int32), jax.ShapeDtypeStruct((ne, LANES), jnp.int32)],
        compiler_params=_params("arbitrary"),
        name="moe_select",
    )(affT, upper)


def _moe_ffn_kernel(tot_ref, h_ref, pos_ref, gate_ref, wg_ref, wu_ref, wd_ref, y_ref, mod_ref, lng_ref,
                    lnb_ref, o_ref):
    b = pl.program_id(0)
    e = pl.program_id(1)
    ne = pl.num_programs(1)
    rows = MOE_ROWS
    tb = h_ref.shape[0]

    @pl.when(e == 0)
    def _():
        o_ref[...] = jnp.zeros(o_ref.shape, F32)

    slot_iota = lax.broadcasted_iota(jnp.int32, (rows, tb), 0)

    def chunk(k, carry):
        hit = pos_ref[0] == slot_iota + k * rows
        onehot = jnp.where(hit, 1.0, 0.0).astype(BF16)
        x = jnp.dot(onehot, h_ref[...], preferred_element_type=F32).astype(BF16)
        gate = jnp.sum(jnp.where(hit, gate_ref[0], 0.0), axis=1, keepdims=True)
        hid = _silu(jnp.dot(x, wg_ref[0], preferred_element_type=F32)) * jnp.dot(
            x, wu_ref[0], preferred_element_type=F32)
        yk = jnp.dot(hid.astype(BF16), wd_ref[0], preferred_element_type=F32)
        yk = (yk * gate).astype(BF16)
        o_ref[...] += lax.dot_general(onehot, yk, (((0,), (0,)), ((), ())),
                                      preferred_element_type=F32)
        return carry

    lax.fori_loop(0, (tot_ref[b * ne + e] + rows - 1) // rows, chunk, 0)

    @pl.when(e == ne - 1)
    def _():
        o_ref[...] = _layer_norm(ALPHA * y_ref[...] + mod_ref[0, 5:6, :] * o_ref[...], lng_ref[...],
                                 lnb_ref[...])


def _moe_ffn(y, h, pos, affT, totals, mod, wg, wu, wd, g, b, tb):
    bsz, L, d = y.shape
    T = bsz * L
    ne, _, ff = wg.shape
    nb = T // tb
    tot = totals[:, :nb].T.reshape(-1)
    per_batch = mod.shape[0] > 1
    assert not per_batch or L % tb == 0
    mi = (lambda i: (i * tb) // L) if per_batch else (lambda i: 0)
    row = pl.BlockSpec((1, 1, tb), lambda i, e, t: (e, 0, i))
    tok = pl.BlockSpec((tb, d), lambda i, e, t: (i, 0))
    vec = pl.BlockSpec((1, d), lambda i, e, t: (0, 0))
    grid_spec = pltpu.PrefetchScalarGridSpec(
        num_scalar_prefetch=1,
        grid=(nb, ne),
        in_specs=[
            tok, row, row,
            pl.BlockSpec((1, d, ff), lambda i, e, t: (e, 0, 0)),
            pl.BlockSpec((1, d, ff), lambda i, e, t: (e, 0, 0)),
            pl.BlockSpec((1, ff, d), lambda i, e, t: (e, 0, 0)),
            tok,
            pl.BlockSpec((1, MOD_ROWS, d), lambda i, e, t: (mi(i), 0, 0)),
            vec, vec,
        ],
        out_specs=tok,
    )
    out = pl.pallas_call(
        _moe_ffn_kernel,
        grid_spec=grid_spec,
        out_shape=jax.ShapeDtypeStruct((T, d), F32),
        compiler_params=_params("parallel", "arbitrary"),
        name="moe_ffn",
    )(tot, h, pos.reshape(ne, 1, T), affT.reshape(ne, 1, T), wg, wu, wd, y.reshape(T, d), mod,
      g.reshape(1, d), b.reshape(1, d))
    return out.reshape(bsz, L, d)


def _moe_layer(y, mod, router_w, wg, wu, wd, g, b):
    bsz, L, d = y.shape
    T = bsz * L
    ne = router_w.shape[1]
    cap = EC_CAPACITY_FACTOR * T // ne
    tb = min(T, MOE_TOKEN_BLOCK)
    h, affT = _moe_prep(y, mod, router_w, tl=min(L, 512))
    pos, totals = _moe_select(affT, cap, tb)
    return _moe_ffn(y, h, pos, affT, totals, mod, wg, wu, wd, g, b, tb)


def _cast_kernel(x_ref, o_ref):
    o_ref[...] = x_ref[...].astype(o_ref.dtype)


def _cast_bf16(w):
    depth, ne, r, c = w.shape
    blk = pl.BlockSpec((1, 1, r, c), lambda i, j: (i, j, 0, 0))
    return pl.pallas_call(
        _cast_kernel,
        grid=(depth, ne),
        in_specs=[blk],
        out_specs=blk,
        out_shape=jax.ShapeDtypeStruct(w.shape, BF16),
        compiler_params=_params("parallel", "parallel"),
        name="cast_bf16",
    )(w)


def kernel(x_prompt, x_sample, state_ssd, c, c_ctx, ada_w, ada_b, ln_mix_g, ln_mix_b, ln_ffn_g, ln_ffn_b, conv_pw1_w, conv_pw1_b, conv_dw_w, conv_dw_b, conv_ln_g, conv_ln_b, conv_pw2_w, conv_pw2_b, ssd_in_w, ssd_conv_w, ssd_conv_b, ssd_dt_bias, ssd_a_log, ssd_d_skip, ssd_norm_w, ssd_out_w, router_w, moe_w_gate, moe_w_up, moe_w_down):
    d = x_prompt.shape[-1]
    nb_s = x_sample.shape[0]
    cond = jnp.concatenate([c_ctx[None], c, jnp.zeros((MOD_ROWS - 1 - nb_s, d), F32)], axis=0)
    mods = _ada_mod(cond, ada_w, ada_b).reshape(DEPTH, MOD_ROWS, 6, d)
    mods = jnp.pad(mods, ((0, 0), (0, 0), (0, MOD_ROWS - 6), (0, 0)))
    w_gate, w_up, w_down = _cast_bf16(moe_w_gate), _cast_bf16(moe_w_up), _cast_bf16(moe_w_down)
    yp, ys = x_prompt, x_sample
    new_states = []
    for i in range(DEPTH):
        mod_p = mods[i, 0:1]
        mod_s = mods[i, 1:1 + nb_s]
        k = i // 2
        if i % 2 == 0:
            cw = (conv_pw1_w[k], conv_pw1_b[k], conv_dw_w[k], conv_dw_b[k], conv_ln_g[k],
                  conv_ln_b[k], conv_pw2_w[k], conv_pw2_b[k])
            yp = _conformer_layer(yp, mod_p, cw, "seq", ln_mix_g[i], ln_mix_b[i])
            ys = _conformer_layer(ys, mod_s, cw, "row" if k % 2 == 0 else "col", ln_mix_g[i],
                                  ln_mix_b[i])
        else:
            sw = (ssd_in_w[k], ssd_conv_w[k], ssd_conv_b[k], ssd_dt_bias[k], ssd_a_log[k],
                  ssd_d_skip[k], ssd_norm_w[k], ssd_out_w[k])
            yp, fin = _ssd_layer(yp, mod_p, sw, None, ln_mix_g[i], ln_mix_b[i])
            ys, _ = _ssd_layer(ys, mod_s, sw, state_ssd[:, k], ln_mix_g[i], ln_mix_b[i])
            new_states.append(fin.astype(x_prompt.dtype))
        wg, wu, wd = w_gate[i], w_up[i], w_down[i]
        yp = _moe_layer(yp, mod_p, router_w[i], wg, wu, wd, ln_ffn_g[i], ln_ffn_b[i])
        ys = _moe_layer(ys, mod_s, router_w[i], wg, wu, wd, ln_ffn_g[i], ln_ffn_b[i])
    return (yp, ys, jnp.stack(new_states, axis=1))
```

```python
import functools

import jax
import jax.numpy as jnp
from jax import lax
from jax.experimental import pallas as pl
from jax.experimental.pallas import tpu as pltpu

F32 = jnp.float32
BF16 = jnp.bfloat16
HIGHEST = lax.Precision.HIGHEST

D_MODEL = 1024
DEPTH = 4
GRID_W = 64
CONV_K = 31
SSM_INNER = 2 * D_MODEL
SSM_HEAD_DIM = 64
SSM_HEADS = SSM_INNER // SSM_HEAD_DIM
SSM_GROUPS = 4
D_STATE = 128
SSM_CONV_K = 5
SSM_BC = SSM_GROUPS * D_STATE
SSM_CONV_DIM = SSM_INNER + 2 * SSM_BC
SSD_CHUNK = 128
N_EXPERTS = 16
EC_CAPACITY_FACTOR = 2
ALPHA = (2 * DEPTH) ** 0.25
LN_EPS = 1e-5

LANES = 128
SUBLANES = 8
VMEM_LIMIT = 56 * 1024 * 1024
MOD_ROWS = 8
CONV_PAD = 16
MOE_TOKEN_BLOCK = 1024
MOE_ROWS = 160
MOE_GROUP = 16


def _params(*sem):
    return pltpu.CompilerParams(dimension_semantics=sem, vmem_limit_bytes=VMEM_LIMIT)


def _layer_norm(x, g, b):
    mu = jnp.mean(x, axis=-1, keepdims=True)
    xc = x - mu
    var = jnp.mean(xc * xc, axis=-1, keepdims=True)
    return xc * lax.rsqrt(var + LN_EPS) * g + b


def _silu(x):
    return x * jax.nn.sigmoid(x)


def _softplus(x):
    return jnp.maximum(x, 0.0) + jnp.log1p(jnp.exp(-jnp.abs(x)))


def _ada_kernel(c_ref, w_ref, b_ref, o_ref):
    x = _silu(c_ref[...])
    o_ref[0] = jnp.dot(x, w_ref[0], precision=HIGHEST, preferred_element_type=F32) + b_ref[0]


def _ada_mod(cond, ada_w, ada_b):
    depth, d, n = ada_w.shape
    tn = 1536
    return pl.pallas_call(
        _ada_kernel,
        grid=(depth, n // tn),
        in_specs=[
            pl.BlockSpec((MOD_ROWS, d), lambda i, j: (0, 0)),
            pl.BlockSpec((1, d, tn), lambda i, j: (i, 0, j)),
            pl.BlockSpec((1, 1, tn), lambda i, j: (i, 0, j)),
        ],
        out_specs=pl.BlockSpec((1, MOD_ROWS, tn), lambda i, j: (i, 0, j)),
        out_shape=jax.ShapeDtypeStruct((depth, MOD_ROWS, n), F32),
        compiler_params=_params("parallel", "parallel"),
        name="ada_mod",
    )(cond, ada_w, ada_b.reshape(depth, 1, n))


def _mod_index(mod):
    if mod.shape[0] == 1:
        return lambda b: 0
    return lambda b: b


def _pw1_glu_kernel(x_ref, mod_ref, w_ref, b_ref, o_ref):
    d = x_ref.shape[-1]
    x = x_ref[0]
    h = (x * (1.0 + mod_ref[0, 1:2, :]) + mod_ref[0, 0:1, :]).astype(BF16)
    u = jnp.dot(h, w_ref[...], preferred_element_type=F32) + b_ref[...]
    o_ref[0] = u[:, :d] * jax.nn.sigmoid(u[:, d:])


def _pw1_glu(x, mod, w, b, tl):
    bsz, L, d = x.shape
    mi = _mod_index(mod)
    return pl.pallas_call(
        _pw1_glu_kernel,
        grid=(bsz, L // tl),
        in_specs=[
            pl.BlockSpec((1, tl, d), lambda i, j: (i, j, 0)),
            pl.BlockSpec((1, MOD_ROWS, d), lambda i, j: (mi(i), 0, 0)),
            pl.BlockSpec((d, 2 * d), lambda i, j: (0, 0)),
            pl.BlockSpec((1, 2 * d), lambda i, j: (0, 0)),
        ],
        out_specs=pl.BlockSpec((1, tl, d), lambda i, j: (i, j, 0)),
        out_shape=jax.ShapeDtypeStruct((bsz, L, d), F32),
        compiler_params=_params("parallel", "parallel"),
        name="conv_pw1_glu",
    )(x, mod, w, b)


def _conv_tail(v, y, mod_ref, lng_ref, lnb_ref, w2_ref, b2_ref, mixg_ref, mixb_ref):
    v = _silu(_layer_norm(v, lng_ref[...], lnb_ref[...]))
    m = jnp.dot(v.astype(BF16), w2_ref[...], preferred_element_type=F32) + b2_ref[...]
    return _layer_norm(ALPHA * y + mod_ref[0, 2:3, :] * m, mixg_ref[...], mixb_ref[...])


def _dwconv_seq_kernel(u_ref, y_ref, mod_ref, dww_ref, dwb_ref, lng_ref, lnb_ref, w2_ref, b2_ref,
                       mixg_ref, mixb_ref, o_ref, pad_ref, sh_ref, v_ref, *, seg, nseg):
    d = u_ref.shape[-1]
    sp = seg + 2 * CONV_PAD
    n = nseg * sp
    zeros = jnp.zeros((CONV_PAD, d), F32)
    for s in range(nseg):
        pad_ref[s * sp:s * sp + CONV_PAD, :] = zeros
        pad_ref[s * sp + CONV_PAD:s * sp + CONV_PAD + seg, :] = u_ref[0, s * seg:(s + 1) * seg, :]
        pad_ref[s * sp + CONV_PAD + seg:(s + 1) * sp, :] = zeros
    first = CONV_PAD - CONV_K // 2
    rows = min(seg, 64)
    for c in range(d // LANES):
        ln = slice(c * LANES, (c + 1) * LANES)
        for q in range(1, SUBLANES):
            sh_ref[q, 0:n - SUBLANES, :] = pad_ref[q:q + n - SUBLANES, ln]
        for s in range(nseg):
            for r in range(seg // rows):
                acc = jnp.zeros((rows, LANES), F32) + dwb_ref[:, ln]
                for k in range(CONV_K):
                    q = (first + k) % SUBLANES
                    base = s * sp + r * rows + (first + k) - q
                    tap = pad_ref[base:base + rows, ln] if q == 0 else sh_ref[q, base:base + rows, :]
                    acc = acc + dww_ref[k:k + 1, ln] * tap
                v_ref[s * seg + r * rows:s * seg + (r + 1) * rows, ln] = acc
    o_ref[0] = _conv_tail(v_ref[...], y_ref[0], mod_ref, lng_ref, lnb_ref, w2_ref, b2_ref,
                          mixg_ref, mixb_ref)


def _dwconv_col_kernel(u_ref, y_ref, mod_ref, dww_ref, dwb_ref, lng_ref, lnb_ref, w2_ref, b2_ref,
                       mixg_ref, mixb_ref, o_ref, pad_ref, v_ref):
    hh, wt, d = u_ref.shape[1:]
    zeros = jnp.zeros((CONV_PAD, wt, d), F32)
    pad_ref[0:CONV_PAD] = zeros
    pad_ref[CONV_PAD:CONV_PAD + hh] = u_ref[0]
    pad_ref[CONV_PAD + hh:CONV_PAD + hh + CONV_PAD] = zeros
    first = CONV_PAD - CONV_K // 2
    rows = 32
    for r in range(hh // rows):
        for c in range(d // LANES):
            ln = slice(c * LANES, (c + 1) * LANES)
            base = first + r * rows
            acc = jnp.zeros((rows, wt, LANES), F32) + dwb_ref[:, ln]
            for k in range(CONV_K):
                acc = acc + dww_ref[k:k + 1, ln] * pad_ref[base + k:base + k + rows, :, ln]
            v_ref[r * rows:(r + 1) * rows, :, ln] = acc
    out = _conv_tail(v_ref[...].reshape(hh * wt, d), y_ref[0].reshape(hh * wt, d), mod_ref,
                     lng_ref, lnb_ref, w2_ref, b2_ref, mixg_ref, mixb_ref)
    o_ref[0] = out.reshape(hh, wt, d)


def _conv_weights_specs(d):
    vec = pl.BlockSpec((1, d), lambda i, j: (0, 0))
    return [
        pl.BlockSpec((CONV_K, d), lambda i, j: (0, 0)),
        vec, vec, vec,
        pl.BlockSpec((d, d), lambda i, j: (0, 0)),
        vec, vec, vec,
    ]


def _dwconv_seq(u, y, mod, weights, seg, nseg):
    bsz, L, d = u.shape
    tl = seg * nseg
    mi = _mod_index(mod)
    tok = pl.BlockSpec((1, tl, d), lambda i, j: (i, j, 0))
    return pl.pallas_call(
        functools.partial(_dwconv_seq_kernel, seg=seg, nseg=nseg),
        grid=(bsz, L // tl),
        in_specs=[tok, tok, pl.BlockSpec((1, MOD_ROWS, d), lambda i, j: (mi(i), 0, 0))]
        + _conv_weights_specs(d),
        out_specs=tok,
        out_shape=jax.ShapeDtypeStruct((bsz, L, d), F32),
        scratch_shapes=[pltpu.VMEM((nseg * (seg + 2 * CONV_PAD), d), F32),
                        pltpu.VMEM((SUBLANES, nseg * (seg + 2 * CONV_PAD), LANES), F32),
                        pltpu.VMEM((tl, d), F32)],
        compiler_params=_params("parallel", "parallel"),
        name="conv_dw_seq",
    )(u, y, mod, *weights)


def _dwconv_col(u, y, mod, weights):
    bsz, L, d = u.shape
    hh = L // GRID_W
    u4 = u.reshape(bsz, hh, GRID_W, d)
    y4 = y.reshape(bsz, hh, GRID_W, d)
    mi = _mod_index(mod)
    tok = pl.BlockSpec((1, hh, SUBLANES, d), lambda i, j: (i, 0, j, 0))
    out = pl.pallas_call(
        _dwconv_col_kernel,
        grid=(bsz, GRID_W // SUBLANES),
        in_specs=[tok, tok, pl.BlockSpec((1, MOD_ROWS, d), lambda i, j: (mi(i), 0, 0))]
        + _conv_weights_specs(d),
        out_specs=tok,
        out_shape=jax.ShapeDtypeStruct((bsz, hh, GRID_W, d), F32),
        scratch_shapes=[pltpu.VMEM((hh + 2 * CONV_PAD, SUBLANES, d), F32),
                        pltpu.VMEM((hh, SUBLANES, d), F32)],
        compiler_params=_params("parallel", "parallel"),
        name="conv_dw_col",
    )(u4, y4, mod, *weights)
    return out.reshape(bsz, L, d)


def _conformer_layer(y, mod, cw, axis, mixg, mixb):
    pw1_w, pw1_b, dw_w, dw_b, ln_g, ln_b, pw2_w, pw2_b = cw
    bsz, L, d = y.shape
    row = lambda a: a.reshape(1, -1)
    u = _pw1_glu(y, mod, pw1_w.astype(BF16), row(pw1_b), tl=min(L, 512))
    weights = (dw_w, row(dw_b), row(ln_g), row(ln_b), pw2_w.astype(BF16), row(pw2_b),
               row(mixg), row(mixb))
    if axis == "seq":
        return _dwconv_seq(u, y, mod, weights, seg=L, nseg=1)
    if axis == "row":
        return _dwconv_seq(u, y, mod, weights, seg=GRID_W, nseg=min(8, L // GRID_W))
    return _dwconv_col(u, y, mod, weights)


def _ssd_in_kernel(x_ref, mod_ref, wz_ref, wx_ref, wdt_ref, z_ref, xbc_ref, dt_ref):
    x = x_ref[0]
    h32 = x * (1.0 + mod_ref[0, 1:2, :]) + mod_ref[0, 0:1, :]
    h = h32.astype(BF16)
    z_ref[0] = jnp.dot(h, wz_ref[...], preferred_element_type=F32).astype(z_ref.dtype)
    xbc_ref[0] = jnp.dot(h, wx_ref[...], preferred_element_type=F32).astype(xbc_ref.dtype)
    h_lo = (h32 - h.astype(F32)).astype(BF16)
    ndt = dt_ref.shape[-1]
    hi = jnp.dot(h, wdt_ref[...], preferred_element_type=F32)
    dt_ref[0] = hi[:, :ndt] + hi[:, ndt:] + jnp.dot(h_lo, wdt_ref[:, :ndt], preferred_element_type=F32)


def _ssd_in_proj(y, mod, wz, wx, wdt2, tl):
    bsz, L, d = y.shape
    mi = _mod_index(mod)
    ndt = wdt2.shape[1] // 2
    full = lambda a: pl.BlockSpec(a.shape, lambda i, j: (0, 0))
    return pl.pallas_call(
        _ssd_in_kernel,
        grid=(bsz, L // tl),
        in_specs=[
            pl.BlockSpec((1, tl, d), lambda i, j: (i, j, 0)),
            pl.BlockSpec((1, MOD_ROWS, d), lambda i, j: (mi(i), 0, 0)),
            full(wz), full(wx), full(wdt2),
        ],
        out_specs=[
            pl.BlockSpec((1, tl, SSM_INNER), lambda i, j: (i, j, 0)),
            pl.BlockSpec((1, tl, SSM_CONV_DIM), lambda i, j: (i, j, 0)),
            pl.BlockSpec((1, tl, ndt), lambda i, j: (i, j, 0)),
        ],
        out_shape=[
            jax.ShapeDtypeStruct((bsz, L, SSM_INNER), BF16),
            jax.ShapeDtypeStruct((bsz, L, SSM_CONV_DIM), BF16),
            jax.ShapeDtypeStruct((bsz, L, ndt), F32),
        ],
        compiler_params=_params("parallel", "parallel"),
        name="ssd_in_proj",
    )(y, mod, wz, wx, wdt2)


def _ssd_conv_kernel(x_ref, w_ref, b_ref, o_ref, pad_ref):
    L, cb = x_ref.shape[1:]
    zeros = jnp.zeros((SUBLANES, cb), F32)
    pad_ref[0:SUBLANES, :] = zeros
    pad_ref[SUBLANES:SUBLANES + L, :] = x_ref[0].astype(F32)
    pad_ref[SUBLANES + L:2 * SUBLANES + L, :] = zeros
    first = SUBLANES - SSM_CONV_K // 2
    rows = min(L, 256)
    for r in range(L // rows):
        for c in range(cb // LANES):
            ln = slice(c * LANES, (c + 1) * LANES)
            base = first + r * rows
            acc = jnp.zeros((rows, LANES), F32) + b_ref[:, ln]
            for k in range(SSM_CONV_K):
                acc = acc + w_ref[k:k + 1, ln] * pad_ref[base + k:base + k + rows, ln]
            o_ref[0, r * rows:(r + 1) * rows, ln] = _silu(acc).astype(o_ref.dtype)


def _ssd_conv(xbc, w, b):
    bsz, L, cd = xbc.shape
    cb = max(w for w in range(LANES, cd + 1, LANES) if cd % w == 0 and (w * L <= (1 << 20) or w == LANES))
    return pl.pallas_call(
        _ssd_conv_kernel,
        grid=(bsz, cd // cb),
        in_specs=[
            pl.BlockSpec((1, L, cb), lambda i, j: (i, 0, j)),
            pl.BlockSpec((SSM_CONV_K, cb), lambda i, j: (0, j)),
            pl.BlockSpec((1, cb), lambda i, j: (0, j)),
        ],
        out_specs=pl.BlockSpec((1, L, cb), lambda i, j: (i, 0, j)),
        out_shape=jax.ShapeDtypeStruct((bsz, L, cd), BF16),
        scratch_shapes=[pltpu.VMEM((L + 2 * SUBLANES, cb), F32)],
        compiler_params=_params("parallel", "parallel"),
        name="ssd_conv",
    )(xbc, w, b.reshape(1, cd))


def _split2(x):
    hi = x.astype(BF16)
    return jnp.concatenate([hi, (x - hi.astype(F32)).astype(BF16)], axis=1)


def _split3(x):
    hi = x.astype(BF16)
    r1 = x - hi.astype(F32)
    mid = r1.astype(BF16)
    lo = (r1 - mid.astype(F32)).astype(BF16)
    return jnp.concatenate([hi, mid, lo], axis=1)


def _ssd_scan_kernel(xs_ref, b_ref, c_ref, dt_ref, bias_row_ref, alog_row_ref, tri_ref, e3_ref, init_ref,
                     y_ref, fin_ref, s_ref):
    d = pl.program_id(0)
    c = pl.program_id(2)
    q = SSD_CHUNK
    nh = SSM_HEADS
    gw = SSM_INNER // SSM_GROUPS

    @pl.when(c == 0)
    def _():
        s_ref[...] = init_ref[0, 0]

    is_f = d == 0
    tri = tri_ref[0]
    mask = tri > 0.5
    dt2 = _softplus(dt_ref[0] + bias_row_ref[...])
    a2 = dt2 * -jnp.exp(alog_row_ref[...])
    dt2T = dt2.T
    c3 = jnp.dot(tri.astype(BF16), _split3(a2), preferred_element_type=F32)
    cum2 = c3[:, :LANES] + c3[:, LANES:2 * LANES] + c3[:, 2 * LANES:]
    cumT2 = cum2.T
    cum = jnp.where(is_f, cum2[:, :nh], cum2[:, nh:2 * nh])
    cumT = jnp.where(is_f, cumT2[:nh], cumT2[nh:2 * nh])
    dtT = jnp.where(is_f, dt2T[:nh], dt2T[nh:2 * nh])
    tot2 = jnp.where(is_f, cum2[q - 1:q], cum2[0:1])
    small = jnp.concatenate([dt2 * jnp.exp(jnp.minimum(tot2 - cum2, 0.0)), jnp.exp(cum2),
                             jnp.broadcast_to(jnp.exp(tot2), (SUBLANES, LANES))], axis=0)
    wide = jnp.dot(_split2(small), e3_ref[0], preferred_element_type=F32)
    w_state = wide[0:q]
    w_off = wide[q:2 * q]
    w_tot = wide[2 * q:2 * q + 1]

    xs_b = xs_ref[0]
    xs = xs_b.astype(F32)
    xdec = (xs * w_state).astype(BF16)
    lane = lax.broadcasted_iota(jnp.int32, (q, SSM_INNER), 1) % LANES
    zero = jnp.zeros((), BF16)
    x_stack = jnp.concatenate([jnp.where(lane < SSM_HEAD_DIM, xs_b, zero),
                               jnp.where(lane >= SSM_HEAD_DIM, xs_b, zero)], axis=0)
    for g in range(SSM_GROUPS):
        gs = slice(g * gw, (g + 1) * gw)
        bg = b_ref[0, :, g * D_STATE:(g + 1) * D_STATE]
        cg = c_ref[0, :, g * D_STATE:(g + 1) * D_STATE]
        cb = lax.dot_general(cg, bg, (((1,), (1,)), ((), ())), preferred_element_type=F32)
        s_g = s_ref[:, gs]
        y_off = jnp.dot(cg, s_g.astype(BF16), preferred_element_type=F32) * w_off[:, gs]
        s_ref[:, gs] = w_tot[:, gs] * s_g + jnp.dot(bg.astype(F32).T.astype(BF16), xdec[:, gs],
                                                    preferred_element_type=F32)
        pairs = gw // LANES
        for p in range(pairs):
            ls = slice(g * gw + p * LANES, g * gw + (p + 1) * LANES)
            ms = []
            for j in range(LANES // SSM_HEAD_DIM):
                h = (g * gw + p * LANES) // SSM_HEAD_DIM + j
                diff = cum[:, h:h + 1] - cumT[h:h + 1, :]
                lm = jnp.exp(jnp.where(mask, diff, -1e30))
                ms.append((cb * lm * dtT[h:h + 1, :]).astype(BF16))
            y_diag = jnp.dot(jnp.concatenate(ms, axis=1), x_stack[:, ls], preferred_element_type=F32)
            y_ref[0, 0, :, ls] = (y_diag + y_off[:, p * LANES:(p + 1) * LANES]).astype(y_ref.dtype)

    @pl.when(c == pl.num_programs(2) - 1)
    def _():
        fin_ref[0, 0] = s_ref[...]


def _ssd_scan(xbc, dt_raw, dt_bias, a_log, init):
    bsz, L, _ = xbc.shape
    nc = L // SSD_CHUNK
    q = SSD_CHUNK
    ndt = LANES
    chunk = lambda d, c: c + d * (nc - 1 - 2 * c)
    r = jnp.arange(q)
    tri = jnp.stack([r[:, None] >= r[None, :], r[:, None] <= r[None, :]]).astype(F32)
    src = jnp.arange(ndt)[None, :, None] - SSM_HEADS * jnp.arange(2)[:, None, None]
    e1 = (src == (jnp.arange(SSM_INNER) // SSM_HEAD_DIM)[None, None, :]).astype(BF16)
    e3 = jnp.concatenate([e1, e1], axis=1)
    pad = lambda a: jnp.pad(a.reshape(-1), (0, ndt - a.size))
    dt_bias, a_log = pad(dt_bias), pad(a_log)
    nblk = SSM_INNER // SSM_BC
    const = lambda shape: pl.BlockSpec(shape, lambda d, b, c: tuple(0 for _ in shape))
    return pl.pallas_call(
        _ssd_scan_kernel,
        grid=(2, bsz, nc),
        in_specs=[
            pl.BlockSpec((1, q, SSM_INNER), lambda d, b, c: (b, chunk(d, c), 0)),
            pl.BlockSpec((1, q, SSM_BC), lambda d, b, c: (b, chunk(d, c), nblk)),
            pl.BlockSpec((1, q, SSM_BC), lambda d, b, c: (b, chunk(d, c), nblk + 1)),
            pl.BlockSpec((1, q, ndt), lambda d, b, c: (b, chunk(d, c), 0)),
            const((1, ndt)), const((1, ndt)),
            pl.BlockSpec((1, q, q), lambda d, b, c: (d, 0, 0)),
            pl.BlockSpec((1, 2 * ndt, SSM_INNER), lambda d, b, c: (d, 0, 0)),
            pl.BlockSpec((1, 1, D_STATE, SSM_INNER), lambda d, b, c: (d, b, 0, 0)),
        ],
        out_specs=[
            pl.BlockSpec((1, 1, q, SSM_INNER), lambda d, b, c: (d, b, chunk(d, c), 0)),
            pl.BlockSpec((1, 1, D_STATE, SSM_INNER), lambda d, b, c: (d, b, 0, 0)),
        ],
        out_shape=[
            jax.ShapeDtypeStruct((2, bsz, L, SSM_INNER), BF16),
            jax.ShapeDtypeStruct((2, bsz, D_STATE, SSM_INNER), F32),
        ],
        scratch_shapes=[pltpu.VMEM((D_STATE, SSM_INNER), F32)],
        compiler_params=_params("parallel", "parallel", "arbitrary"),
        name="ssd_scan",
    )(xbc, xbc, xbc, dt_raw, dt_bias.reshape(1, ndt), a_log.reshape(1, ndt), tri, e3, init)


def _ssd_out_kernel(yf_ref, yb_ref, xs_ref, z_ref, res_ref, mod_ref, dskip_ref, nw_ref, wo_ref,
                    mixg_ref, mixb_ref, o_ref):
    y = yf_ref[0, 0].astype(F32) + yb_ref[0, 0].astype(F32) + dskip_ref[...] * xs_ref[0].astype(F32)
    y = y * _silu(z_ref[0].astype(F32))
    gw = SSM_INNER // SSM_GROUPS
    parts = []
    for g in range(SSM_GROUPS):
        yg = y[:, g * gw:(g + 1) * gw]
        ms = jnp.mean(yg * yg, axis=-1, keepdims=True)
        parts.append(yg * lax.rsqrt(ms + LN_EPS))
    yn = jnp.concatenate(parts, axis=1) * nw_ref[...]
    m = jnp.dot(yn.astype(BF16), wo_ref[...], preferred_element_type=F32)
    o_ref[0] = _layer_norm(ALPHA * res_ref[0] + mod_ref[0, 2:3, :] * m, mixg_ref[...], mixb_ref[...])


def _ssd_out(yscan, xbc, z, y, mod, d_skip, norm_w, out_w, mixg, mixb, tl):
    bsz, L, d = y.shape
    mi = _mod_index(mod)
    inner = pl.BlockSpec((1, tl, SSM_INNER), lambda i, j: (i, j, 0))
    vec = lambda n: pl.BlockSpec((1, n), lambda i, j: (0, 0))
    return pl.pallas_call(
        _ssd_out_kernel,
        grid=(bsz, L // tl),
        in_specs=[
            pl.BlockSpec((1, 1, tl, SSM_INNER), lambda i, j: (0, i, j, 0)),
            pl.BlockSpec((1, 1, tl, SSM_INNER), lambda i, j: (1, i, j, 0)),
            inner, inner,
            pl.BlockSpec((1, tl, d), lambda i, j: (i, j, 0)),
            pl.BlockSpec((1, MOD_ROWS, d), lambda i, j: (mi(i), 0, 0)),
            vec(SSM_INNER), vec(SSM_INNER),
            pl.BlockSpec((SSM_INNER, d), lambda i, j: (0, 0)),
            vec(d), vec(d),
        ],
        out_specs=pl.BlockSpec((1, tl, d), lambda i, j: (i, j, 0)),
        out_shape=jax.ShapeDtypeStruct((bsz, L, d), F32),
        compiler_params=_params("parallel", "parallel"),
        name="ssd_out",
    )(yscan, yscan, xbc, z, y, mod, jnp.repeat(d_skip, SSM_HEAD_DIM).reshape(1, SSM_INNER),
      norm_w.reshape(1, SSM_INNER), out_w, mixg.reshape(1, d), mixb.reshape(1, d))


def _cast_cols(w, first, count, bw):
    rows = w.shape[0]
    return pl.pallas_call(
        _cast_kernel,
        grid=(count,),
        in_specs=[pl.BlockSpec((rows, bw), lambda j: (0, first + j))],
        out_specs=pl.BlockSpec((rows, bw), lambda j: (0, j)),
        out_shape=jax.ShapeDtypeStruct((rows, count * bw), BF16),
        compiler_params=_params("parallel"),
        name="cast_cols",
    )(w)


def _dt_weight_kernel(w_ref, o_ref, *, valid):
    w = w_ref[...]
    lane = lax.broadcasted_iota(jnp.int32, w.shape, 1)
    w = jnp.where(lane < valid, w, 0.0)
    o_ref[...] = _split2(w)


def _dt_weight(w, first_col, valid):
    rows = w.shape[0]
    assert first_col % LANES == 0 and valid <= LANES
    return pl.pallas_call(
        functools.partial(_dt_weight_kernel, valid=valid),
        grid=(1,),
        in_specs=[pl.BlockSpec((rows, LANES), lambda j: (0, first_col // LANES))],
        out_specs=pl.BlockSpec((rows, 2 * LANES), lambda j: (0, 0)),
        out_shape=jax.ShapeDtypeStruct((rows, 2 * LANES), BF16),
        compiler_params=_params("arbitrary"),
        name="dt_weight",
    )(w)


def _ssd_layer(y, mod, sw, init, mixg, mixb):
    (wz, wx, wdt2), conv_w, conv_b, dt_bias, a_log, d_skip, norm_w, out_w = sw
    bsz, L, d = y.shape
    z, xbc, dt_raw = _ssd_in_proj(y, mod, wz, wx, wdt2, tl=min(L, 256))
    xbc = _ssd_conv(xbc, conv_w, conv_b)
    if init is None:
        init_t = jnp.zeros((2, bsz, D_STATE, SSM_INNER), F32)
    else:
        init_t = init.astype(F32).transpose(1, 0, 4, 2, 3).reshape(2, bsz, D_STATE, SSM_INNER)
    yscan, fin = _ssd_scan(xbc, dt_raw, dt_bias, a_log, init_t)
    out = _ssd_out(yscan, xbc, z, y, mod, d_skip, norm_w, out_w, mixg, mixb, tl=min(L, 256))
    fin = fin.reshape(2, bsz, D_STATE, SSM_HEADS, SSM_HEAD_DIM).transpose(1, 0, 3, 4, 2)
    return out, fin


def _moe_prep_kernel(y_ref, mod_ref, rwT_ref, h_ref, affT_ref):
    h32 = y_ref[0] * (1.0 + mod_ref[0, 4:5, :]) + mod_ref[0, 3:4, :]
    h_ref[...] = h32.astype(BF16)
    logT = lax.dot_general(rwT_ref[...], h32, (((1,), (1,)), ((), ())), precision=HIGHEST,
                           preferred_element_type=F32)
    ex = jnp.exp(logT - jnp.max(logT, axis=0, keepdims=True))
    affT_ref[...] = ex / jnp.sum(ex, axis=0, keepdims=True)


def _moe_prep(y, mod, router_w, tl):
    bsz, L, d = y.shape
    mi = _mod_index(mod)
    nt = L // tl
    ne = router_w.shape[1]
    return pl.pallas_call(
        _moe_prep_kernel,
        grid=(bsz, nt),
        in_specs=[
            pl.BlockSpec((1, tl, d), lambda i, j: (i, j, 0)),
            pl.BlockSpec((1, MOD_ROWS, d), lambda i, j: (mi(i), 0, 0)),
            pl.BlockSpec((ne, d), lambda i, j: (0, 0)),
        ],
        out_specs=[
            pl.BlockSpec((tl, d), lambda i, j: (i * nt + j, 0)),
            pl.BlockSpec((ne, tl), lambda i, j: (0, i * nt + j)),
        ],
        out_shape=[
            jax.ShapeDtypeStruct((bsz * L, d), BF16),
            jax.ShapeDtypeStruct((ne, bsz * L), F32),
        ],
        compiler_params=_params("parallel", "parallel"),
        name="moe_prep",
    )(y, mod, router_w.T)


def _moe_select_kernel(aff_ref, upper_ref, gate_ref, *, cap):
    ne, T = aff_ref.shape
    aff = aff_ref[...]
    bits = lax.bitcast_convert_type(aff, jnp.int32)

    def search(i, v):
        cand = v | lax.shift_left(jnp.int32(1), 30 - i)
        cnt = jnp.sum(jnp.where(bits >= cand, 1.0, 0.0), axis=1, keepdims=True)
        return jnp.where(cnt >= cap, cand, v)

    thr = lax.fori_loop(0, 31, search, jnp.zeros((ne, 1), jnp.int32))
    need = cap - jnp.sum(jnp.where(bits > thr, 1.0, 0.0), axis=1, keepdims=True)
    upper = upper_ref[...]
    tie_rank = jnp.zeros((ne, 1), F32)
    for j in range(T // LANES):
        ls = slice(j * LANES, (j + 1) * LANES)
        blk = bits[:, ls]
        eq = jnp.where(blk == thr, 1.0, 0.0)
        eq_incl = jnp.dot(eq.astype(BF16), upper, preferred_element_type=F32)
        sel = (blk > thr) | ((blk == thr) & (eq_incl - eq + tie_rank < need))
        tie_rank = tie_rank + eq_incl[:, LANES - 1:LANES]
        gate_ref[:, ls] = jnp.where(sel, aff[:, ls], -1.0)


def _moe_slot_kernel(gate_ref, upper_ref, pos_ref, total_ref, *, tb):
    ne, T = gate_ref.shape
    upper = upper_ref[...]
    lane = lax.broadcasted_iota(jnp.int32, (ne, LANES), 1)
    slot = jnp.zeros((ne, 1), F32)
    totals = jnp.zeros((ne, LANES), F32)
    for j in range(T // LANES):
        t0 = j * LANES
        sel = gate_ref[:, t0:t0 + LANES] >= 0.0
        m = jnp.where(sel, 1.0, 0.0)
        m_incl = jnp.dot(m.astype(BF16), upper, preferred_element_type=F32)
        pos_ref[:, t0:t0 + LANES] = jnp.where(sel, m_incl - m + slot, -1.0).astype(jnp.int32)
        slot = slot + m_incl[:, LANES - 1:LANES]
        if (t0 + LANES) % tb == 0:
            totals = jnp.where(lane == t0 // tb, slot, totals)
            slot = jnp.zeros((ne, 1), F32)
    total_ref[...] = totals.astype(jnp.int32)


def _moe_select(affT, cap, tb, nb):
    ne, T = affT.shape
    r = jnp.arange(LANES)
    upper = (r[:, None] <= r[None, :]).astype(BF16)
    full = pl.BlockSpec((ne, T), lambda i: (0, 0))
    tri = pl.BlockSpec((LANES, LANES), lambda i: (0, 0))
    small = pl.BlockSpec((ne, LANES), lambda i: (0, 0))
    gate = pl.pallas_call(
        functools.partial(_moe_select_kernel, cap=cap),
        grid=(1,),
        in_specs=[full, tri],
        out_specs=full,
        out_shape=jax.ShapeDtypeStruct((ne, T), F32),
        compiler_params=_params("arbitrary"),
        name="moe_select",
    )(affT, upper)
    gate = gate.reshape(ne, T // (MOE_GROUP * nb), nb, MOE_GROUP).transpose(0, 2, 1, 3).reshape(ne, T)
    pos, totals = pl.pallas_call(
        functools.partial(_moe_slot_kernel, tb=tb),
        grid=(1,),
        in_specs=[full, tri],
        out_specs=[full, small],
        out_shape=[jax.ShapeDtypeStruct((ne, T), jnp.int32), jax.ShapeDtypeStruct((ne, LANES), jnp.int32)],
        compiler_params=_params("arbitrary"),
        name="moe_slot",
    )(gate, upper)
    return gate, pos, totals


def _moe_ffn_kernel(tot_ref, h_ref, pos_ref, gate_ref, wg_ref, wu_ref, wd_ref, y_ref, mod_ref, lng_ref,
                    lnb_ref, o_ref, acc_ref):
    b = pl.program_id(0)
    e = pl.program_id(1)
    ne = pl.num_programs(1)
    rows = MOE_ROWS
    tb, d = acc_ref.shape

    @pl.when(e == 0)
    def _():
        acc_ref[...] = jnp.zeros(acc_ref.shape, F32)

    slot_iota = lax.broadcasted_iota(jnp.int32, (rows, tb), 0)

    def chunk(k, carry):
        hit = pos_ref[0] == slot_iota + k * rows
        onehot = jnp.where(hit, 1.0, 0.0).astype(BF16)
        x = jnp.dot(onehot, h_ref[...].reshape(tb, d), preferred_element_type=F32).astype(BF16)
        gate = jnp.sum(jnp.where(hit, gate_ref[0], 0.0), axis=1, keepdims=True)
        hid = _silu(jnp.dot(x, wg_ref[0], preferred_element_type=F32)) * jnp.dot(
            x, wu_ref[0], preferred_element_type=F32)
        yk = jnp.dot(hid.astype(BF16), wd_ref[0], preferred_element_type=F32)
        yk = (yk * gate).astype(BF16)
        acc_ref[...] += lax.dot_general(onehot, yk, (((0,), (0,)), ((), ())),
                                        preferred_element_type=F32)
        return carry

    lax.fori_loop(0, (tot_ref[b * ne + e] + rows - 1) // rows, chunk, 0)

    @pl.when(e == ne - 1)
    def _():
        nmod = mod_ref.shape[0]
        seg = tb // nmod
        y = y_ref[...].reshape(tb, d)
        for s in range(nmod):
            rs = slice(s * seg, (s + 1) * seg)
            out = _layer_norm(ALPHA * y[rs] + mod_ref[s, 5:6, :] * acc_ref[rs, :], lng_ref[...],
                              lnb_ref[...])
            o_ref[s * seg // MOE_GROUP:(s + 1) * seg // MOE_GROUP] = out.reshape(
                seg // MOE_GROUP, 1, MOE_GROUP, d)


def _moe_ffn(y, h, pos, gate, totals, mod, wg, wu, wd, g, b, tb):
    bsz, L, d = y.shape
    T = bsz * L
    ne, _, ff = wg.shape
    nb = T // tb
    ng = tb // MOE_GROUP
    nmod = mod.shape[0]
    assert nmod == 1 or (nmod == bsz and L % (MOE_GROUP * nb) == 0)
    tot = totals[:, :nb].T.reshape(-1)
    row = pl.BlockSpec((1, 1, tb), lambda i, e, t: (e, 0, i))
    tok = pl.BlockSpec((ng, 1, MOE_GROUP, d), lambda i, e, t: (0, i, 0, 0))
    vec = pl.BlockSpec((1, d), lambda i, e, t: (0, 0))
    grid_spec = pltpu.PrefetchScalarGridSpec(
        num_scalar_prefetch=1,
        grid=(nb, ne),
        in_specs=[
            tok, row, row,
            pl.BlockSpec((1, d, ff), lambda i, e, t: (e, 0, 0)),
            pl.BlockSpec((1, d, ff), lambda i, e, t: (e, 0, 0)),
            pl.BlockSpec((1, ff, d), lambda i, e, t: (e, 0, 0)),
            tok,
            pl.BlockSpec((nmod, MOD_ROWS, d), lambda i, e, t: (0, 0, 0)),
            vec, vec,
        ],
        out_specs=tok,
        scratch_shapes=[pltpu.VMEM((tb, d), F32)],
    )
    out = pl.pallas_call(
        _moe_ffn_kernel,
        grid_spec=grid_spec,
        out_shape=jax.ShapeDtypeStruct((ng, nb, MOE_GROUP, d), F32),
        compiler_params=_params("parallel", "arbitrary"),
        name="moe_ffn",
    )(tot, h.reshape(ng, nb, MOE_GROUP, d), pos.reshape(ne, 1, T), gate.reshape(ne, 1, T), wg, wu, wd,
      y.reshape(ng, nb, MOE_GROUP, d), mod, g.reshape(1, d), b.reshape(1, d))
    return out.reshape(bsz, L, d)


def _moe_layer(y, mod, router_w, wg, wu, wd, g, b):
    bsz, L, d = y.shape
    T = bsz * L
    ne = router_w.shape[1]
    cap = EC_CAPACITY_FACTOR * T // ne
    tb = min(T, MOE_TOKEN_BLOCK)
    h, affT = _moe_prep(y, mod, router_w, tl=min(L, 512))
    gate, pos, totals = _moe_select(affT, cap, tb, T // tb)
    return _moe_ffn(y, h, pos, gate, totals, mod, wg, wu, wd, g, b, tb)


def _cast_kernel(x_ref, o_ref):
    o_ref[...] = x_ref[...].astype(o_ref.dtype)


def _cast_bf16(w):
    depth, ne, r, c = w.shape
    blk = pl.BlockSpec((1, 1, r, c), lambda i, j: (i, j, 0, 0))
    return pl.pallas_call(
        _cast_kernel,
        grid=(depth, ne),
        in_specs=[blk],
        out_specs=blk,
        out_shape=jax.ShapeDtypeStruct(w.shape, BF16),
        compiler_params=_params("parallel", "parallel"),
        name="cast_bf16",
    )(w)


def kernel(x_prompt, x_sample, state_ssd, c, c_ctx, ada_w, ada_b, ln_mix_g, ln_mix_b, ln_ffn_g, ln_ffn_b, conv_pw1_w, conv_pw1_b, conv_dw_w, conv_dw_b, conv_ln_g, conv_ln_b, conv_pw2_w, conv_pw2_b, ssd_in_w, ssd_conv_w, ssd_conv_b, ssd_dt_bias, ssd_a_log, ssd_d_skip, ssd_norm_w, ssd_out_w, router_w, moe_w_gate, moe_w_up, moe_w_down):
    d = x_prompt.shape[-1]
    nb_s = x_sample.shape[0]
    cond = jnp.concatenate([c_ctx[None], c, jnp.zeros((MOD_ROWS - 1 - nb_s, d), F32)], axis=0)
    mods = _ada_mod(cond, ada_w, ada_b).reshape(DEPTH, MOD_ROWS, 6, d)
    mods = jnp.pad(mods, ((0, 0), (0, 0), (0, MOD_ROWS - 6), (0, 0)))
    w_gate, w_up, w_down = _cast_bf16(moe_w_gate), _cast_bf16(moe_w_up), _cast_bf16(moe_w_down)
    yp, ys = x_prompt, x_sample
    new_states = []
    for i in range(DEPTH):
        mod_p = mods[i, 0:1]
        mod_s = mods[i, 1:1 + nb_s]
        k = i // 2
        if i % 2 == 0:
            cw = (conv_pw1_w[k], conv_pw1_b[k], conv_dw_w[k], conv_dw_b[k], conv_ln_g[k],
                  conv_ln_b[k], conv_pw2_w[k], conv_pw2_b[k])
            yp = _conformer_layer(yp, mod_p, cw, "seq", ln_mix_g[i], ln_mix_b[i])
            ys = _conformer_layer(ys, mod_s, cw, "row" if k % 2 == 0 else "col", ln_mix_g[i],
                                  ln_mix_b[i])
        else:
            in_w = ssd_in_w[k]
            bw = SSM_INNER // 2
            in_split = (_cast_cols(in_w, 0, SSM_INNER // bw, bw),
                        _cast_cols(in_w, SSM_INNER // bw, SSM_CONV_DIM // bw, bw),
                        _dt_weight(in_w, SSM_INNER + SSM_CONV_DIM, 2 * SSM_HEADS))
            sw = (in_split, ssd_conv_w[k], ssd_conv_b[k], ssd_dt_bias[k], ssd_a_log[k],
                  ssd_d_skip[k], ssd_norm_w[k], ssd_out_w[k].astype(BF16))
            yp, fin = _ssd_layer(yp, mod_p, sw, None, ln_mix_g[i], ln_mix_b[i])
            ys, _ = _ssd_layer(ys, mod_s, sw, state_ssd[:, k], ln_mix_g[i], ln_mix_b[i])
            new_states.append(fin.astype(x_prompt.dtype))
        wg, wu, wd = w_gate[i], w_up[i], w_down[i]
        yp = _moe_layer(yp, mod_p, router_w[i], wg, wu, wd, ln_ffn_g[i], ln_ffn_b[i])
        ys = _moe_layer(ys, mod_s, router_w[i], wg, wu, wd, ln_ffn_g[i], ln_ffn_b[i])
    return (yp, ys, jnp.stack(new_states, axis=1))
```

```python
import functools

import jax
import jax.numpy as jnp
from jax import lax
from jax.experimental import pallas as pl
from jax.experimental.pallas import tpu as pltpu

F32 = jnp.float32
BF16 = jnp.bfloat16
HIGHEST = lax.Precision.HIGHEST

D_MODEL = 1024
DEPTH = 4
GRID_W = 64
CONV_K = 31
SSM_INNER = 2 * D_MODEL
SSM_HEAD_DIM = 64
SSM_HEADS = SSM_INNER // SSM_HEAD_DIM
SSM_GROUPS = 4
D_STATE = 128
SSM_CONV_K = 5
SSM_BC = SSM_GROUPS * D_STATE
SSM_CONV_DIM = SSM_INNER + 2 * SSM_BC
SSD_CHUNK = 128
SSD_HALO = 16
N_EXPERTS = 16
EC_CAPACITY_FACTOR = 2
ALPHA = (2 * DEPTH) ** 0.25
LN_EPS = 1e-5

LANES = 128
SUBLANES = 8
VMEM_LIMIT = 56 * 1024 * 1024
MOD_ROWS = 8
CONV_PAD = 16
MOE_TOKEN_BLOCK = 1024
MOE_ROWS = 160
MOE_GROUP = 16


def _params(*sem):
    return pltpu.CompilerParams(dimension_semantics=sem, vmem_limit_bytes=VMEM_LIMIT)


def _layer_norm(x, g, b):
    mu = jnp.mean(x, axis=-1, keepdims=True)
    xc = x - mu
    var = jnp.mean(xc * xc, axis=-1, keepdims=True)
    return xc * lax.rsqrt(var + LN_EPS) * g + b


def _silu(x):
    return x * jax.nn.sigmoid(x)


def _softplus(x):
    return jnp.maximum(x, 0.0) + jnp.log(1.0 + jnp.exp(-jnp.abs(x)))


def _ada_kernel(c_ref, w_ref, b_ref, o_ref):
    x = _silu(c_ref[...])
    o_ref[0] = jnp.dot(x, w_ref[0], precision=HIGHEST, preferred_element_type=F32) + b_ref[0]


def _ada_mod(cond, ada_w, ada_b):
    depth, d, n = ada_w.shape
    tn = 1536
    return pl.pallas_call(
        _ada_kernel,
        grid=(depth, n // tn),
        in_specs=[
            pl.BlockSpec((MOD_ROWS, d), lambda i, j: (0, 0)),
            pl.BlockSpec((1, d, tn), lambda i, j: (i, 0, j)),
            pl.BlockSpec((1, 1, tn), lambda i, j: (i, 0, j)),
        ],
        out_specs=pl.BlockSpec((1, MOD_ROWS, tn), lambda i, j: (i, 0, j)),
        out_shape=jax.ShapeDtypeStruct((depth, MOD_ROWS, n), F32),
        compiler_params=_params("parallel", "parallel"),
        name="ada_mod",
    )(cond, ada_w, ada_b.reshape(depth, 1, n))


def _mod_index(mod):
    if mod.shape[0] == 1:
        return lambda b: 0
    return lambda b: b


def _pw1_glu_kernel(x_ref, mod_ref, w_ref, b_ref, o_ref):
    d = x_ref.shape[-1]
    x = x_ref[0]
    h = (x * (1.0 + mod_ref[0, 1:2, :]) + mod_ref[0, 0:1, :]).astype(BF16)
    u = jnp.dot(h, w_ref[...], preferred_element_type=F32) + b_ref[...]
    o_ref[0] = u[:, :d] * jax.nn.sigmoid(u[:, d:])


def _pw1_glu(x, mod, w, b, tl):
    bsz, L, d = x.shape
    mi = _mod_index(mod)
    return pl.pallas_call(
        _pw1_glu_kernel,
        grid=(bsz, L // tl),
        in_specs=[
            pl.BlockSpec((1, tl, d), lambda i, j: (i, j, 0)),
            pl.BlockSpec((1, MOD_ROWS, d), lambda i, j: (mi(i), 0, 0)),
            pl.BlockSpec((d, 2 * d), lambda i, j: (0, 0)),
            pl.BlockSpec((1, 2 * d), lambda i, j: (0, 0)),
        ],
        out_specs=pl.BlockSpec((1, tl, d), lambda i, j: (i, j, 0)),
        out_shape=jax.ShapeDtypeStruct((bsz, L, d), F32),
        compiler_params=_params("parallel", "parallel"),
        name="conv_pw1_glu",
    )(x, mod, w, b)


def _conv_tail(v, y, mod_ref, lng_ref, lnb_ref, w2_ref, b2_ref, mixg_ref, mixb_ref):
    v = _silu(_layer_norm(v, lng_ref[...], lnb_ref[...]))
    m = jnp.dot(v.astype(BF16), w2_ref[...], preferred_element_type=F32) + b2_ref[...]
    return _layer_norm(ALPHA * y + mod_ref[0, 2:3, :] * m, mixg_ref[...], mixb_ref[...])


def _dwconv_seq_kernel(u_ref, y_ref, mod_ref, dww_ref, dwb_ref, lng_ref, lnb_ref, w2_ref, b2_ref,
                       mixg_ref, mixb_ref, o_ref, pad_ref, sh_ref, v_ref, *, seg, nseg):
    d = u_ref.shape[-1]
    sp = seg + 2 * CONV_PAD
    n = nseg * sp
    zeros = jnp.zeros((CONV_PAD, d), F32)
    for s in range(nseg):
        pad_ref[s * sp:s * sp + CONV_PAD, :] = zeros
        pad_ref[s * sp + CONV_PAD:s * sp + CONV_PAD + seg, :] = u_ref[0, s * seg:(s + 1) * seg, :]
        pad_ref[s * sp + CONV_PAD + seg:(s + 1) * sp, :] = zeros
    first = CONV_PAD - CONV_K // 2
    rows = min(seg, 64)
    for c in range(d // LANES):
        ln = slice(c * LANES, (c + 1) * LANES)
        for q in range(1, SUBLANES):
            sh_ref[q, 0:n - SUBLANES, :] = pad_ref[q:q + n - SUBLANES, ln]
        for s in range(nseg):
            for r in range(seg // rows):
                acc = jnp.zeros((rows, LANES), F32) + dwb_ref[:, ln]
                for k in range(CONV_K):
                    q = (first + k) % SUBLANES
                    base = s * sp + r * rows + (first + k) - q
                    tap = pad_ref[base:base + rows, ln] if q == 0 else sh_ref[q, base:base + rows, :]
                    acc = acc + dww_ref[k:k + 1, ln] * tap
                v_ref[s * seg + r * rows:s * seg + (r + 1) * rows, ln] = acc
    o_ref[0] = _conv_tail(v_ref[...], y_ref[0], mod_ref, lng_ref, lnb_ref, w2_ref, b2_ref,
                          mixg_ref, mixb_ref)


def _dwconv_col_kernel(u_ref, y_ref, mod_ref, dww_ref, dwb_ref, lng_ref, lnb_ref, w2_ref, b2_ref,
                       mixg_ref, mixb_ref, o_ref, pad_ref, v_ref):
    hh, wt, d = u_ref.shape[1:]
    zeros = jnp.zeros((CONV_PAD, wt, d), F32)
    pad_ref[0:CONV_PAD] = zeros
    pad_ref[CONV_PAD:CONV_PAD + hh] = u_ref[0]
    pad_ref[CONV_PAD + hh:CONV_PAD + hh + CONV_PAD] = zeros
    first = CONV_PAD - CONV_K // 2
    rows = 32
    for r in range(hh // rows):
        for c in range(d // LANES):
            ln = slice(c * LANES, (c + 1) * LANES)
            base = first + r * rows
            acc = jnp.zeros((rows, wt, LANES), F32) + dwb_ref[:, ln]
            for k in range(CONV_K):
                acc = acc + dww_ref[k:k + 1, ln] * pad_ref[base + k:base + k + rows, :, ln]
            v_ref[r * rows:(r + 1) * rows, :, ln] = acc
    out = _conv_tail(v_ref[...].reshape(hh * wt, d), y_ref[0].reshape(hh * wt, d), mod_ref,
                     lng_ref, lnb_ref, w2_ref, b2_ref, mixg_ref, mixb_ref)
    o_ref[0] = out.reshape(hh, wt, d)


def _conv_weights_specs(d):
    vec = pl.BlockSpec((1, d), lambda i, j: (0, 0))
    return [
        pl.BlockSpec((CONV_K, d), lambda i, j: (0, 0)),
        vec, vec, vec,
        pl.BlockSpec((d, d), lambda i, j: (0, 0)),
        vec, vec, vec,
    ]


def _dwconv_seq(u, y, mod, weights, seg, nseg):
    bsz, L, d = u.shape
    tl = seg * nseg
    mi = _mod_index(mod)
    tok = pl.BlockSpec((1, tl, d), lambda i, j: (i, j, 0))
    return pl.pallas_call(
        functools.partial(_dwconv_seq_kernel, seg=seg, nseg=nseg),
        grid=(bsz, L // tl),
        in_specs=[tok, tok, pl.BlockSpec((1, MOD_ROWS, d), lambda i, j: (mi(i), 0, 0))]
        + _conv_weights_specs(d),
        out_specs=tok,
        out_shape=jax.ShapeDtypeStruct((bsz, L, d), F32),
        scratch_shapes=[pltpu.VMEM((nseg * (seg + 2 * CONV_PAD), d), F32),
                        pltpu.VMEM((SUBLANES, nseg * (seg + 2 * CONV_PAD), LANES), F32),
                        pltpu.VMEM((tl, d), F32)],
        compiler_params=_params("parallel", "parallel"),
        name="conv_dw_seq",
    )(u, y, mod, *weights)


def _dwconv_col(u, y, mod, weights):
    bsz, L, d = u.shape
    hh = L // GRID_W
    u4 = u.reshape(bsz, hh, GRID_W, d)
    y4 = y.reshape(bsz, hh, GRID_W, d)
    mi = _mod_index(mod)
    tok = pl.BlockSpec((1, hh, SUBLANES, d), lambda i, j: (i, 0, j, 0))
    out = pl.pallas_call(
        _dwconv_col_kernel,
        grid=(bsz, GRID_W // SUBLANES),
        in_specs=[tok, tok, pl.BlockSpec((1, MOD_ROWS, d), lambda i, j: (mi(i), 0, 0))]
        + _conv_weights_specs(d),
        out_specs=tok,
        out_shape=jax.ShapeDtypeStruct((bsz, hh, GRID_W, d), F32),
        scratch_shapes=[pltpu.VMEM((hh + 2 * CONV_PAD, SUBLANES, d), F32),
                        pltpu.VMEM((hh, SUBLANES, d), F32)],
        compiler_params=_params("parallel", "parallel"),
        name="conv_dw_col",
    )(u4, y4, mod, *weights)
    return out.reshape(bsz, L, d)


def _conformer_layer(y, mod, cw, axis, mixg, mixb):
    pw1_w, pw1_b, dw_w, dw_b, ln_g, ln_b, pw2_w, pw2_b = cw
    bsz, L, d = y.shape
    row = lambda a: a.reshape(1, -1)
    u = _pw1_glu(y, mod, pw1_w.astype(BF16), row(pw1_b), tl=min(L, 512))
    weights = (dw_w, row(dw_b), row(ln_g), row(ln_b), pw2_w.astype(BF16), row(pw2_b),
               row(mixg), row(mixb))
    if axis == "seq":
        return _dwconv_seq(u, y, mod, weights, seg=L, nseg=1)
    if axis == "row":
        return _dwconv_seq(u, y, mod, weights, seg=GRID_W, nseg=min(8, L // GRID_W))
    return _dwconv_col(u, y, mod, weights)


def _ssd_in_kernel(x_ref, xp_ref, xn_ref, mod_ref, wz_ref, wx_ref, wdt_ref, cw_ref, cb_ref, z_ref, xbc_ref,
                   dt_ref, pad_ref):
    j = pl.program_id(1)
    tl = x_ref.shape[1]
    scale = 1.0 + mod_ref[0, 1:2, :]
    shift = mod_ref[0, 0:1, :]
    h32 = x_ref[0] * scale + shift
    h = h32.astype(BF16)
    h_prev = jnp.where(j > 0, xp_ref[0] * scale + shift, 0.0).astype(BF16)
    h_next = jnp.where(j < pl.num_programs(1) - 1, xn_ref[0] * scale + shift, 0.0).astype(BF16)
    z_ref[0] = jnp.dot(h, wz_ref[...], preferred_element_type=F32).astype(z_ref.dtype)
    h_lo = (h32 - h.astype(F32)).astype(BF16)
    ndt = dt_ref.shape[-1]
    hi = jnp.dot(h, wdt_ref[...], preferred_element_type=F32)
    dt_ref[0] = hi[:, :ndt] + hi[:, ndt:] + jnp.dot(h_lo, wdt_ref[:, :ndt], preferred_element_type=F32)
    pad_ref[...] = jnp.dot(jnp.concatenate([h_prev, h, h_next], axis=0), wx_ref[...],
                           preferred_element_type=F32)
    first = SSD_HALO - SSM_CONV_K // 2
    rows = min(tl, 256)
    for r in range(tl // rows):
        for c in range(xbc_ref.shape[-1] // LANES):
            ln = slice(c * LANES, (c + 1) * LANES)
            base = first + r * rows
            acc = jnp.zeros((rows, LANES), F32) + cb_ref[:, ln]
            for k in range(SSM_CONV_K):
                acc = acc + cw_ref[k:k + 1, ln] * pad_ref[base + k:base + k + rows, ln]
            xbc_ref[0, r * rows:(r + 1) * rows, ln] = _silu(acc).astype(xbc_ref.dtype)


def _ssd_in_proj(y, mod, wz, wx, wdt2, conv_w, conv_b, tl):
    bsz, L, d = y.shape
    mi = _mod_index(mod)
    ndt = wdt2.shape[1] // 2
    cd = wx.shape[1]
    full = lambda a: pl.BlockSpec(a.shape, lambda i, j: (0, 0))
    per = tl // SSD_HALO
    nh = L // SSD_HALO
    return pl.pallas_call(
        _ssd_in_kernel,
        grid=(bsz, L // tl),
        in_specs=[
            pl.BlockSpec((1, tl, d), lambda i, j: (i, j, 0)),
            pl.BlockSpec((1, SSD_HALO, d), lambda i, j: (i, jnp.maximum(j * per - 1, 0), 0)),
            pl.BlockSpec((1, SSD_HALO, d), lambda i, j: (i, jnp.minimum((j + 1) * per, nh - 1), 0)),
            pl.BlockSpec((1, MOD_ROWS, d), lambda i, j: (mi(i), 0, 0)),
            full(wz), full(wx), full(wdt2), full(conv_w),
            pl.BlockSpec((1, cd), lambda i, j: (0, 0)),
        ],
        out_specs=[
            pl.BlockSpec((1, tl, SSM_INNER), lambda i, j: (i, j, 0)),
            pl.BlockSpec((1, tl, cd), lambda i, j: (i, j, 0)),
            pl.BlockSpec((1, tl, ndt), lambda i, j: (i, j, 0)),
        ],
        out_shape=[
            jax.ShapeDtypeStruct((bsz, L, SSM_INNER), BF16),
            jax.ShapeDtypeStruct((bsz, L, cd), BF16),
            jax.ShapeDtypeStruct((bsz, L, ndt), F32),
        ],
        scratch_shapes=[pltpu.VMEM((tl + 2 * SSD_HALO, cd), F32)],
        compiler_params=_params("parallel", "parallel"),
        name="ssd_in_proj",
    )(y, y, y, mod, wz, wx, wdt2, conv_w, conv_b.reshape(1, cd))


def _split2(x):
    hi = x.astype(BF16)
    return jnp.concatenate([hi, (x - hi.astype(F32)).astype(BF16)], axis=1)


def _split3(x):
    hi = x.astype(BF16)
    r1 = x - hi.astype(F32)
    mid = r1.astype(BF16)
    lo = (r1 - mid.astype(F32)).astype(BF16)
    return jnp.concatenate([hi, mid, lo], axis=1)


def _ssd_scan_kernel(xs_ref, b_ref, c_ref, dt_ref, bias_row_ref, alog_row_ref, tri_ref, e3_ref, init_ref,
                     y_ref, fin_ref, s_ref):
    d = pl.program_id(0)
    c = pl.program_id(2)
    q = SSD_CHUNK
    nh = SSM_HEADS
    gw = SSM_INNER // SSM_GROUPS

    @pl.when(c == 0)
    def _():
        s_ref[...] = init_ref[0, 0]

    is_f = d == 0
    tri = tri_ref[0]
    mask = tri > 0.5
    dt2 = _softplus(dt_ref[0] + bias_row_ref[...])
    a2 = dt2 * -jnp.exp(alog_row_ref[...])
    dt2T = dt2.T
    c3 = jnp.dot(tri.astype(BF16), _split3(a2), preferred_element_type=F32)
    cum2 = c3[:, :LANES] + c3[:, LANES:2 * LANES] + c3[:, 2 * LANES:]
    cumT2 = cum2.T
    cum = jnp.where(is_f, cum2[:, :nh], cum2[:, nh:2 * nh])
    cumT = jnp.where(is_f, cumT2[:nh], cumT2[nh:2 * nh])
    dtT = jnp.where(is_f, dt2T[:nh], dt2T[nh:2 * nh])
    tot2 = jnp.where(is_f, cum2[q - 1:q], cum2[0:1])
    small = jnp.concatenate([dt2 * jnp.exp(jnp.minimum(tot2 - cum2, 0.0)), jnp.exp(cum2),
                             jnp.broadcast_to(jnp.exp(tot2), (SUBLANES, LANES))], axis=0)
    wide = jnp.dot(_split2(small), e3_ref[0], preferred_element_type=F32)
    w_state = wide[0:q]
    w_off = wide[q:2 * q]
    w_tot = wide[2 * q:2 * q + 1]

    xs_b = xs_ref[0]
    xs = xs_b.astype(F32)
    xdec = (xs * w_state).astype(BF16)
    lane = lax.broadcasted_iota(jnp.int32, (q, SSM_INNER), 1) % LANES
    zero = jnp.zeros((), BF16)
    x_stack = jnp.concatenate([jnp.where(lane < SSM_HEAD_DIM, xs_b, zero),
                               jnp.where(lane >= SSM_HEAD_DIM, xs_b, zero)], axis=0)
    for g in range(SSM_GROUPS):
        gs = slice(g * gw, (g + 1) * gw)
        bg = b_ref[0, :, g * D_STATE:(g + 1) * D_STATE]
        cg = c_ref[0, :, g * D_STATE:(g + 1) * D_STATE]
        cb = lax.dot_general(cg, bg, (((1,), (1,)), ((), ())), preferred_element_type=F32)
        s_g = s_ref[:, gs]
        y_off = jnp.dot(cg, s_g.astype(BF16), preferred_element_type=F32) * w_off[:, gs]
        s_ref[:, gs] = w_tot[:, gs] * s_g + jnp.dot(bg.astype(F32).T.astype(BF16), xdec[:, gs],
                                                    preferred_element_type=F32)
        pairs = gw // LANES
        for p in range(pairs):
            ls = slice(g * gw + p * LANES, g * gw + (p + 1) * LANES)
            ms = []
            for j in range(LANES // SSM_HEAD_DIM):
                h = (g * gw + p * LANES) // SSM_HEAD_DIM + j
                diff = cum[:, h:h + 1] - cumT[h:h + 1, :]
                lm = jnp.exp(jnp.where(mask, diff, -1e30))
                ms.append((cb * lm * dtT[h:h + 1, :]).astype(BF16))
            y_diag = jnp.dot(jnp.concatenate(ms, axis=1), x_stack[:, ls], preferred_element_type=F32)
            y_ref[0, 0, :, ls] = (y_diag + y_off[:, p * LANES:(p + 1) * LANES]).astype(y_ref.dtype)

    @pl.when(c == pl.num_programs(2) - 1)
    def _():
        fin_ref[0, 0] = s_ref[...]


def _ssd_scan(xbc, dt_raw, dt_bias, a_log, init):
    bsz, L, _ = xbc.shape
    nc = L // SSD_CHUNK
    q = SSD_CHUNK
    ndt = LANES
    chunk = lambda d, c: c + d * (nc - 1 - 2 * c)
    r = jnp.arange(q)
    tri = jnp.stack([r[:, None] >= r[None, :], r[:, None] <= r[None, :]]).astype(F32)
    src = jnp.arange(ndt)[None, :, None] - SSM_HEADS * jnp.arange(2)[:, None, None]
    e1 = (src == (jnp.arange(SSM_INNER) // SSM_HEAD_DIM)[None, None, :]).astype(BF16)
    e3 = jnp.concatenate([e1, e1], axis=1)
    pad = lambda a: jnp.pad(a.reshape(-1), (0, ndt - a.size))
    dt_bias, a_log = pad(dt_bias), pad(a_log)
    nblk = SSM_INNER // SSM_BC
    const = lambda shape: pl.BlockSpec(shape, lambda d, b, c: tuple(0 for _ in shape))
    return pl.pallas_call(
        _ssd_scan_kernel,
        grid=(2, bsz, nc),
        in_specs=[
            pl.BlockSpec((1, q, SSM_INNER), lambda d, b, c: (b, chunk(d, c), 0)),
            pl.BlockSpec((1, q, SSM_BC), lambda d, b, c: (b, chunk(d, c), nblk)),
            pl.BlockSpec((1, q, SSM_BC), lambda d, b, c: (b, chunk(d, c), nblk + 1)),
            pl.BlockSpec((1, q, ndt), lambda d, b, c: (b, chunk(d, c), 0)),
            const((1, ndt)), const((1, ndt)),
            pl.BlockSpec((1, q, q), lambda d, b, c: (d, 0, 0)),
            pl.BlockSpec((1, 2 * ndt, SSM_INNER), lambda d, b, c: (d, 0, 0)),
            pl.BlockSpec((1, 1, D_STATE, SSM_INNER), lambda d, b, c: (d, b, 0, 0)),
        ],
        out_specs=[
            pl.BlockSpec((1, 1, q, SSM_INNER), lambda d, b, c: (d, b, chunk(d, c), 0)),
            pl.BlockSpec((1, 1, D_STATE, SSM_INNER), lambda d, b, c: (d, b, 0, 0)),
        ],
        out_shape=[
            jax.ShapeDtypeStruct((2, bsz, L, SSM_INNER), BF16),
            jax.ShapeDtypeStruct((2, bsz, D_STATE, SSM_INNER), F32),
        ],
        scratch_shapes=[pltpu.VMEM((D_STATE, SSM_INNER), F32)],
        compiler_params=_params("parallel", "parallel", "arbitrary"),
        name="ssd_scan",
    )(xbc, xbc, xbc, dt_raw, dt_bias.reshape(1, ndt), a_log.reshape(1, ndt), tri, e3, init)


def _ssd_out_kernel(yf_ref, yb_ref, xs_ref, z_ref, res_ref, mod_ref, dskip_ref, nw_ref, wo_ref,
                    mixg_ref, mixb_ref, o_ref):
    y = yf_ref[0, 0].astype(F32) + yb_ref[0, 0].astype(F32) + dskip_ref[...] * xs_ref[0].astype(F32)
    y = y * _silu(z_ref[0].astype(F32))
    gw = SSM_INNER // SSM_GROUPS
    parts = []
    for g in range(SSM_GROUPS):
        yg = y[:, g * gw:(g + 1) * gw]
        ms = jnp.mean(yg * yg, axis=-1, keepdims=True)
        parts.append(yg * lax.rsqrt(ms + LN_EPS))
    yn = jnp.concatenate(parts, axis=1) * nw_ref[...]
    m = jnp.dot(yn.astype(BF16), wo_ref[...], preferred_element_type=F32)
    o_ref[0] = _layer_norm(ALPHA * res_ref[0] + mod_ref[0, 2:3, :] * m, mixg_ref[...], mixb_ref[...])


def _ssd_out(yscan, xbc, z, y, mod, d_skip, norm_w, out_w, mixg, mixb, tl):
    bsz, L, d = y.shape
    mi = _mod_index(mod)
    inner = pl.BlockSpec((1, tl, SSM_INNER), lambda i, j: (i, j, 0))
    vec = lambda n: pl.BlockSpec((1, n), lambda i, j: (0, 0))
    return pl.pallas_call(
        _ssd_out_kernel,
        grid=(bsz, L // tl),
        in_specs=[
            pl.BlockSpec((1, 1, tl, SSM_INNER), lambda i, j: (0, i, j, 0)),
            pl.BlockSpec((1, 1, tl, SSM_INNER), lambda i, j: (1, i, j, 0)),
            inner, inner,
            pl.BlockSpec((1, tl, d), lambda i, j: (i, j, 0)),
            pl.BlockSpec((1, MOD_ROWS, d), lambda i, j: (mi(i), 0, 0)),
            vec(SSM_INNER), vec(SSM_INNER),
            pl.BlockSpec((SSM_INNER, d), lambda i, j: (0, 0)),
            vec(d), vec(d),
        ],
        out_specs=pl.BlockSpec((1, tl, d), lambda i, j: (i, j, 0)),
        out_shape=jax.ShapeDtypeStruct((bsz, L, d), F32),
        compiler_params=_params("parallel", "parallel"),
        name="ssd_out",
    )(yscan, yscan, xbc, z, y, mod, jnp.repeat(d_skip, SSM_HEAD_DIM).reshape(1, SSM_INNER),
      norm_w.reshape(1, SSM_INNER), out_w, mixg.reshape(1, d), mixb.reshape(1, d))


def _cast_cols(w, first, count, bw):
    rows = w.shape[0]
    return pl.pallas_call(
        _cast_kernel,
        grid=(count,),
        in_specs=[pl.BlockSpec((rows, bw), lambda j: (0, first + j))],
        out_specs=pl.BlockSpec((rows, bw), lambda j: (0, j)),
        out_shape=jax.ShapeDtypeStruct((rows, count * bw), BF16),
        compiler_params=_params("parallel"),
        name="cast_cols",
    )(w)


def _dt_weight_kernel(w_ref, o_ref, *, valid):
    w = w_ref[...]
    lane = lax.broadcasted_iota(jnp.int32, w.shape, 1)
    w = jnp.where(lane < valid, w, 0.0)
    o_ref[...] = _split2(w)


def _dt_weight(w, first_col, valid):
    rows = w.shape[0]
    assert first_col % LANES == 0 and valid <= LANES
    return pl.pallas_call(
        functools.partial(_dt_weight_kernel, valid=valid),
        grid=(1,),
        in_specs=[pl.BlockSpec((rows, LANES), lambda j: (0, first_col // LANES))],
        out_specs=pl.BlockSpec((rows, 2 * LANES), lambda j: (0, 0)),
        out_shape=jax.ShapeDtypeStruct((rows, 2 * LANES), BF16),
        compiler_params=_params("arbitrary"),
        name="dt_weight",
    )(w)


def _ssd_layer(y, mod, sw, init, mixg, mixb):
    (wz, wx, wdt2), conv_w, conv_b, dt_bias, a_log, d_skip, norm_w, out_w = sw
    bsz, L, d = y.shape
    z, xbc, dt_raw = _ssd_in_proj(y, mod, wz, wx, wdt2, conv_w, conv_b, tl=min(L, 256))
    if init is None:
        init_t = jnp.zeros((2, bsz, D_STATE, SSM_INNER), F32)
    else:
        init_t = init.astype(F32).transpose(1, 0, 4, 2, 3).reshape(2, bsz, D_STATE, SSM_INNER)
    yscan, fin = _ssd_scan(xbc, dt_raw, dt_bias, a_log, init_t)
    out = _ssd_out(yscan, xbc, z, y, mod, d_skip, norm_w, out_w, mixg, mixb, tl=min(L, 256))
    fin = fin.reshape(2, bsz, D_STATE, SSM_HEADS, SSM_HEAD_DIM).transpose(1, 0, 3, 4, 2)
    return out, fin


def _moe_prep_kernel(y_ref, mod_ref, rwT_ref, h_ref, affT_ref):
    h32 = y_ref[0] * (1.0 + mod_ref[0, 4:5, :]) + mod_ref[0, 3:4, :]
    h_ref[...] = h32.astype(BF16)
    logT = lax.dot_general(rwT_ref[...], h32, (((1,), (1,)), ((), ())), precision=HIGHEST,
                           preferred_element_type=F32)
    ex = jnp.exp(logT - jnp.max(logT, axis=0, keepdims=True))
    affT_ref[...] = ex / jnp.sum(ex, axis=0, keepdims=True)


def _moe_prep(y, mod, router_w, tl):
    bsz, L, d = y.shape
    mi = _mod_index(mod)
    nt = L // tl
    ne = router_w.shape[1]
    return pl.pallas_call(
        _moe_prep_kernel,
        grid=(bsz, nt),
        in_specs=[
            pl.BlockSpec((1, tl, d), lambda i, j: (i, j, 0)),
            pl.BlockSpec((1, MOD_ROWS, d), lambda i, j: (mi(i), 0, 0)),
            pl.BlockSpec((ne, d), lambda i, j: (0, 0)),
        ],
        out_specs=[
            pl.BlockSpec((tl, d), lambda i, j: (i * nt + j, 0)),
            pl.BlockSpec((ne, tl), lambda i, j: (0, i * nt + j)),
        ],
        out_shape=[
            jax.ShapeDtypeStruct((bsz * L, d), BF16),
            jax.ShapeDtypeStruct((ne, bsz * L), F32),
        ],
        compiler_params=_params("parallel", "parallel"),
        name="moe_prep",
    )(y, mod, router_w.T)


def _moe_select_kernel(aff_ref, upper_ref, gate_ref, *, cap):
    ne, T = aff_ref.shape
    aff = aff_ref[...]
    bits = lax.bitcast_convert_type(aff, jnp.int32)

    def search(i, v):
        cand = v | lax.shift_left(jnp.int32(1), 30 - i)
        cnt = jnp.sum(jnp.where(bits >= cand, 1.0, 0.0), axis=1, keepdims=True)
        return jnp.where(cnt >= cap, cand, v)

    thr = lax.fori_loop(0, 31, search, jnp.zeros((ne, 1), jnp.int32))
    need = cap - jnp.sum(jnp.where(bits > thr, 1.0, 0.0), axis=1, keepdims=True)
    upper = upper_ref[...]
    tie_rank = jnp.zeros((ne, 1), F32)
    for j in range(T // LANES):
        ls = slice(j * LANES, (j + 1) * LANES)
        blk = bits[:, ls]
        eq = jnp.where(blk == thr, 1.0, 0.0)
        eq_incl = jnp.dot(eq.astype(BF16), upper, preferred_element_type=F32)
        sel = (blk > thr) | ((blk == thr) & (eq_incl - eq + tie_rank < need))
        tie_rank = tie_rank + eq_incl[:, LANES - 1:LANES]
        gate_ref[:, ls] = jnp.where(sel, aff[:, ls], -1.0)


def _moe_slot_kernel(gate_ref, upper_ref, pos_ref, total_ref, *, tb):
    ne, T = gate_ref.shape
    upper = upper_ref[...]
    lane = lax.broadcasted_iota(jnp.int32, (ne, LANES), 1)
    slot = jnp.zeros((ne, 1), F32)
    totals = jnp.zeros((ne, LANES), F32)
    for j in range(T // LANES):
        t0 = j * LANES
        sel = gate_ref[:, t0:t0 + LANES] >= 0.0
        m = jnp.where(sel, 1.0, 0.0)
        m_incl = jnp.dot(m.astype(BF16), upper, preferred_element_type=F32)
        pos_ref[:, t0:t0 + LANES] = jnp.where(sel, m_incl - m + slot, -1.0).astype(jnp.int32)
        slot = slot + m_incl[:, LANES - 1:LANES]
        if (t0 + LANES) % tb == 0:
            totals = jnp.where(lane == t0 // tb, slot, totals)
            slot = jnp.zeros((ne, 1), F32)
    total_ref[...] = totals.astype(jnp.int32)


def _moe_select(affT, cap, tb, nb):
    ne, T = affT.shape
    r = jnp.arange(LANES)
    upper = (r[:, None] <= r[None, :]).astype(BF16)
    full = pl.BlockSpec((ne, T), lambda i: (0, 0))
    tri = pl.BlockSpec((LANES, LANES), lambda i: (0, 0))
    small = pl.BlockSpec((ne, LANES), lambda i: (0, 0))
    gate = pl.pallas_call(
        functools.partial(_moe_select_kernel, cap=cap),
        grid=(1,),
        in_specs=[full, tri],
        out_specs=full,
        out_shape=jax.ShapeDtypeStruct((ne, T), F32),
        compiler_params=_params("arbitrary"),
        name="moe_select",
    )(affT, upper)
    gate = gate.reshape(ne, T // (MOE_GROUP * nb), nb, MOE_GROUP).transpose(0, 2, 1, 3).reshape(ne, T)
    pos, totals = pl.pallas_call(
        functools.partial(_moe_slot_kernel, tb=tb),
        grid=(1,),
        in_specs=[full, tri],
        out_specs=[full, small],
        out_shape=[jax.ShapeDtypeStruct((ne, T), jnp.int32), jax.ShapeDtypeStruct((ne, LANES), jnp.int32)],
        compiler_params=_params("arbitrary"),
        name="moe_slot",
    )(gate, upper)
    return gate, pos, totals


def _moe_ffn_kernel(tot_ref, h_ref, pos_ref, gate_ref, wg_ref, wu_ref, wd_ref, y_ref, mod_ref, lng_ref,
                    lnb_ref, o_ref, acc_ref):
    e = pl.program_id(1)
    s = pl.program_id(2)
    ne = pl.num_programs(1)
    nsub = pl.num_programs(2)
    b = pl.program_id(0) * nsub + s
    rows = MOE_ROWS
    _, tb, d = acc_ref.shape
    acc = acc_ref.at[s]

    @pl.when(e == 0)
    def _():
        acc[...] = jnp.zeros(acc.shape, F32)

    def pick(ref):
        out = ref[0, 0, 0:1, :]
        for i in range(1, ref.shape[2]):
            out = jnp.where(s == i, ref[0, 0, i:i + 1, :], out)
        return out

    slot_iota = lax.broadcasted_iota(jnp.int32, (rows, tb), 0)

    def chunk(k, carry):
        hit = pick(pos_ref) == slot_iota + k * rows
        onehot = jnp.where(hit, 1.0, 0.0).astype(BF16)
        h = h_ref[:, pl.ds(s, 1)].reshape(tb, d)
        x = jnp.dot(onehot, h, preferred_element_type=F32).astype(BF16)
        gate = jnp.sum(jnp.where(hit, pick(gate_ref), 0.0), axis=1, keepdims=True)
        hid = _silu(jnp.dot(x, wg_ref[0, 0], preferred_element_type=F32)) * jnp.dot(
            x, wu_ref[0, 0], preferred_element_type=F32)
        yk = jnp.dot(hid.astype(BF16), wd_ref[0, 0], preferred_element_type=F32)
        yk = (yk * gate).astype(BF16)
        acc[...] += lax.dot_general(onehot, yk, (((0,), (0,)), ((), ())), preferred_element_type=F32)
        return carry

    lax.fori_loop(0, (tot_ref[b * ne + e] + rows - 1) // rows, chunk, 0)

    @pl.when(e == ne - 1)
    def _():
        nmod = mod_ref.shape[0]
        seg = tb // nmod
        y = y_ref[...].reshape(tb, d)
        for m in range(nmod):
            rs = slice(m * seg, (m + 1) * seg)
            out = _layer_norm(ALPHA * y[rs] + mod_ref[m, 5:6, :] * acc[rs, :], lng_ref[...], lnb_ref[...])
            o_ref[m * seg // MOE_GROUP:(m + 1) * seg // MOE_GROUP] = out.reshape(
                seg // MOE_GROUP, 1, MOE_GROUP, d)


def _moe_ffn(y, h, pos, gate, totals, mod, layer, w_gate, w_up, w_down, g, b, tb):
    bsz, L, d = y.shape
    T = bsz * L
    _, ne, _, ff = w_gate.shape
    nb = T // tb
    nsub = 2 if nb % 2 == 0 else 1
    ng = tb // MOE_GROUP
    nmod = mod.shape[0]
    assert nmod == 1 or (nmod == bsz and L % (MOE_GROUP * nb) == 0)
    tot = totals[:, :nb].T.reshape(-1)
    last = lambda p, e, s: jnp.where(e == ne - 1, p * nsub + s, p * nsub)
    row = pl.BlockSpec((1, 1, nsub, tb), lambda p, e, s, t: (e, p, 0, 0))
    tok = pl.BlockSpec((ng, 1, MOE_GROUP, d), lambda p, e, s, t: (0, last(p, e, s), 0, 0))
    vec = pl.BlockSpec((1, d), lambda p, e, s, t: (0, 0))
    grid_spec = pltpu.PrefetchScalarGridSpec(
        num_scalar_prefetch=1,
        grid=(nb // nsub, ne, nsub),
        in_specs=[
            pl.BlockSpec((ng, nsub, MOE_GROUP, d), lambda p, e, s, t: (0, p, 0, 0)),
            row, row,
            pl.BlockSpec((1, 1, d, ff), lambda p, e, s, t: (layer, e, 0, 0)),
            pl.BlockSpec((1, 1, d, ff), lambda p, e, s, t: (layer, e, 0, 0)),
            pl.BlockSpec((1, 1, ff, d), lambda p, e, s, t: (layer, e, 0, 0)),
            tok,
            pl.BlockSpec((nmod, MOD_ROWS, d), lambda p, e, s, t: (0, 0, 0)),
            vec, vec,
        ],
        out_specs=tok,
        scratch_shapes=[pltpu.VMEM((nsub, tb, d), F32)],
    )
    out = pl.pallas_call(
        _moe_ffn_kernel,
        grid_spec=grid_spec,
        out_shape=jax.ShapeDtypeStruct((ng, nb, MOE_GROUP, d), F32),
        compiler_params=_params("parallel", "arbitrary", "arbitrary"),
        name="moe_ffn",
    )(tot, h.reshape(ng, nb, MOE_GROUP, d), pos.reshape(ne, nb // nsub, nsub, tb),
      gate.reshape(ne, nb // nsub, nsub, tb), w_gate, w_up, w_down, y.reshape(ng, nb, MOE_GROUP, d), mod,
      g.reshape(1, d), b.reshape(1, d))
    return out.reshape(bsz, L, d)


def _moe_layer(y, mod, router_w, layer, w_gate, w_up, w_down, g, b):
    bsz, L, d = y.shape
    T = bsz * L
    ne = router_w.shape[1]
    cap = EC_CAPACITY_FACTOR * T // ne
    tb = min(T, MOE_TOKEN_BLOCK)
    h, affT = _moe_prep(y, mod, router_w, tl=min(L, 512))
    gate, pos, totals = _moe_select(affT, cap, tb, T // tb)
    return _moe_ffn(y, h, pos, gate, totals, mod, layer, w_gate, w_up, w_down, g, b, tb)


def _cast_kernel(x_ref, o_ref):
    o_ref[...] = x_ref[...].astype(o_ref.dtype)


def _cast_bf16(w):
    depth, ne, r, c = w.shape
    blk = pl.BlockSpec((1, 1, r, c), lambda i, j: (i, j, 0, 0))
    return pl.pallas_call(
        _cast_kernel,
        grid=(depth, ne),
        in_specs=[blk],
        out_specs=blk,
        out_shape=jax.ShapeDtypeStruct(w.shape, BF16),
        compiler_params=_params("parallel", "parallel"),
        name="cast_bf16",
    )(w)


def kernel(x_prompt, x_sample, state_ssd, c, c_ctx, ada_w, ada_b, ln_mix_g, ln_mix_b, ln_ffn_g, ln_ffn_b, conv_pw1_w, conv_pw1_b, conv_dw_w, conv_dw_b, conv_ln_g, conv_ln_b, conv_pw2_w, conv_pw2_b, ssd_in_w, ssd_conv_w, ssd_conv_b, ssd_dt_bias, ssd_a_log, ssd_d_skip, ssd_norm_w, ssd_out_w, router_w, moe_w_gate, moe_w_up, moe_w_down):
    d = x_prompt.shape[-1]
    nb_s = x_sample.shape[0]
    cond = jnp.concatenate([c_ctx[None], c, jnp.zeros((MOD_ROWS - 1 - nb_s, d), F32)], axis=0)
    mods = _ada_mod(cond, ada_w, ada_b).reshape(DEPTH, MOD_ROWS, 6, d)
    mods = jnp.pad(mods, ((0, 0), (0, 0), (0, MOD_ROWS - 6), (0, 0)))
    w_gate, w_up, w_down = _cast_bf16(moe_w_gate), _cast_bf16(moe_w_up), _cast_bf16(moe_w_down)
    yp, ys = x_prompt, x_sample
    new_states = []
    for i in range(DEPTH):
        mod_p = mods[i, 0:1]
        mod_s = mods[i, 1:1 + nb_s]
        k = i // 2
        if i % 2 == 0:
            cw = (conv_pw1_w[k], conv_pw1_b[k], conv_dw_w[k], conv_dw_b[k], conv_ln_g[k],
                  conv_ln_b[k], conv_pw2_w[k], conv_pw2_b[k])
            yp = _conformer_layer(yp, mod_p, cw, "seq", ln_mix_g[i], ln_mix_b[i])
            ys = _conformer_layer(ys, mod_s, cw, "row" if k % 2 == 0 else "col", ln_mix_g[i],
                                  ln_mix_b[i])
        else:
            in_w = ssd_in_w[k]
            bw = SSM_INNER // 2
            in_split = (_cast_cols(in_w, 0, SSM_INNER // bw, bw),
                        _cast_cols(in_w, SSM_INNER // bw, SSM_CONV_DIM // bw, bw),
                        _dt_weight(in_w, SSM_INNER + SSM_CONV_DIM, 2 * SSM_HEADS))
            sw = (in_split, ssd_conv_w[k], ssd_conv_b[k], ssd_dt_bias[k], ssd_a_log[k],
                  ssd_d_skip[k], ssd_norm_w[k], ssd_out_w[k].astype(BF16))
            yp, fin = _ssd_layer(yp, mod_p, sw, None, ln_mix_g[i], ln_mix_b[i])
            ys, _ = _ssd_layer(ys, mod_s, sw, state_ssd[:, k], ln_mix_g[i], ln_mix_b[i])
            new_states.append(fin.astype(x_prompt.dtype))
        yp = _moe_layer(yp, mod_p, router_w[i], i, w_gate, w_up, w_down, ln_ffn_g[i], ln_ffn_b[i])
        ys = _moe_layer(ys, mod_s, router_w[i], i, w_gate, w_up, w_down, ln_ffn_g[i], ln_ffn_b[i])
    return (yp, ys, jnp.stack(new_states, axis=1))
```

```python
import functools

import jax
import jax.numpy as jnp
from jax import lax
from jax.experimental import pallas as pl
from jax.experimental.pallas import tpu as pltpu

F32 = jnp.float32
BF16 = jnp.bfloat16
HIGHEST = lax.Precision.HIGHEST

D_MODEL = 1024
DEPTH = 4
GRID_W = 64
CONV_K = 31
SSM_INNER = 2 * D_MODEL
SSM_HEAD_DIM = 64
SSM_HEADS = SSM_INNER // SSM_HEAD_DIM
SSM_GROUPS = 4
D_STATE = 128
SSM_CONV_K = 5
SSM_BC = SSM_GROUPS * D_STATE
SSM_CONV_DIM = SSM_INNER + 2 * SSM_BC
SSD_CHUNK = 128
N_EXPERTS = 16
EC_CAPACITY_FACTOR = 2
ALPHA = (2 * DEPTH) ** 0.25
LN_EPS = 1e-5

LANES = 128
SUBLANES = 8
VMEM_LIMIT = 56 * 1024 * 1024
MOD_ROWS = 8
CONV_PAD = 16
MOE_TOKEN_BLOCK = 1024
MOE_ROWS = 160
MOE_GROUP = 16
MOE_BATCH = 3


def _params(*sem):
    return pltpu.CompilerParams(dimension_semantics=sem, vmem_limit_bytes=VMEM_LIMIT)


def _layer_norm(x, g, b):
    mu = jnp.mean(x, axis=-1, keepdims=True)
    xc = x - mu
    var = jnp.mean(xc * xc, axis=-1, keepdims=True)
    return xc * lax.rsqrt(var + LN_EPS) * g + b


def _silu(x):
    return x * jax.nn.sigmoid(x)


def _softplus(x):
    return jnp.maximum(x, 0.0) + jnp.log(1.0 + jnp.exp(-jnp.abs(x)))


def _ada_kernel(c_ref, w_ref, b_ref, o_ref):
    x = _silu(c_ref[...])
    o_ref[0] = jnp.dot(x, w_ref[0], precision=HIGHEST, preferred_element_type=F32) + b_ref[0]


def _ada_mod(cond, ada_w, ada_b):
    depth, d, n = ada_w.shape
    tn = 1536
    return pl.pallas_call(
        _ada_kernel,
        grid=(depth, n // tn),
        in_specs=[
            pl.BlockSpec((MOD_ROWS, d), lambda i, j: (0, 0)),
            pl.BlockSpec((1, d, tn), lambda i, j: (i, 0, j)),
            pl.BlockSpec((1, 1, tn), lambda i, j: (i, 0, j)),
        ],
        out_specs=pl.BlockSpec((1, MOD_ROWS, tn), lambda i, j: (i, 0, j)),
        out_shape=jax.ShapeDtypeStruct((depth, MOD_ROWS, n), F32),
        compiler_params=_params("parallel", "parallel"),
        name="ada_mod",
    )(cond, ada_w, ada_b.reshape(depth, 1, n))


def _mod_index(mod):
    if mod.shape[0] == 1:
        return lambda b: 0
    return lambda b: b


def _pw1_glu_kernel(x_ref, mod_ref, w_ref, b_ref, o_ref):
    d = x_ref.shape[-1]
    x = x_ref[0]
    h = (x * (1.0 + mod_ref[0, 1:2, :]) + mod_ref[0, 0:1, :]).astype(BF16)
    u = jnp.dot(h, w_ref[...], preferred_element_type=F32) + b_ref[...]
    o_ref[0] = u[:, :d] * jax.nn.sigmoid(u[:, d:])


def _pw1_glu(x, mod, w, b, tl):
    bsz, L, d = x.shape
    mi = _mod_index(mod)
    return pl.pallas_call(
        _pw1_glu_kernel,
        grid=(bsz, L // tl),
        in_specs=[
            pl.BlockSpec((1, tl, d), lambda i, j: (i, j, 0)),
            pl.BlockSpec((1, MOD_ROWS, d), lambda i, j: (mi(i), 0, 0)),
            pl.BlockSpec((d, 2 * d), lambda i, j: (0, 0)),
            pl.BlockSpec((1, 2 * d), lambda i, j: (0, 0)),
        ],
        out_specs=pl.BlockSpec((1, tl, d), lambda i, j: (i, j, 0)),
        out_shape=jax.ShapeDtypeStruct((bsz, L, d), F32),
        compiler_params=_params("parallel", "parallel"),
        name="conv_pw1_glu",
    )(x, mod, w, b)


def _conv_tail(v, y, mod_ref, lng_ref, lnb_ref, w2_ref, b2_ref, mixg_ref, mixb_ref):
    v = _silu(_layer_norm(v, lng_ref[...], lnb_ref[...]))
    m = jnp.dot(v.astype(BF16), w2_ref[...], preferred_element_type=F32) + b2_ref[...]
    return _layer_norm(ALPHA * y + mod_ref[0, 2:3, :] * m, mixg_ref[...], mixb_ref[...])


def _dwconv_seq_kernel(u_ref, y_ref, mod_ref, dww_ref, dwb_ref, lng_ref, lnb_ref, w2_ref, b2_ref,
                       mixg_ref, mixb_ref, o_ref, pad_ref, sh_ref, v_ref, *, seg, nseg):
    d = u_ref.shape[-1]
    sp = seg + 2 * CONV_PAD
    n = nseg * sp
    zeros = jnp.zeros((CONV_PAD, d), F32)
    for s in range(nseg):
        pad_ref[s * sp:s * sp + CONV_PAD, :] = zeros
        pad_ref[s * sp + CONV_PAD:s * sp + CONV_PAD + seg, :] = u_ref[0, s * seg:(s + 1) * seg, :]
        pad_ref[s * sp + CONV_PAD + seg:(s + 1) * sp, :] = zeros
    first = CONV_PAD - CONV_K // 2
    rows = min(seg, 64)
    for c in range(d // LANES):
        ln = slice(c * LANES, (c + 1) * LANES)
        for q in range(1, SUBLANES):
            sh_ref[q, 0:n - SUBLANES, :] = pad_ref[q:q + n - SUBLANES, ln]
        for s in range(nseg):
            for r in range(seg // rows):
                acc = jnp.zeros((rows, LANES), F32) + dwb_ref[:, ln]
                for k in range(CONV_K):
                    q = (first + k) % SUBLANES
                    base = s * sp + r * rows + (first + k) - q
                    tap = pad_ref[base:base + rows, ln] if q == 0 else sh_ref[q, base:base + rows, :]
                    acc = acc + dww_ref[k:k + 1, ln] * tap
                v_ref[s * seg + r * rows:s * seg + (r + 1) * rows, ln] = acc
    o_ref[0] = _conv_tail(v_ref[...], y_ref[0], mod_ref, lng_ref, lnb_ref, w2_ref, b2_ref,
                          mixg_ref, mixb_ref)


def _dwconv_col_kernel(u_ref, y_ref, mod_ref, dww_ref, dwb_ref, lng_ref, lnb_ref, w2_ref, b2_ref,
                       mixg_ref, mixb_ref, o_ref, pad_ref, v_ref):
    hh, wt, d = u_ref.shape[1:]
    zeros = jnp.zeros((CONV_PAD, wt, d), F32)
    pad_ref[0:CONV_PAD] = zeros
    pad_ref[CONV_PAD:CONV_PAD + hh] = u_ref[0]
    pad_ref[CONV_PAD + hh:CONV_PAD + hh + CONV_PAD] = zeros
    first = CONV_PAD - CONV_K // 2
    rows = 32
    for r in range(hh // rows):
        for c in range(d // LANES):
            ln = slice(c * LANES, (c + 1) * LANES)
            base = first + r * rows
            acc = jnp.zeros((rows, wt, LANES), F32) + dwb_ref[:, ln]
            for k in range(CONV_K):
                acc = acc + dww_ref[k:k + 1, ln] * pad_ref[base + k:base + k + rows, :, ln]
            v_ref[r * rows:(r + 1) * rows, :, ln] = acc
    out = _conv_tail(v_ref[...].reshape(hh * wt, d), y_ref[0].reshape(hh * wt, d), mod_ref,
                     lng_ref, lnb_ref, w2_ref, b2_ref, mixg_ref, mixb_ref)
    o_ref[0] = out.reshape(hh, wt, d)


def _conv_weights_specs(d):
    vec = pl.BlockSpec((1, d), lambda i, j: (0, 0))
    return [
        pl.BlockSpec((CONV_K, d), lambda i, j: (0, 0)),
        vec, vec, vec,
        pl.BlockSpec((d, d), lambda i, j: (0, 0)),
        vec, vec, vec,
    ]


def _dwconv_seq(u, y, mod, weights, seg, nseg):
    bsz, L, d = u.shape
    tl = seg * nseg
    mi = _mod_index(mod)
    tok = pl.BlockSpec((1, tl, d), lambda i, j: (i, j, 0))
    return pl.pallas_call(
        functools.partial(_dwconv_seq_kernel, seg=seg, nseg=nseg),
        grid=(bsz, L // tl),
        in_specs=[tok, tok, pl.BlockSpec((1, MOD_ROWS, d), lambda i, j: (mi(i), 0, 0))]
        + _conv_weights_specs(d),
        out_specs=tok,
        out_shape=jax.ShapeDtypeStruct((bsz, L, d), F32),
        scratch_shapes=[pltpu.VMEM((nseg * (seg + 2 * CONV_PAD), d), F32),
                        pltpu.VMEM((SUBLANES, nseg * (seg + 2 * CONV_PAD), LANES), F32),
                        pltpu.VMEM((tl, d), F32)],
        compiler_params=_params("parallel", "parallel"),
        name="conv_dw_seq",
    )(u, y, mod, *weights)


def _dwconv_col(u, y, mod, weights):
    bsz, L, d = u.shape
    hh = L // GRID_W
    u4 = u.reshape(bsz, hh, GRID_W, d)
    y4 = y.reshape(bsz, hh, GRID_W, d)
    mi = _mod_index(mod)
    tok = pl.BlockSpec((1, hh, SUBLANES, d), lambda i, j: (i, 0, j, 0))
    out = pl.pallas_call(
        _dwconv_col_kernel,
        grid=(bsz, GRID_W // SUBLANES),
        in_specs=[tok, tok, pl.BlockSpec((1, MOD_ROWS, d), lambda i, j: (mi(i), 0, 0))]
        + _conv_weights_specs(d),
        out_specs=tok,
        out_shape=jax.ShapeDtypeStruct((bsz, hh, GRID_W, d), F32),
        scratch_shapes=[pltpu.VMEM((hh + 2 * CONV_PAD, SUBLANES, d), F32),
                        pltpu.VMEM((hh, SUBLANES, d), F32)],
        compiler_params=_params("parallel", "parallel"),
        name="conv_dw_col",
    )(u4, y4, mod, *weights)
    return out.reshape(bsz, L, d)


def _conformer_layer(y, mod, cw, axis, mixg, mixb):
    pw1_w, pw1_b, dw_w, dw_b, ln_g, ln_b, pw2_w, pw2_b = cw
    bsz, L, d = y.shape
    row = lambda a: a.reshape(1, -1)
    u = _pw1_glu(y, mod, pw1_w.astype(BF16), row(pw1_b), tl=min(L, 512))
    weights = (dw_w, row(dw_b), row(ln_g), row(ln_b), pw2_w.astype(BF16), row(pw2_b),
               row(mixg), row(mixb))
    if axis == "seq":
        return _dwconv_seq(u, y, mod, weights, seg=L, nseg=1)
    if axis == "row":
        return _dwconv_seq(u, y, mod, weights, seg=GRID_W, nseg=min(8, L // GRID_W))
    return _dwconv_col(u, y, mod, weights)


def _ssd_in_kernel(x_ref, mod_ref, wz_ref, wx_ref, wdt_ref, z_ref, xbc_ref, dt_ref):
    x = x_ref[0]
    h32 = x * (1.0 + mod_ref[0, 1:2, :]) + mod_ref[0, 0:1, :]
    h = h32.astype(BF16)
    z_ref[0] = jnp.dot(h, wz_ref[...], preferred_element_type=F32).astype(z_ref.dtype)
    xbc_ref[0] = jnp.dot(h, wx_ref[...], preferred_element_type=F32).astype(xbc_ref.dtype)
    h_lo = (h32 - h.astype(F32)).astype(BF16)
    ndt = dt_ref.shape[-1]
    hi = jnp.dot(h, wdt_ref[...], preferred_element_type=F32)
    dt_ref[0] = hi[:, :ndt] + hi[:, ndt:] + jnp.dot(h_lo, wdt_ref[:, :ndt], preferred_element_type=F32)


def _ssd_in_proj(y, mod, wz, wx, wdt2, tl):
    bsz, L, d = y.shape
    mi = _mod_index(mod)
    ndt = wdt2.shape[1] // 2
    full = lambda a: pl.BlockSpec(a.shape, lambda i, j: (0, 0))
    return pl.pallas_call(
        _ssd_in_kernel,
        grid=(bsz, L // tl),
        in_specs=[
            pl.BlockSpec((1, tl, d), lambda i, j: (i, j, 0)),
            pl.BlockSpec((1, MOD_ROWS, d), lambda i, j: (mi(i), 0, 0)),
            full(wz), full(wx), full(wdt2),
        ],
        out_specs=[
            pl.BlockSpec((1, tl, SSM_INNER), lambda i, j: (i, j, 0)),
            pl.BlockSpec((1, tl, SSM_CONV_DIM), lambda i, j: (i, j, 0)),
            pl.BlockSpec((1, tl, ndt), lambda i, j: (i, j, 0)),
        ],
        out_shape=[
            jax.ShapeDtypeStruct((bsz, L, SSM_INNER), BF16),
            jax.ShapeDtypeStruct((bsz, L, SSM_CONV_DIM), BF16),
            jax.ShapeDtypeStruct((bsz, L, ndt), F32),
        ],
        compiler_params=_params("parallel", "parallel"),
        name="ssd_in_proj",
    )(y, mod, wz, wx, wdt2)


def _ssd_conv_kernel(x_ref, w_ref, b_ref, o_ref, pad_ref):
    L, cb = x_ref.shape[1:]
    zeros = jnp.zeros((SUBLANES, cb), F32)
    pad_ref[0:SUBLANES, :] = zeros
    pad_ref[SUBLANES:SUBLANES + L, :] = x_ref[0].astype(F32)
    pad_ref[SUBLANES + L:2 * SUBLANES + L, :] = zeros
    first = SUBLANES - SSM_CONV_K // 2
    rows = min(L, 256)
    for r in range(L // rows):
        for c in range(cb // LANES):
            ln = slice(c * LANES, (c + 1) * LANES)
            base = first + r * rows
            acc = jnp.zeros((rows, LANES), F32) + b_ref[:, ln]
            for k in range(SSM_CONV_K):
                acc = acc + w_ref[k:k + 1, ln] * pad_ref[base + k:base + k + rows, ln]
            o_ref[0, r * rows:(r + 1) * rows, ln] = _silu(acc).astype(o_ref.dtype)


def _ssd_conv(xbc, w, b):
    bsz, L, cd = xbc.shape
    cb = max(w for w in range(LANES, cd + 1, LANES) if cd % w == 0 and (w * L <= (1 << 20) or w == LANES))
    return pl.pallas_call(
        _ssd_conv_kernel,
        grid=(bsz, cd // cb),
        in_specs=[
            pl.BlockSpec((1, L, cb), lambda i, j: (i, 0, j)),
            pl.BlockSpec((SSM_CONV_K, cb), lambda i, j: (0, j)),
            pl.BlockSpec((1, cb), lambda i, j: (0, j)),
        ],
        out_specs=pl.BlockSpec((1, L, cb), lambda i, j: (i, 0, j)),
        out_shape=jax.ShapeDtypeStruct((bsz, L, cd), BF16),
        scratch_shapes=[pltpu.VMEM((L + 2 * SUBLANES, cb), F32)],
        compiler_params=_params("parallel", "parallel"),
        name="ssd_conv",
    )(xbc, w, b.reshape(1, cd))


def _split2(x):
    hi = x.astype(BF16)
    return jnp.concatenate([hi, (x - hi.astype(F32)).astype(BF16)], axis=1)


def _split3(x):
    hi = x.astype(BF16)
    r1 = x - hi.astype(F32)
    mid = r1.astype(BF16)
    lo = (r1 - mid.astype(F32)).astype(BF16)
    return jnp.concatenate([hi, mid, lo], axis=1)


def _ssd_scan_kernel(xs_ref, b_ref, c_ref, dt_ref, bias_row_ref, alog_row_ref, tri_ref, e3_ref, init_ref,
                     y_ref, fin_ref, s_ref):
    d = pl.program_id(0)
    c = pl.program_id(2)
    q = SSD_CHUNK
    nh = SSM_HEADS
    gw = SSM_INNER // SSM_GROUPS

    @pl.when(c == 0)
    def _():
        s_ref[...] = init_ref[0, 0]

    is_f = d == 0
    tri = tri_ref[0]
    mask = tri > 0.5
    dt2 = _softplus(dt_ref[0] + bias_row_ref[...])
    a2 = dt2 * -jnp.exp(alog_row_ref[...])
    dt2T = dt2.T
    c3 = jnp.dot(tri.astype(BF16), _split3(a2), preferred_element_type=F32)
    cum2 = c3[:, :LANES] + c3[:, LANES:2 * LANES] + c3[:, 2 * LANES:]
    cumT2 = cum2.T
    cum = jnp.where(is_f, cum2[:, :nh], cum2[:, nh:2 * nh])
    cumT = jnp.where(is_f, cumT2[:nh], cumT2[nh:2 * nh])
    dtT = jnp.where(is_f, dt2T[:nh], dt2T[nh:2 * nh])
    tot2 = jnp.where(is_f, cum2[q - 1:q], cum2[0:1])
    small = jnp.concatenate([dt2 * jnp.exp(jnp.minimum(tot2 - cum2, 0.0)), jnp.exp(cum2),
                             jnp.broadcast_to(jnp.exp(tot2), (SUBLANES, LANES))], axis=0)
    wide = jnp.dot(_split2(small), e3_ref[0], preferred_element_type=F32)
    w_state = wide[0:q]
    w_off = wide[q:2 * q]
    w_tot = wide[2 * q:2 * q + 1]

    xs_b = xs_ref[0]
    xs = xs_b.astype(F32)
    xdec = (xs * w_state).astype(BF16)
    lane = lax.broadcasted_iota(jnp.int32, (q, SSM_INNER), 1) % LANES
    zero = jnp.zeros((), BF16)
    x_stack = jnp.concatenate([jnp.where(lane < SSM_HEAD_DIM, xs_b, zero),
                               jnp.where(lane >= SSM_HEAD_DIM, xs_b, zero)], axis=0)
    for g in range(SSM_GROUPS):
        gs = slice(g * gw, (g + 1) * gw)
        bg = b_ref[0, :, g * D_STATE:(g + 1) * D_STATE]
        cg = c_ref[0, :, g * D_STATE:(g + 1) * D_STATE]
        cb = lax.dot_general(cg, bg, (((1,), (1,)), ((), ())), preferred_element_type=F32)
        s_g = s_ref[:, gs]
        y_off = jnp.dot(cg, s_g.astype(BF16), preferred_element_type=F32) * w_off[:, gs]
        s_ref[:, gs] = w_tot[:, gs] * s_g + jnp.dot(bg.astype(F32).T.astype(BF16), xdec[:, gs],
                                                    preferred_element_type=F32)
        pairs = gw // LANES
        for p in range(pairs):
            ls = slice(g * gw + p * LANES, g * gw + (p + 1) * LANES)
            ms = []
            for j in range(LANES // SSM_HEAD_DIM):
                h = (g * gw + p * LANES) // SSM_HEAD_DIM + j
                diff = cum[:, h:h + 1] - cumT[h:h + 1, :]
                lm = jnp.exp(jnp.where(mask, diff, -1e30))
                ms.append((cb * lm * dtT[h:h + 1, :]).astype(BF16))
            y_diag = jnp.dot(jnp.concatenate(ms, axis=1), x_stack[:, ls], preferred_element_type=F32)
            y_ref[0, 0, :, ls] = (y_diag + y_off[:, p * LANES:(p + 1) * LANES]).astype(y_ref.dtype)

    @pl.when(c == pl.num_programs(2) - 1)
    def _():
        fin_ref[0, 0] = s_ref[...]


def _ssd_scan(xbc, dt_raw, dt_bias, a_log, init):
    bsz, L, _ = xbc.shape
    nc = L // SSD_CHUNK
    q = SSD_CHUNK
    ndt = LANES
    chunk = lambda d, c: c + d * (nc - 1 - 2 * c)
    r = jnp.arange(q)
    tri = jnp.stack([r[:, None] >= r[None, :], r[:, None] <= r[None, :]]).astype(F32)
    src = jnp.arange(ndt)[None, :, None] - SSM_HEADS * jnp.arange(2)[:, None, None]
    e1 = (src == (jnp.arange(SSM_INNER) // SSM_HEAD_DIM)[None, None, :]).astype(BF16)
    e3 = jnp.concatenate([e1, e1], axis=1)
    pad = lambda a: jnp.pad(a.reshape(-1), (0, ndt - a.size))
    dt_bias, a_log = pad(dt_bias), pad(a_log)
    nblk = SSM_INNER // SSM_BC
    const = lambda shape: pl.BlockSpec(shape, lambda d, b, c: tuple(0 for _ in shape))
    return pl.pallas_call(
        _ssd_scan_kernel,
        grid=(2, bsz, nc),
        in_specs=[
            pl.BlockSpec((1, q, SSM_INNER), lambda d, b, c: (b, chunk(d, c), 0)),
            pl.BlockSpec((1, q, SSM_BC), lambda d, b, c: (b, chunk(d, c), nblk)),
            pl.BlockSpec((1, q, SSM_BC), lambda d, b, c: (b, chunk(d, c), nblk + 1)),
            pl.BlockSpec((1, q, ndt), lambda d, b, c: (b, chunk(d, c), 0)),
            const((1, ndt)), const((1, ndt)),
            pl.BlockSpec((1, q, q), lambda d, b, c: (d, 0, 0)),
            pl.BlockSpec((1, 2 * ndt, SSM_INNER), lambda d, b, c: (d, 0, 0)),
            pl.BlockSpec((1, 1, D_STATE, SSM_INNER), lambda d, b, c: (d, b, 0, 0)),
        ],
        out_specs=[
            pl.BlockSpec((1, 1, q, SSM_INNER), lambda d, b, c: (d, b, chunk(d, c), 0)),
            pl.BlockSpec((1, 1, D_STATE, SSM_INNER), lambda d, b, c: (d, b, 0, 0)),
        ],
        out_shape=[
            jax.ShapeDtypeStruct((2, bsz, L, SSM_INNER), BF16),
            jax.ShapeDtypeStruct((2, bsz, D_STATE, SSM_INNER), F32),
        ],
        scratch_shapes=[pltpu.VMEM((D_STATE, SSM_INNER), F32)],
        compiler_params=_params("parallel", "parallel", "arbitrary"),
        name="ssd_scan",
    )(xbc, xbc, xbc, dt_raw, dt_bias.reshape(1, ndt), a_log.reshape(1, ndt), tri, e3, init)


def _ssd_out_kernel(yf_ref, yb_ref, xs_ref, z_ref, res_ref, mod_ref, dskip_ref, nw_ref, wo_ref,
                    mixg_ref, mixb_ref, o_ref):
    y = yf_ref[0, 0].astype(F32) + yb_ref[0, 0].astype(F32) + dskip_ref[...] * xs_ref[0].astype(F32)
    y = y * _silu(z_ref[0].astype(F32))
    gw = SSM_INNER // SSM_GROUPS
    parts = []
    for g in range(SSM_GROUPS):
        yg = y[:, g * gw:(g + 1) * gw]
        ms = jnp.mean(yg * yg, axis=-1, keepdims=True)
        parts.append(yg * lax.rsqrt(ms + LN_EPS))
    yn = jnp.concatenate(parts, axis=1) * nw_ref[...]
    m = jnp.dot(yn.astype(BF16), wo_ref[...], preferred_element_type=F32)
    o_ref[0] = _layer_norm(ALPHA * res_ref[0] + mod_ref[0, 2:3, :] * m, mixg_ref[...], mixb_ref[...])


def _ssd_out(yscan, xbc, z, y, mod, d_skip, norm_w, out_w, mixg, mixb, tl):
    bsz, L, d = y.shape
    mi = _mod_index(mod)
    inner = pl.BlockSpec((1, tl, SSM_INNER), lambda i, j: (i, j, 0))
    vec = lambda n: pl.BlockSpec((1, n), lambda i, j: (0, 0))
    return pl.pallas_call(
        _ssd_out_kernel,
        grid=(bsz, L // tl),
        in_specs=[
            pl.BlockSpec((1, 1, tl, SSM_INNER), lambda i, j: (0, i, j, 0)),
            pl.BlockSpec((1, 1, tl, SSM_INNER), lambda i, j: (1, i, j, 0)),
            inner, inner,
            pl.BlockSpec((1, tl, d), lambda i, j: (i, j, 0)),
            pl.BlockSpec((1, MOD_ROWS, d), lambda i, j: (mi(i), 0, 0)),
            vec(SSM_INNER), vec(SSM_INNER),
            pl.BlockSpec((SSM_INNER, d), lambda i, j: (0, 0)),
            vec(d), vec(d),
        ],
        out_specs=pl.BlockSpec((1, tl, d), lambda i, j: (i, j, 0)),
        out_shape=jax.ShapeDtypeStruct((bsz, L, d), F32),
        compiler_params=_params("parallel", "parallel"),
        name="ssd_out",
    )(yscan, yscan, xbc, z, y, mod, jnp.repeat(d_skip, SSM_HEAD_DIM).reshape(1, SSM_INNER),
      norm_w.reshape(1, SSM_INNER), out_w, mixg.reshape(1, d), mixb.reshape(1, d))


def _cast_cols(w, first, count, bw):
    rows = w.shape[0]
    return pl.pallas_call(
        _cast_kernel,
        grid=(count,),
        in_specs=[pl.BlockSpec((rows, bw), lambda j: (0, first + j))],
        out_specs=pl.BlockSpec((rows, bw), lambda j: (0, j)),
        out_shape=jax.ShapeDtypeStruct((rows, count * bw), BF16),
        compiler_params=_params("parallel"),
        name="cast_cols",
    )(w)


def _dt_weight_kernel(w_ref, o_ref, *, valid):
    w = w_ref[...]
    lane = lax.broadcasted_iota(jnp.int32, w.shape, 1)
    w = jnp.where(lane < valid, w, 0.0)
    o_ref[...] = _split2(w)


def _dt_weight(w, first_col, valid):
    rows = w.shape[0]
    assert first_col % LANES == 0 and valid <= LANES
    return pl.pallas_call(
        functools.partial(_dt_weight_kernel, valid=valid),
        grid=(1,),
        in_specs=[pl.BlockSpec((rows, LANES), lambda j: (0, first_col // LANES))],
        out_specs=pl.BlockSpec((rows, 2 * LANES), lambda j: (0, 0)),
        out_shape=jax.ShapeDtypeStruct((rows, 2 * LANES), BF16),
        compiler_params=_params("arbitrary"),
        name="dt_weight",
    )(w)


def _ssd_layer(y, mod, sw, init, mixg, mixb):
    (wz, wx, wdt2), conv_w, conv_b, dt_bias, a_log, d_skip, norm_w, out_w = sw
    bsz, L, d = y.shape
    z, xbc, dt_raw = _ssd_in_proj(y, mod, wz, wx, wdt2, tl=min(L, 256))
    xbc = _ssd_conv(xbc, conv_w, conv_b)
    if init is None:
        init_t = jnp.zeros((2, bsz, D_STATE, SSM_INNER), F32)
    else:
        init_t = init.astype(F32).transpose(1, 0, 4, 2, 3).reshape(2, bsz, D_STATE, SSM_INNER)
    yscan, fin = _ssd_scan(xbc, dt_raw, dt_bias, a_log, init_t)
    out = _ssd_out(yscan, xbc, z, y, mod, d_skip, norm_w, out_w, mixg, mixb, tl=min(L, 256))
    fin = fin.reshape(2, bsz, D_STATE, SSM_HEADS, SSM_HEAD_DIM).transpose(1, 0, 3, 4, 2)
    return out, fin


def _moe_prep_kernel(y_ref, mod_ref, rwT_ref, h_ref, affT_ref):
    h32 = y_ref[0] * (1.0 + mod_ref[0, 4:5, :]) + mod_ref[0, 3:4, :]
    h = h32.astype(BF16)
    h_ref[...] = h
    ne = affT_ref.shape[0]
    nt = (((1,), (1,)), ((), ()))
    both = lax.dot_general(rwT_ref[...], h, nt, preferred_element_type=F32)
    h_lo = (h32 - h.astype(F32)).astype(BF16)
    logT = both[:ne] + both[ne:] + lax.dot_general(rwT_ref[:ne], h_lo, nt, preferred_element_type=F32)
    ex = jnp.exp(logT - jnp.max(logT, axis=0, keepdims=True))
    affT_ref[...] = ex / jnp.sum(ex, axis=0, keepdims=True)


def _moe_prep(y, mod, router_w, tl):
    bsz, L, d = y.shape
    mi = _mod_index(mod)
    nt = L // tl
    ne = router_w.shape[1]
    rwT = router_w.T
    rw_hi = rwT.astype(BF16)
    return pl.pallas_call(
        _moe_prep_kernel,
        grid=(bsz, nt),
        in_specs=[
            pl.BlockSpec((1, tl, d), lambda i, j: (i, j, 0)),
            pl.BlockSpec((1, MOD_ROWS, d), lambda i, j: (mi(i), 0, 0)),
            pl.BlockSpec((2 * ne, d), lambda i, j: (0, 0)),
        ],
        out_specs=[
            pl.BlockSpec((tl, d), lambda i, j: (i * nt + j, 0)),
            pl.BlockSpec((ne, tl), lambda i, j: (0, i * nt + j)),
        ],
        out_shape=[
            jax.ShapeDtypeStruct((bsz * L, d), BF16),
            jax.ShapeDtypeStruct((ne, bsz * L), F32),
        ],
        compiler_params=_params("parallel", "parallel"),
        name="moe_prep",
    )(y, mod, jnp.concatenate([rw_hi, (rwT - rw_hi.astype(F32)).astype(BF16)], axis=0))


def _moe_select_kernel(aff_ref, upper_ref, gate_ref, *, cap):
    ne, T = aff_ref.shape
    aff = aff_ref[...]
    bits = lax.bitcast_convert_type(aff, jnp.int32)

    def search(i, v):
        cand = v | lax.shift_left(jnp.int32(1), 30 - i)
        cnt = jnp.sum(jnp.where(bits >= cand, 1.0, 0.0), axis=1, keepdims=True)
        return jnp.where(cnt >= cap, cand, v)

    thr = lax.fori_loop(0, 31, search, jnp.zeros((ne, 1), jnp.int32))
    need = cap - jnp.sum(jnp.where(bits > thr, 1.0, 0.0), axis=1, keepdims=True)
    upper = upper_ref[...]
    tie_rank = jnp.zeros((ne, 1), F32)
    for j in range(T // LANES):
        ls = slice(j * LANES, (j + 1) * LANES)
        blk = bits[:, ls]
        eq = jnp.where(blk == thr, 1.0, 0.0)
        eq_incl = jnp.dot(eq.astype(BF16), upper, preferred_element_type=F32)
        sel = (blk > thr) | ((blk == thr) & (eq_incl - eq + tie_rank < need))
        tie_rank = tie_rank + eq_incl[:, LANES - 1:LANES]
        gate_ref[:, ls] = jnp.where(sel, aff[:, ls], -1.0)


def _moe_slot_kernel(gate_ref, upper_ref, pos_ref, total_ref, *, tb):
    ne, T = gate_ref.shape
    upper = upper_ref[...]
    lane = lax.broadcasted_iota(jnp.int32, (ne, LANES), 1)
    slot = jnp.zeros((ne, 1), F32)
    totals = jnp.zeros((ne, LANES), F32)
    for j in range(T // LANES):
        t0 = j * LANES
        sel = gate_ref[:, t0:t0 + LANES] >= 0.0
        m = jnp.where(sel, 1.0, 0.0)
        m_incl = jnp.dot(m.astype(BF16), upper, preferred_element_type=F32)
        pos_ref[:, t0:t0 + LANES] = jnp.where(sel, m_incl - m + slot, -1.0).astype(jnp.int32)
        slot = slot + m_incl[:, LANES - 1:LANES]
        if (t0 + LANES) % tb == 0:
            totals = jnp.where(lane == t0 // tb, slot, totals)
            slot = jnp.zeros((ne, 1), F32)
    total_ref[...] = totals.astype(jnp.int32)


def _moe_select(affT, cap, tb, nb):
    ne, T = affT.shape
    r = jnp.arange(LANES)
    upper = (r[:, None] <= r[None, :]).astype(BF16)
    full = pl.BlockSpec((ne, T), lambda i: (0, 0))
    tri = pl.BlockSpec((LANES, LANES), lambda i: (0, 0))
    small = pl.BlockSpec((ne, LANES), lambda i: (0, 0))
    gate = pl.pallas_call(
        functools.partial(_moe_select_kernel, cap=cap),
        grid=(1,),
        in_specs=[full, tri],
        out_specs=full,
        out_shape=jax.ShapeDtypeStruct((ne, T), F32),
        compiler_params=_params("arbitrary"),
        name="moe_select",
    )(affT, upper)
    gate = gate.reshape(ne, T // (MOE_GROUP * nb), nb, MOE_GROUP).transpose(0, 2, 1, 3).reshape(ne, T)
    pos, totals = pl.pallas_call(
        functools.partial(_moe_slot_kernel, tb=tb),
        grid=(1,),
        in_specs=[full, tri],
        out_specs=[full, small],
        out_shape=[jax.ShapeDtypeStruct((ne, T), jnp.int32), jax.ShapeDtypeStruct((ne, LANES), jnp.int32)],
        compiler_params=_params("arbitrary"),
        name="moe_slot",
    )(gate, upper)
    return gate, pos, totals


def _moe_ffn_kernel(tot_ref, h_ref, pos_ref, gate_ref, wg_ref, wu_ref, wd_ref, y_ref, mod_ref, lng_ref,
                    lnb_ref, o_ref, acc_ref, hot_ref, yk_ref, cnt_ref):
    b = pl.program_id(0)
    e = pl.program_id(1)
    ne = pl.num_programs(1)
    rows = MOE_ROWS
    tb, d = acc_ref.shape

    @pl.when(e == 0)
    def _():
        acc_ref[...] = jnp.zeros(acc_ref.shape, F32)
        hot_ref[...] = jnp.zeros(hot_ref.shape, BF16)
        yk_ref[...] = jnp.zeros(yk_ref.shape, BF16)
        cnt_ref[0] = 0

    def flush():
        acc_ref[...] += lax.dot_general(hot_ref[...].reshape(MOE_BATCH * rows, tb),
                                        yk_ref[...].reshape(MOE_BATCH * rows, d),
                                        (((0,), (0,)), ((), ())), preferred_element_type=F32)
        hot_ref[...] = jnp.zeros(hot_ref.shape, BF16)
        cnt_ref[0] = 0

    slot_iota = lax.broadcasted_iota(jnp.int32, (rows, tb), 0)

    def chunk(k, carry):
        hit = pos_ref[0] == slot_iota + k * rows
        onehot = jnp.where(hit, 1.0, 0.0).astype(BF16)
        x = jnp.dot(onehot, h_ref[...].reshape(tb, d), preferred_element_type=F32).astype(BF16)
        gate = jnp.sum(jnp.where(hit, gate_ref[0], 0.0), axis=1, keepdims=True)
        hid = _silu(jnp.dot(x, wg_ref[0, 0], preferred_element_type=F32)) * jnp.dot(
            x, wu_ref[0, 0], preferred_element_type=F32)
        yk = jnp.dot(hid.astype(BF16), wd_ref[0, 0], preferred_element_type=F32)
        n = cnt_ref[0]
        hot_ref[n] = onehot
        yk_ref[n] = (yk * gate).astype(BF16)
        cnt_ref[0] = n + 1

        @pl.when(n + 1 == MOE_BATCH)
        def _():
            flush()

        return carry

    lax.fori_loop(0, (tot_ref[b * ne + e] + rows - 1) // rows, chunk, 0)

    @pl.when(e == ne - 1)
    def _():
        @pl.when(cnt_ref[0] > 0)
        def _():
            flush()

        nmod = mod_ref.shape[0]
        seg = tb // nmod
        y = y_ref[...].reshape(tb, d)
        for s in range(nmod):
            rs = slice(s * seg, (s + 1) * seg)
            out = _layer_norm(ALPHA * y[rs] + mod_ref[s, 5:6, :] * acc_ref[rs, :], lng_ref[...],
                              lnb_ref[...])
            o_ref[s * seg // MOE_GROUP:(s + 1) * seg // MOE_GROUP] = out.reshape(
                seg // MOE_GROUP, 1, MOE_GROUP, d)


def _moe_ffn(y, h, pos, gate, totals, mod, layer, w_gate, w_up, w_down, g, b, tb):
    bsz, L, d = y.shape
    T = bsz * L
    _, ne, _, ff = w_gate.shape
    nb = T // tb
    ng = tb // MOE_GROUP
    nmod = mod.shape[0]
    assert nmod == 1 or (nmod == bsz and L % (MOE_GROUP * nb) == 0)
    tot = totals[:, :nb].T.reshape(-1)
    row = pl.BlockSpec((1, 1, tb), lambda i, e, t: (e, 0, i))
    tok = pl.BlockSpec((ng, 1, MOE_GROUP, d), lambda i, e, t: (0, i, 0, 0))
    vec = pl.BlockSpec((1, d), lambda i, e, t: (0, 0))
    grid_spec = pltpu.PrefetchScalarGridSpec(
        num_scalar_prefetch=1,
        grid=(nb, ne),
        in_specs=[
            tok, row, row,
            pl.BlockSpec((1, 1, d, ff), lambda i, e, t: (layer, e, 0, 0)),
            pl.BlockSpec((1, 1, d, ff), lambda i, e, t: (layer, e, 0, 0)),
            pl.BlockSpec((1, 1, ff, d), lambda i, e, t: (layer, e, 0, 0)),
            tok,
            pl.BlockSpec((nmod, MOD_ROWS, d), lambda i, e, t: (0, 0, 0)),
            vec, vec,
        ],
        out_specs=tok,
        scratch_shapes=[pltpu.VMEM((tb, d), F32), pltpu.VMEM((MOE_BATCH, MOE_ROWS, tb), BF16),
                        pltpu.VMEM((MOE_BATCH, MOE_ROWS, d), BF16), pltpu.SMEM((1,), jnp.int32)],
    )
    out = pl.pallas_call(
        _moe_ffn_kernel,
        grid_spec=grid_spec,
        out_shape=jax.ShapeDtypeStruct((ng, nb, MOE_GROUP, d), F32),
        compiler_params=_params("parallel", "arbitrary"),
        name="moe_ffn",
    )(tot, h.reshape(ng, nb, MOE_GROUP, d), pos.reshape(ne, 1, T), gate.reshape(ne, 1, T), w_gate, w_up, w_down,
      y.reshape(ng, nb, MOE_GROUP, d), mod, g.reshape(1, d), b.reshape(1, d))
    return out.reshape(bsz, L, d)


def _moe_layer(y, mod, router_w, layer, w_gate, w_up, w_down, g, b):
    bsz, L, d = y.shape
    T = bsz * L
    ne = router_w.shape[1]
    cap = EC_CAPACITY_FACTOR * T // ne
    tb = min(T, MOE_TOKEN_BLOCK)
    h, affT = _moe_prep(y, mod, router_w, tl=min(L, 512))
    gate, pos, totals = _moe_select(affT, cap, tb, T // tb)
    return _moe_ffn(y, h, pos, gate, totals, mod, layer, w_gate, w_up, w_down, g, b, tb)


def _cast_kernel(x_ref, o_ref):
    o_ref[...] = x_ref[...].astype(o_ref.dtype)


def _cast_bf16(w):
    depth, ne, r, c = w.shape
    blk = pl.BlockSpec((1, 1, r, c), lambda i, j: (i, j, 0, 0))
    return pl.pallas_call(
        _cast_kernel,
        grid=(depth, ne),
        in_specs=[blk],
        out_specs=blk,
        out_shape=jax.ShapeDtypeStruct(w.shape, BF16),
        compiler_params=_params("parallel", "parallel"),
        name="cast_bf16",
    )(w)


def kernel(x_prompt, x_sample, state_ssd, c, c_ctx, ada_w, ada_b, ln_mix_g, ln_mix_b, ln_ffn_g, ln_ffn_b, conv_pw1_w, conv_pw1_b, conv_dw_w, conv_dw_b, conv_ln_g, conv_ln_b, conv_pw2_w, conv_pw2_b, ssd_in_w, ssd_conv_w, ssd_conv_b, ssd_dt_bias, ssd_a_log, ssd_d_skip, ssd_norm_w, ssd_out_w, router_w, moe_w_gate, moe_w_up, moe_w_down):
    d = x_prompt.shape[-1]
    nb_s = x_sample.shape[0]
    cond = jnp.concatenate([c_ctx[None], c, jnp.zeros((MOD_ROWS - 1 - nb_s, d), F32)], axis=0)
    mods = _ada_mod(cond, ada_w, ada_b).reshape(DEPTH, MOD_ROWS, 6, d)
    mods = jnp.pad(mods, ((0, 0), (0, 0), (0, MOD_ROWS - 6), (0, 0)))
    w_gate, w_up, w_down = _cast_bf16(moe_w_gate), _cast_bf16(moe_w_up), _cast_bf16(moe_w_down)
    yp, ys = x_prompt, x_sample
    new_states = []
    for i in range(DEPTH):
        mod_p = mods[i, 0:1]
        mod_s = mods[i, 1:1 + nb_s]
        k = i // 2
        if i % 2 == 0:
            cw = (conv_pw1_w[k], conv_pw1_b[k], conv_dw_w[k], conv_dw_b[k], conv_ln_g[k],
                  conv_ln_b[k], conv_pw2_w[k], conv_pw2_b[k])
            yp = _conformer_layer(yp, mod_p, cw, "seq", ln_mix_g[i], ln_mix_b[i])
            ys = _conformer_layer(ys, mod_s, cw, "row" if k % 2 == 0 else "col", ln_mix_g[i],
                                  ln_mix_b[i])
        else:
            in_w = ssd_in_w[k]
            bw = SSM_INNER // 2
            in_split = (_cast_cols(in_w, 0, SSM_INNER // bw, bw),
                        _cast_cols(in_w, SSM_INNER // bw, SSM_CONV_DIM // bw, bw),
                        _dt_weight(in_w, SSM_INNER + SSM_CONV_DIM, 2 * SSM_HEADS))
            sw = (in_split, ssd_conv_w[k], ssd_conv_b[k], ssd_dt_bias[k], ssd_a_log[k],
                  ssd_d_skip[k], ssd_norm_w[k], ssd_out_w[k].astype(BF16))
            yp, fin = _ssd_layer(yp, mod_p, sw, None, ln_mix_g[i], ln_mix_b[i])
            ys, _ = _ssd_layer(ys, mod_s, sw, state_ssd[:, k], ln_mix_g[i], ln_mix_b[i])
            new_states.append(fin.astype(x_prompt.dtype))
        yp = _moe_layer(yp, mod_p, router_w[i], i, w_gate, w_up, w_down, ln_ffn_g[i], ln_ffn_b[i])
        ys = _moe_layer(ys, mod_s, router_w[i], i, w_gate, w_up, w_down, ln_ffn_g[i], ln_ffn_b[i])
    return (yp, ys, jnp.stack(new_states, axis=1))
```

```python
import functools

import jax
import jax.numpy as jnp
from jax import lax
from jax.experimental import pallas as pl
from jax.experimental.pallas import tpu as pltpu

F32 = jnp.float32
BF16 = jnp.bfloat16
HIGHEST = lax.Precision.HIGHEST

D_MODEL = 1024
DEPTH = 4
GRID_W = 64
CONV_K = 31
SSM_INNER = 2 * D_MODEL
SSM_HEAD_DIM = 64
SSM_HEADS = SSM_INNER // SSM_HEAD_DIM
SSM_GROUPS = 4
D_STATE = 128
SSM_CONV_K = 5
SSM_BC = SSM_GROUPS * D_STATE
SSM_CONV_DIM = SSM_INNER + 2 * SSM_BC
SSD_CHUNK = 128
N_EXPERTS = 16
EC_CAPACITY_FACTOR = 2
ALPHA = (2 * DEPTH) ** 0.25
LN_EPS = 1e-5

LANES = 128
SUBLANES = 8
VMEM_LIMIT = 56 * 1024 * 1024
MOD_ROWS = 8
CONV_PAD = 16
MOE_TOKEN_BLOCK = 1024
MOE_ROWS = 160
MOE_GROUP = 16
MOE_EXPERTS_PER_STEP = 2


def _params(*sem):
    return pltpu.CompilerParams(dimension_semantics=sem, vmem_limit_bytes=VMEM_LIMIT)


def _layer_norm(x, g, b):
    mu = jnp.mean(x, axis=-1, keepdims=True)
    xc = x - mu
    var = jnp.mean(xc * xc, axis=-1, keepdims=True)
    return xc * lax.rsqrt(var + LN_EPS) * g + b


def _silu(x):
    return x * jax.nn.sigmoid(x)


def _softplus(x):
    return jnp.maximum(x, 0.0) + jnp.log(1.0 + jnp.exp(-jnp.abs(x)))


def _ada_kernel(c_ref, w_ref, b_ref, o_ref):
    x = _silu(c_ref[...])
    o_ref[0] = jnp.dot(x, w_ref[0], precision=HIGHEST, preferred_element_type=F32) + b_ref[0]


def _ada_mod(cond, ada_w, ada_b):
    depth, d, n = ada_w.shape
    tn = 1536
    return pl.pallas_call(
        _ada_kernel,
        grid=(depth, n // tn),
        in_specs=[
            pl.BlockSpec((MOD_ROWS, d), lambda i, j: (0, 0)),
            pl.BlockSpec((1, d, tn), lambda i, j: (i, 0, j)),
            pl.BlockSpec((1, 1, tn), lambda i, j: (i, 0, j)),
        ],
        out_specs=pl.BlockSpec((1, MOD_ROWS, tn), lambda i, j: (i, 0, j)),
        out_shape=jax.ShapeDtypeStruct((depth, MOD_ROWS, n), F32),
        compiler_params=_params("parallel", "parallel"),
        name="ada_mod",
    )(cond, ada_w, ada_b.reshape(depth, 1, n))


def _mod_index(mod):
    if mod.shape[0] == 1:
        return lambda b: 0
    return lambda b: b


def _pw1_glu_kernel(x_ref, mod_ref, w_ref, b_ref, o_ref):
    d = x_ref.shape[-1]
    x = x_ref[0]
    h = (x * (1.0 + mod_ref[0, 1:2, :]) + mod_ref[0, 0:1, :]).astype(BF16)
    u = jnp.dot(h, w_ref[...], preferred_element_type=F32) + b_ref[...]
    o_ref[0] = u[:, :d] * jax.nn.sigmoid(u[:, d:])


def _pw1_glu(x, mod, w, b, tl):
    bsz, L, d = x.shape
    mi = _mod_index(mod)
    return pl.pallas_call(
        _pw1_glu_kernel,
        grid=(bsz, L // tl),
        in_specs=[
            pl.BlockSpec((1, tl, d), lambda i, j: (i, j, 0)),
            pl.BlockSpec((1, MOD_ROWS, d), lambda i, j: (mi(i), 0, 0)),
            pl.BlockSpec((d, 2 * d), lambda i, j: (0, 0)),
            pl.BlockSpec((1, 2 * d), lambda i, j: (0, 0)),
        ],
        out_specs=pl.BlockSpec((1, tl, d), lambda i, j: (i, j, 0)),
        out_shape=jax.ShapeDtypeStruct((bsz, L, d), F32),
        compiler_params=_params("parallel", "parallel"),
        name="conv_pw1_glu",
    )(x, mod, w, b)


def _conv_tail(v, y, mod_ref, lng_ref, lnb_ref, w2_ref, b2_ref, mixg_ref, mixb_ref):
    v = _silu(_layer_norm(v, lng_ref[...], lnb_ref[...]))
    m = jnp.dot(v.astype(BF16), w2_ref[...], preferred_element_type=F32) + b2_ref[...]
    return _layer_norm(ALPHA * y + mod_ref[0, 2:3, :] * m, mixg_ref[...], mixb_ref[...])


def _dwconv_seq_kernel(u_ref, y_ref, mod_ref, dww_ref, dwb_ref, lng_ref, lnb_ref, w2_ref, b2_ref,
                       mixg_ref, mixb_ref, o_ref, pad_ref, sh_ref, v_ref, *, seg, nseg):
    d = u_ref.shape[-1]
    sp = seg + 2 * CONV_PAD
    n = nseg * sp
    zeros = jnp.zeros((CONV_PAD, d), F32)
    for s in range(nseg):
        pad_ref[s * sp:s * sp + CONV_PAD, :] = zeros
        pad_ref[s * sp + CONV_PAD:s * sp + CONV_PAD + seg, :] = u_ref[0, s * seg:(s + 1) * seg, :]
        pad_ref[s * sp + CONV_PAD + seg:(s + 1) * sp, :] = zeros
    first = CONV_PAD - CONV_K // 2
    rows = min(seg, 64)
    for c in range(d // LANES):
        ln = slice(c * LANES, (c + 1) * LANES)
        for q in range(1, SUBLANES):
            sh_ref[q, 0:n - SUBLANES, :] = pad_ref[q:q + n - SUBLANES, ln]
        for s in range(nseg):
            for r in range(seg // rows):
                acc = jnp.zeros((rows, LANES), F32) + dwb_ref[:, ln]
                for k in range(CONV_K):
                    q = (first + k) % SUBLANES
                    base = s * sp + r * rows + (first + k) - q
                    tap = pad_ref[base:base + rows, ln] if q == 0 else sh_ref[q, base:base + rows, :]
                    acc = acc + dww_ref[k:k + 1, ln] * tap
                v_ref[s * seg + r * rows:s * seg + (r + 1) * rows, ln] = acc
    o_ref[0] = _conv_tail(v_ref[...], y_ref[0], mod_ref, lng_ref, lnb_ref, w2_ref, b2_ref,
                          mixg_ref, mixb_ref)


def _dwconv_col_kernel(u_ref, y_ref, mod_ref, dww_ref, dwb_ref, lng_ref, lnb_ref, w2_ref, b2_ref,
                       mixg_ref, mixb_ref, o_ref, pad_ref, v_ref):
    hh, wt, d = u_ref.shape[1:]
    zeros = jnp.zeros((CONV_PAD, wt, d), F32)
    pad_ref[0:CONV_PAD] = zeros
    pad_ref[CONV_PAD:CONV_PAD + hh] = u_ref[0]
    pad_ref[CONV_PAD + hh:CONV_PAD + hh + CONV_PAD] = zeros
    first = CONV_PAD - CONV_K // 2
    rows = 32
    for r in range(hh // rows):
        for c in range(d // LANES):
            ln = slice(c * LANES, (c + 1) * LANES)
            base = first + r * rows
            acc = jnp.zeros((rows, wt, LANES), F32) + dwb_ref[:, ln]
            for k in range(CONV_K):
                acc = acc + dww_ref[k:k + 1, ln] * pad_ref[base + k:base + k + rows, :, ln]
            v_ref[r * rows:(r + 1) * rows, :, ln] = acc
    out = _conv_tail(v_ref[...].reshape(hh * wt, d), y_ref[0].reshape(hh * wt, d), mod_ref,
                     lng_ref, lnb_ref, w2_ref, b2_ref, mixg_ref, mixb_ref)
    o_ref[0] = out.reshape(hh, wt, d)


def _conv_weights_specs(d):
    vec = pl.BlockSpec((1, d), lambda i, j: (0, 0))
    return [
        pl.BlockSpec((CONV_K, d), lambda i, j: (0, 0)),
        vec, vec, vec,
        pl.BlockSpec((d, d), lambda i, j: (0, 0)),
        vec, vec, vec,
    ]


def _dwconv_seq(u, y, mod, weights, seg, nseg):
    bsz, L, d = u.shape
    tl = seg * nseg
    mi = _mod_index(mod)
    tok = pl.BlockSpec((1, tl, d), lambda i, j: (i, j, 0))
    return pl.pallas_call(
        functools.partial(_dwconv_seq_kernel, seg=seg, nseg=nseg),
        grid=(bsz, L // tl),
        in_specs=[tok, tok, pl.BlockSpec((1, MOD_ROWS, d), lambda i, j: (mi(i), 0, 0))]
        + _conv_weights_specs(d),
        out_specs=tok,
        out_shape=jax.ShapeDtypeStruct((bsz, L, d), F32),
        scratch_shapes=[pltpu.VMEM((nseg * (seg + 2 * CONV_PAD), d), F32),
                        pltpu.VMEM((SUBLANES, nseg * (seg + 2 * CONV_PAD), LANES), F32),
                        pltpu.VMEM((tl, d), F32)],
        compiler_params=_params("parallel", "parallel"),
        name="conv_dw_seq",
    )(u, y, mod, *weights)


def _dwconv_col(u, y, mod, weights):
    bsz, L, d = u.shape
    hh = L // GRID_W
    u4 = u.reshape(bsz, hh, GRID_W, d)
    y4 = y.reshape(bsz, hh, GRID_W, d)
    mi = _mod_index(mod)
    tok = pl.BlockSpec((1, hh, SUBLANES, d), lambda i, j: (i, 0, j, 0))
    out = pl.pallas_call(
        _dwconv_col_kernel,
        grid=(bsz, GRID_W // SUBLANES),
        in_specs=[tok, tok, pl.BlockSpec((1, MOD_ROWS, d), lambda i, j: (mi(i), 0, 0))]
        + _conv_weights_specs(d),
        out_specs=tok,
        out_shape=jax.ShapeDtypeStruct((bsz, hh, GRID_W, d), F32),
        scratch_shapes=[pltpu.VMEM((hh + 2 * CONV_PAD, SUBLANES, d), F32),
                        pltpu.VMEM((hh, SUBLANES, d), F32)],
        compiler_params=_params("parallel", "parallel"),
        name="conv_dw_col",
    )(u4, y4, mod, *weights)
    return out.reshape(bsz, L, d)


def _conformer_layer(y, mod, cw, axis, mixg, mixb):
    pw1_w, pw1_b, dw_w, dw_b, ln_g, ln_b, pw2_w, pw2_b = cw
    bsz, L, d = y.shape
    row = lambda a: a.reshape(1, -1)
    u = _pw1_glu(y, mod, pw1_w.astype(BF16), row(pw1_b), tl=min(L, 512))
    weights = (dw_w, row(dw_b), row(ln_g), row(ln_b), pw2_w.astype(BF16), row(pw2_b),
               row(mixg), row(mixb))
    if axis == "seq":
        return _dwconv_seq(u, y, mod, weights, seg=L, nseg=1)
    if axis == "row":
        return _dwconv_seq(u, y, mod, weights, seg=GRID_W, nseg=min(8, L // GRID_W))
    return _dwconv_col(u, y, mod, weights)


def _ssd_in_kernel(x_ref, mod_ref, wz_ref, wx_ref, wdt_ref, z_ref, xbc_ref, dt_ref):
    x = x_ref[0]
    h32 = x * (1.0 + mod_ref[0, 1:2, :]) + mod_ref[0, 0:1, :]
    h = h32.astype(BF16)
    z_ref[0] = jnp.dot(h, wz_ref[...], preferred_element_type=F32).astype(z_ref.dtype)
    xbc_ref[0] = jnp.dot(h, wx_ref[...], preferred_element_type=F32).astype(xbc_ref.dtype)
    h_lo = (h32 - h.astype(F32)).astype(BF16)
    ndt = dt_ref.shape[-1]
    hi = jnp.dot(h, wdt_ref[...], preferred_element_type=F32)
    dt_ref[0] = hi[:, :ndt] + hi[:, ndt:] + jnp.dot(h_lo, wdt_ref[:, :ndt], preferred_element_type=F32)


def _ssd_in_proj(y, mod, wz, wx, wdt2, tl):
    bsz, L, d = y.shape
    mi = _mod_index(mod)
    ndt = wdt2.shape[1] // 2
    full = lambda a: pl.BlockSpec(a.shape, lambda i, j: (0, 0))
    return pl.pallas_call(
        _ssd_in_kernel,
        grid=(bsz, L // tl),
        in_specs=[
            pl.BlockSpec((1, tl, d), lambda i, j: (i, j, 0)),
            pl.BlockSpec((1, MOD_ROWS, d), lambda i, j: (mi(i), 0, 0)),
            full(wz), full(wx), full(wdt2),
        ],
        out_specs=[
            pl.BlockSpec((1, tl, SSM_INNER), lambda i, j: (i, j, 0)),
            pl.BlockSpec((1, tl, SSM_CONV_DIM), lambda i, j: (i, j, 0)),
            pl.BlockSpec((1, tl, ndt), lambda i, j: (i, j, 0)),
        ],
        out_shape=[
            jax.ShapeDtypeStruct((bsz, L, SSM_INNER), BF16),
            jax.ShapeDtypeStruct((bsz, L, SSM_CONV_DIM), BF16),
            jax.ShapeDtypeStruct((bsz, L, ndt), F32),
        ],
        compiler_params=_params("parallel", "parallel"),
        name="ssd_in_proj",
    )(y, mod, wz, wx, wdt2)


def _ssd_conv_kernel(x_ref, w_ref, b_ref, o_ref, pad_ref):
    L, cb = x_ref.shape[1:]
    zeros = jnp.zeros((SUBLANES, cb), F32)
    pad_ref[0:SUBLANES, :] = zeros
    pad_ref[SUBLANES:SUBLANES + L, :] = x_ref[0].astype(F32)
    pad_ref[SUBLANES + L:2 * SUBLANES + L, :] = zeros
    first = SUBLANES - SSM_CONV_K // 2
    rows = min(L, 256)
    for r in range(L // rows):
        for c in range(cb // LANES):
            ln = slice(c * LANES, (c + 1) * LANES)
            base = first + r * rows
            acc = jnp.zeros((rows, LANES), F32) + b_ref[:, ln]
            for k in range(SSM_CONV_K):
                acc = acc + w_ref[k:k + 1, ln] * pad_ref[base + k:base + k + rows, ln]
            o_ref[0, r * rows:(r + 1) * rows, ln] = _silu(acc).astype(o_ref.dtype)


def _ssd_conv(xbc, w, b):
    bsz, L, cd = xbc.shape
    cb = max(w for w in range(LANES, cd + 1, LANES) if cd % w == 0 and (w * L <= (1 << 20) or w == LANES))
    return pl.pallas_call(
        _ssd_conv_kernel,
        grid=(bsz, cd // cb),
        in_specs=[
            pl.BlockSpec((1, L, cb), lambda i, j: (i, 0, j)),
            pl.BlockSpec((SSM_CONV_K, cb), lambda i, j: (0, j)),
            pl.BlockSpec((1, cb), lambda i, j: (0, j)),
        ],
        out_specs=pl.BlockSpec((1, L, cb), lambda i, j: (i, 0, j)),
        out_shape=jax.ShapeDtypeStruct((bsz, L, cd), BF16),
        scratch_shapes=[pltpu.VMEM((L + 2 * SUBLANES, cb), F32)],
        compiler_params=_params("parallel", "parallel"),
        name="ssd_conv",
    )(xbc, w, b.reshape(1, cd))


def _split2(x):
    hi = x.astype(BF16)
    return jnp.concatenate([hi, (x - hi.astype(F32)).astype(BF16)], axis=1)


def _split3(x):
    hi = x.astype(BF16)
    r1 = x - hi.astype(F32)
    mid = r1.astype(BF16)
    lo = (r1 - mid.astype(F32)).astype(BF16)
    return jnp.concatenate([hi, mid, lo], axis=1)


def _ssd_scan_kernel(xs_ref, b_ref, c_ref, dt_ref, bias_row_ref, alog_row_ref, tri_ref, e3_ref, init_ref,
                     y_ref, fin_ref, s_ref):
    d = pl.program_id(0)
    c = pl.program_id(2)
    q = SSD_CHUNK
    nh = SSM_HEADS
    gw = SSM_INNER // SSM_GROUPS

    @pl.when(c == 0)
    def _():
        s_ref[...] = init_ref[0, 0]

    is_f = d == 0
    tri = tri_ref[0]
    mask = tri > 0.5
    dt2 = _softplus(dt_ref[0] + bias_row_ref[...])
    a2 = dt2 * -jnp.exp(alog_row_ref[...])
    dt2T = dt2.T
    c3 = jnp.dot(tri.astype(BF16), _split3(a2), preferred_element_type=F32)
    cum2 = c3[:, :LANES] + c3[:, LANES:2 * LANES] + c3[:, 2 * LANES:]
    cumT2 = cum2.T
    cum = jnp.where(is_f, cum2[:, :nh], cum2[:, nh:2 * nh])
    cumT = jnp.where(is_f, cumT2[:nh], cumT2[nh:2 * nh])
    dtT = jnp.where(is_f, dt2T[:nh], dt2T[nh:2 * nh])
    tot2 = jnp.where(is_f, cum2[q - 1:q], cum2[0:1])
    small = jnp.concatenate([dt2 * jnp.exp(jnp.minimum(tot2 - cum2, 0.0)), jnp.exp(cum2),
                             jnp.broadcast_to(jnp.exp(tot2), (SUBLANES, LANES))], axis=0)
    wide = jnp.dot(_split2(small), e3_ref[0], preferred_element_type=F32)
    w_state = wide[0:q]
    w_off = wide[q:2 * q]
    w_tot = wide[2 * q:2 * q + 1]

    xs_b = xs_ref[0]
    xs = xs_b.astype(F32)
    xdec = (xs * w_state).astype(BF16)
    lane = lax.broadcasted_iota(jnp.int32, (q, SSM_INNER), 1) % LANES
    zero = jnp.zeros((), BF16)
    x_stack = jnp.concatenate([jnp.where(lane < SSM_HEAD_DIM, xs_b, zero),
                               jnp.where(lane >= SSM_HEAD_DIM, xs_b, zero)], axis=0)
    for g in range(SSM_GROUPS):
        gs = slice(g * gw, (g + 1) * gw)
        bg = b_ref[0, :, g * D_STATE:(g + 1) * D_STATE]
        cg = c_ref[0, :, g * D_STATE:(g + 1) * D_STATE]
        cb = lax.dot_general(cg, bg, (((1,), (1,)), ((), ())), preferred_element_type=F32)
        s_g = s_ref[:, gs]
        y_off = jnp.dot(cg, s_g.astype(BF16), preferred_element_type=F32) * w_off[:, gs]
        s_ref[:, gs] = w_tot[:, gs] * s_g + jnp.dot(bg.astype(F32).T.astype(BF16), xdec[:, gs],
                                                    preferred_element_type=F32)
        pairs = gw // LANES
        for p in range(pairs):
            ls = slice(g * gw + p * LANES, g * gw + (p + 1) * LANES)
            ms = []
            for j in range(LANES // SSM_HEAD_DIM):
                h = (g * gw + p * LANES) // SSM_HEAD_DIM + j
                diff = cum[:, h:h + 1] - cumT[h:h + 1, :]
                lm = jnp.exp(jnp.where(mask, diff, -1e30))
                ms.append((cb * lm * dtT[h:h + 1, :]).astype(BF16))
            y_diag = jnp.dot(jnp.concatenate(ms, axis=1), x_stack[:, ls], preferred_element_type=F32)
            y_ref[0, 0, :, ls] = (y_diag + y_off[:, p * LANES:(p + 1) * LANES]).astype(y_ref.dtype)

    @pl.when(c == pl.num_programs(2) - 1)
    def _():
        fin_ref[0, 0] = s_ref[...]


def _ssd_scan(xbc, dt_raw, dt_bias, a_log, init):
    bsz, L, _ = xbc.shape
    nc = L // SSD_CHUNK
    q = SSD_CHUNK
    ndt = LANES
    chunk = lambda d, c: c + d * (nc - 1 - 2 * c)
    r = jnp.arange(q)
    tri = jnp.stack([r[:, None] >= r[None, :], r[:, None] <= r[None, :]]).astype(F32)
    src = jnp.arange(ndt)[None, :, None] - SSM_HEADS * jnp.arange(2)[:, None, None]
    e1 = (src == (jnp.arange(SSM_INNER) // SSM_HEAD_DIM)[None, None, :]).astype(BF16)
    e3 = jnp.concatenate([e1, e1], axis=1)
    pad = lambda a: jnp.pad(a.reshape(-1), (0, ndt - a.size))
    dt_bias, a_log = pad(dt_bias), pad(a_log)
    nblk = SSM_INNER // SSM_BC
    const = lambda shape: pl.BlockSpec(shape, lambda d, b, c: tuple(0 for _ in shape))
    return pl.pallas_call(
        _ssd_scan_kernel,
        grid=(2, bsz, nc),
        in_specs=[
            pl.BlockSpec((1, q, SSM_INNER), lambda d, b, c: (b, chunk(d, c), 0)),
            pl.BlockSpec((1, q, SSM_BC), lambda d, b, c: (b, chunk(d, c), nblk)),
            pl.BlockSpec((1, q, SSM_BC), lambda d, b, c: (b, chunk(d, c), nblk + 1)),
            pl.BlockSpec((1, q, ndt), lambda d, b, c: (b, chunk(d, c), 0)),
            const((1, ndt)), const((1, ndt)),
            pl.BlockSpec((1, q, q), lambda d, b, c: (d, 0, 0)),
            pl.BlockSpec((1, 2 * ndt, SSM_INNER), lambda d, b, c: (d, 0, 0)),
            pl.BlockSpec((1, 1, D_STATE, SSM_INNER), lambda d, b, c: (d, b, 0, 0)),
        ],
        out_specs=[
            pl.BlockSpec((1, 1, q, SSM_INNER), lambda d, b, c: (d, b, chunk(d, c), 0)),
            pl.BlockSpec((1, 1, D_STATE, SSM_INNER), lambda d, b, c: (d, b, 0, 0)),
        ],
        out_shape=[
            jax.ShapeDtypeStruct((2, bsz, L, SSM_INNER), BF16),
            jax.ShapeDtypeStruct((2, bsz, D_STATE, SSM_INNER), F32),
        ],
        scratch_shapes=[pltpu.VMEM((D_STATE, SSM_INNER), F32)],
        compiler_params=_params("parallel", "parallel", "arbitrary"),
        name="ssd_scan",
    )(xbc, xbc, xbc, dt_raw, dt_bias.reshape(1, ndt), a_log.reshape(1, ndt), tri, e3, init)


def _ssd_out_kernel(yf_ref, yb_ref, xs_ref, z_ref, res_ref, mod_ref, dskip_ref, nw_ref, wo_ref,
                    mixg_ref, mixb_ref, o_ref):
    y = yf_ref[0, 0].astype(F32) + yb_ref[0, 0].astype(F32) + dskip_ref[...] * xs_ref[0].astype(F32)
    y = y * _silu(z_ref[0].astype(F32))
    gw = SSM_INNER // SSM_GROUPS
    parts = []
    for g in range(SSM_GROUPS):
        yg = y[:, g * gw:(g + 1) * gw]
        ms = jnp.mean(yg * yg, axis=-1, keepdims=True)
        parts.append(yg * lax.rsqrt(ms + LN_EPS))
    yn = jnp.concatenate(parts, axis=1) * nw_ref[...]
    m = jnp.dot(yn.astype(BF16), wo_ref[...], preferred_element_type=F32)
    o_ref[0] = _layer_norm(ALPHA * res_ref[0] + mod_ref[0, 2:3, :] * m, mixg_ref[...], mixb_ref[...])


def _ssd_out(yscan, xbc, z, y, mod, d_skip, norm_w, out_w, mixg, mixb, tl):
    bsz, L, d = y.shape
    mi = _mod_index(mod)
    inner = pl.BlockSpec((1, tl, SSM_INNER), lambda i, j: (i, j, 0))
    vec = lambda n: pl.BlockSpec((1, n), lambda i, j: (0, 0))
    return pl.pallas_call(
        _ssd_out_kernel,
        grid=(bsz, L // tl),
        in_specs=[
            pl.BlockSpec((1, 1, tl, SSM_INNER), lambda i, j: (0, i, j, 0)),
            pl.BlockSpec((1, 1, tl, SSM_INNER), lambda i, j: (1, i, j, 0)),
            inner, inner,
            pl.BlockSpec((1, tl, d), lambda i, j: (i, j, 0)),
            pl.BlockSpec((1, MOD_ROWS, d), lambda i, j: (mi(i), 0, 0)),
            vec(SSM_INNER), vec(SSM_INNER),
            pl.BlockSpec((SSM_INNER, d), lambda i, j: (0, 0)),
            vec(d), vec(d),
        ],
        out_specs=pl.BlockSpec((1, tl, d), lambda i, j: (i, j, 0)),
        out_shape=jax.ShapeDtypeStruct((bsz, L, d), F32),
        compiler_params=_params("parallel", "parallel"),
        name="ssd_out",
    )(yscan, yscan, xbc, z, y, mod, jnp.repeat(d_skip, SSM_HEAD_DIM).reshape(1, SSM_INNER),
      norm_w.reshape(1, SSM_INNER), out_w, mixg.reshape(1, d), mixb.reshape(1, d))


def _cast_cols(w, first, count, bw):
    rows = w.shape[0]
    return pl.pallas_call(
        _cast_kernel,
        grid=(count,),
        in_specs=[pl.BlockSpec((rows, bw), lambda j: (0, first + j))],
        out_specs=pl.BlockSpec((rows, bw), lambda j: (0, j)),
        out_shape=jax.ShapeDtypeStruct((rows, count * bw), BF16),
        compiler_params=_params("parallel"),
        name="cast_cols",
    )(w)


def _dt_weight_kernel(w_ref, o_ref, *, valid):
    w = w_ref[...]
    lane = lax.broadcasted_iota(jnp.int32, w.shape, 1)
    w = jnp.where(lane < valid, w, 0.0)
    o_ref[...] = _split2(w)


def _dt_weight(w, first_col, valid):
    rows = w.shape[0]
    assert first_col % LANES == 0 and valid <= LANES
    return pl.pallas_call(
        functools.partial(_dt_weight_kernel, valid=valid),
        grid=(1,),
        in_specs=[pl.BlockSpec((rows, LANES), lambda j: (0, first_col // LANES))],
        out_specs=pl.BlockSpec((rows, 2 * LANES), lambda j: (0, 0)),
        out_shape=jax.ShapeDtypeStruct((rows, 2 * LANES), BF16),
        compiler_params=_params("arbitrary"),
        name="dt_weight",
    )(w)


def _ssd_layer(y, mod, sw, init, mixg, mixb):
    (wz, wx, wdt2), conv_w, conv_b, dt_bias, a_log, d_skip, norm_w, out_w = sw
    bsz, L, d = y.shape
    z, xbc, dt_raw = _ssd_in_proj(y, mod, wz, wx, wdt2, tl=min(L, 512))
    xbc = _ssd_conv(xbc, conv_w, conv_b)
    if init is None:
        init_t = jnp.zeros((2, bsz, D_STATE, SSM_INNER), F32)
    else:
        init_t = init.astype(F32).transpose(1, 0, 4, 2, 3).reshape(2, bsz, D_STATE, SSM_INNER)
    yscan, fin = _ssd_scan(xbc, dt_raw, dt_bias, a_log, init_t)
    out = _ssd_out(yscan, xbc, z, y, mod, d_skip, norm_w, out_w, mixg, mixb, tl=min(L, 512))
    fin = fin.reshape(2, bsz, D_STATE, SSM_HEADS, SSM_HEAD_DIM).transpose(1, 0, 3, 4, 2)
    return out, fin


def _moe_prep_kernel(y_ref, mod_ref, rwT_ref, h_ref, affT_ref):
    h32 = y_ref[0] * (1.0 + mod_ref[0, 4:5, :]) + mod_ref[0, 3:4, :]
    h = h32.astype(BF16)
    h_ref[...] = h
    ne = affT_ref.shape[0]
    nt = (((1,), (1,)), ((), ()))
    both = lax.dot_general(rwT_ref[...], h, nt, preferred_element_type=F32)
    h_lo = (h32 - h.astype(F32)).astype(BF16)
    logT = both[:ne] + both[ne:] + lax.dot_general(rwT_ref[:ne], h_lo, nt, preferred_element_type=F32)
    ex = jnp.exp(logT - jnp.max(logT, axis=0, keepdims=True))
    affT_ref[...] = ex / jnp.sum(ex, axis=0, keepdims=True)


def _moe_prep(y, mod, router_w, tl):
    bsz, L, d = y.shape
    mi = _mod_index(mod)
    nt = L // tl
    ne = router_w.shape[1]
    rwT = router_w.T
    rw_hi = rwT.astype(BF16)
    return pl.pallas_call(
        _moe_prep_kernel,
        grid=(bsz, nt),
        in_specs=[
            pl.BlockSpec((1, tl, d), lambda i, j: (i, j, 0)),
            pl.BlockSpec((1, MOD_ROWS, d), lambda i, j: (mi(i), 0, 0)),
            pl.BlockSpec((2 * ne, d), lambda i, j: (0, 0)),
        ],
        out_specs=[
            pl.BlockSpec((tl, d), lambda i, j: (i * nt + j, 0)),
            pl.BlockSpec((ne, tl), lambda i, j: (0, i * nt + j)),
        ],
        out_shape=[
            jax.ShapeDtypeStruct((bsz * L, d), BF16),
            jax.ShapeDtypeStruct((ne, bsz * L), F32),
        ],
        compiler_params=_params("parallel", "parallel"),
        name="moe_prep",
    )(y, mod, jnp.concatenate([rw_hi, (rwT - rw_hi.astype(F32)).astype(BF16)], axis=0))


def _moe_select_kernel(aff_ref, upper_ref, gate_ref, *, cap):
    ne, T = aff_ref.shape
    aff = aff_ref[...]
    bits = lax.bitcast_convert_type(aff, jnp.int32)

    def search(i, v):
        cand = v | lax.shift_left(jnp.int32(1), 30 - i)
        cnt = jnp.sum(jnp.where(bits >= cand, 1.0, 0.0), axis=1, keepdims=True)
        return jnp.where(cnt >= cap, cand, v)

    thr = lax.fori_loop(0, 31, search, jnp.zeros((ne, 1), jnp.int32))
    need = cap - jnp.sum(jnp.where(bits > thr, 1.0, 0.0), axis=1, keepdims=True)
    upper = upper_ref[...]
    tie_rank = jnp.zeros((ne, 1), F32)
    for j in range(T // LANES):
        ls = slice(j * LANES, (j + 1) * LANES)
        blk = bits[:, ls]
        eq = jnp.where(blk == thr, 1.0, 0.0)
        eq_incl = jnp.dot(eq.astype(BF16), upper, preferred_element_type=F32)
        sel = (blk > thr) | ((blk == thr) & (eq_incl - eq + tie_rank < need))
        tie_rank = tie_rank + eq_incl[:, LANES - 1:LANES]
        gate_ref[:, ls] = jnp.where(sel, aff[:, ls], -1.0)


def _moe_slot_kernel(gate_ref, upper_ref, pos_ref, total_ref, *, tb):
    ne, T = gate_ref.shape
    upper = upper_ref[...]
    lane = lax.broadcasted_iota(jnp.int32, (ne, LANES), 1)
    slot = jnp.zeros((ne, 1), F32)
    totals = jnp.zeros((ne, LANES), F32)
    for j in range(T // LANES):
        t0 = j * LANES
        sel = gate_ref[:, t0:t0 + LANES] >= 0.0
        m = jnp.where(sel, 1.0, 0.0)
        m_incl = jnp.dot(m.astype(BF16), upper, preferred_element_type=F32)
        pos_ref[:, t0:t0 + LANES] = jnp.where(sel, m_incl - m + slot, -1.0).astype(jnp.int32)
        slot = slot + m_incl[:, LANES - 1:LANES]
        if (t0 + LANES) % tb == 0:
            totals = jnp.where(lane == t0 // tb, slot, totals)
            slot = jnp.zeros((ne, 1), F32)
    total_ref[...] = totals.astype(jnp.int32)


def _moe_select(affT, cap, tb, nb):
    ne, T = affT.shape
    r = jnp.arange(LANES)
    upper = (r[:, None] <= r[None, :]).astype(BF16)
    full = pl.BlockSpec((ne, T), lambda i: (0, 0))
    tri = pl.BlockSpec((LANES, LANES), lambda i: (0, 0))
    small = pl.BlockSpec((ne, LANES), lambda i: (0, 0))
    gate = pl.pallas_call(
        functools.partial(_moe_select_kernel, cap=cap),
        grid=(1,),
        in_specs=[full, tri],
        out_specs=full,
        out_shape=jax.ShapeDtypeStruct((ne, T), F32),
        compiler_params=_params("arbitrary"),
        name="moe_select",
    )(affT, upper)
    gate = gate.reshape(ne, T // (MOE_GROUP * nb), nb, MOE_GROUP).transpose(0, 2, 1, 3).reshape(ne, T)
    pos, totals = pl.pallas_call(
        functools.partial(_moe_slot_kernel, tb=tb),
        grid=(1,),
        in_specs=[full, tri],
        out_specs=[full, small],
        out_shape=[jax.ShapeDtypeStruct((ne, T), jnp.int32), jax.ShapeDtypeStruct((ne, LANES), jnp.int32)],
        compiler_params=_params("arbitrary"),
        name="moe_slot",
    )(gate, upper)
    return gate, pos, totals


def _moe_ffn_kernel(tot_ref, h_ref, pos_ref, gate_ref, wg_ref, wu_ref, wd_ref, y_ref, mod_ref, lng_ref,
                    lnb_ref, o_ref, acc_ref):
    b = pl.program_id(0)
    eg = pl.program_id(1)
    rows = MOE_ROWS
    tb, d = acc_ref.shape
    per = pos_ref.shape[1]
    ne = pl.num_programs(1) * per

    @pl.when(eg == 0)
    def _():
        acc_ref[...] = jnp.zeros(acc_ref.shape, F32)

    slot_iota = lax.broadcasted_iota(jnp.int32, (rows, tb), 0)
    for j in range(per):
        def chunk(k, carry, j=j):
            hit = pos_ref[0, j:j + 1, :] == slot_iota + k * rows
            onehot = jnp.where(hit, 1.0, 0.0).astype(BF16)
            x = jnp.dot(onehot, h_ref[...].reshape(tb, d), preferred_element_type=F32).astype(BF16)
            gate = jnp.sum(jnp.where(hit, gate_ref[0, j:j + 1, :], 0.0), axis=1, keepdims=True)
            hid = _silu(jnp.dot(x, wg_ref[0, j], preferred_element_type=F32)) * jnp.dot(
                x, wu_ref[0, j], preferred_element_type=F32)
            yk = jnp.dot(hid.astype(BF16), wd_ref[0, j], preferred_element_type=F32)
            yk = (yk * gate).astype(BF16)
            acc_ref[...] += lax.dot_general(onehot, yk, (((0,), (0,)), ((), ())),
                                            preferred_element_type=F32)
            return carry

        lax.fori_loop(0, (tot_ref[b * ne + eg * per + j] + rows - 1) // rows, chunk, 0)

    @pl.when(eg == pl.num_programs(1) - 1)
    def _():
        nmod = mod_ref.shape[0]
        seg = tb // nmod
        y = y_ref[...].reshape(tb, d)
        for s in range(nmod):
            rs = slice(s * seg, (s + 1) * seg)
            out = _layer_norm(ALPHA * y[rs] + mod_ref[s, 5:6, :] * acc_ref[rs, :], lng_ref[...],
                              lnb_ref[...])
            o_ref[s * seg // MOE_GROUP:(s + 1) * seg // MOE_GROUP] = out.reshape(
                seg // MOE_GROUP, 1, MOE_GROUP, d)


def _moe_ffn(y, h, pos, gate, totals, mod, layer, w_gate, w_up, w_down, g, b, tb):
    bsz, L, d = y.shape
    T = bsz * L
    _, ne, _, ff = w_gate.shape
    per = MOE_EXPERTS_PER_STEP
    nb = T // tb
    ng = tb // MOE_GROUP
    nmod = mod.shape[0]
    assert nmod == 1 or (nmod == bsz and L % (MOE_GROUP * nb) == 0)
    tot = totals[:, :nb].T.reshape(-1)
    row = pl.BlockSpec((1, per, tb), lambda i, e, t: (e, 0, i))
    tok = pl.BlockSpec((ng, 1, MOE_GROUP, d), lambda i, e, t: (0, i, 0, 0))
    vec = pl.BlockSpec((1, d), lambda i, e, t: (0, 0))
    grid_spec = pltpu.PrefetchScalarGridSpec(
        num_scalar_prefetch=1,
        grid=(nb, ne // per),
        in_specs=[
            tok, row, row,
            pl.BlockSpec((1, per, d, ff), lambda i, e, t: (layer, e, 0, 0)),
            pl.BlockSpec((1, per, d, ff), lambda i, e, t: (layer, e, 0, 0)),
            pl.BlockSpec((1, per, ff, d), lambda i, e, t: (layer, e, 0, 0)),
            tok,
            pl.BlockSpec((nmod, MOD_ROWS, d), lambda i, e, t: (0, 0, 0)),
            vec, vec,
        ],
        out_specs=tok,
        scratch_shapes=[pltpu.VMEM((tb, d), F32)],
    )
    out = pl.pallas_call(
        _moe_ffn_kernel,
        grid_spec=grid_spec,
        out_shape=jax.ShapeDtypeStruct((ng, nb, MOE_GROUP, d), F32),
        compiler_params=_params("parallel", "arbitrary"),
        name="moe_ffn",
    )(tot, h.reshape(ng, nb, MOE_GROUP, d), pos.reshape(ne // per, per, T), gate.reshape(ne // per, per, T),
      w_gate, w_up, w_down, y.reshape(ng, nb, MOE_GROUP, d), mod, g.reshape(1, d), b.reshape(1, d))
    return out.reshape(bsz, L, d)


def _moe_layer(y, mod, router_w, layer, w_gate, w_up, w_down, g, b):
    bsz, L, d = y.shape
    T = bsz * L
    ne = router_w.shape[1]
    cap = EC_CAPACITY_FACTOR * T // ne
    tb = min(T, MOE_TOKEN_BLOCK)
    h, affT = _moe_prep(y, mod, router_w, tl=min(L, 512))
    gate, pos, totals = _moe_select(affT, cap, tb, T // tb)
    return _moe_ffn(y, h, pos, gate, totals, mod, layer, w_gate, w_up, w_down, g, b, tb)


def _cast_kernel(x_ref, o_ref):
    o_ref[...] = x_ref[...].astype(o_ref.dtype)


def _cast_bf16(w):
    depth, ne, r, c = w.shape
    blk = pl.BlockSpec((1, 1, r, c), lambda i, j: (i, j, 0, 0))
    return pl.pallas_call(
        _cast_kernel,
        grid=(depth, ne),
        in_specs=[blk],
        out_specs=blk,
        out_shape=jax.ShapeDtypeStruct(w.shape, BF16),
        compiler_params=_params("parallel", "parallel"),
        name="cast_bf16",
    )(w)


def kernel(x_prompt, x_sample, state_ssd, c, c_ctx, ada_w, ada_b, ln_mix_g, ln_mix_b, ln_ffn_g, ln_ffn_b, conv_pw1_w, conv_pw1_b, conv_dw_w, conv_dw_b, conv_ln_g, conv_ln_b, conv_pw2_w, conv_pw2_b, ssd_in_w, ssd_conv_w, ssd_conv_b, ssd_dt_bias, ssd_a_log, ssd_d_skip, ssd_norm_w, ssd_out_w, router_w, moe_w_gate, moe_w_up, moe_w_down):
    d = x_prompt.shape[-1]
    nb_s = x_sample.shape[0]
    cond = jnp.concatenate([c_ctx[None], c, jnp.zeros((MOD_ROWS - 1 - nb_s, d), F32)], axis=0)
    mods = _ada_mod(cond, ada_w, ada_b).reshape(DEPTH, MOD_ROWS, 6, d)
    mods = jnp.pad(mods, ((0, 0), (0, 0), (0, MOD_ROWS - 6), (0, 0)))
    w_gate, w_up, w_down = _cast_bf16(moe_w_gate), _cast_bf16(moe_w_up), _cast_bf16(moe_w_down)
    yp, ys = x_prompt, x_sample
    new_states = []
    for i in range(DEPTH):
        mod_p = mods[i, 0:1]
        mod_s = mods[i, 1:1 + nb_s]
        k = i // 2
        if i % 2 == 0:
            cw = (conv_pw1_w[k], conv_pw1_b[k], conv_dw_w[k], conv_dw_b[k], conv_ln_g[k],
                  conv_ln_b[k], conv_pw2_w[k], conv_pw2_b[k])
            yp = _conformer_layer(yp, mod_p, cw, "seq", ln_mix_g[i], ln_mix_b[i])
            ys = _conformer_layer(ys, mod_s, cw, "row" if k % 2 == 0 else "col", ln_mix_g[i],
                                  ln_mix_b[i])
        else:
            in_w = ssd_in_w[k]
            bw = SSM_INNER // 2
            in_split = (_cast_cols(in_w, 0, SSM_INNER // bw, bw),
                        _cast_cols(in_w, SSM_INNER // bw, SSM_CONV_DIM // bw, bw),
                        _dt_weight(in_w, SSM_INNER + SSM_CONV_DIM, 2 * SSM_HEADS))
            sw = (in_split, ssd_conv_w[k], ssd_conv_b[k], ssd_dt_bias[k], ssd_a_log[k],
                  ssd_d_skip[k], ssd_norm_w[k], ssd_out_w[k].astype(BF16))
            yp, fin = _ssd_layer(yp, mod_p, sw, None, ln_mix_g[i], ln_mix_b[i])
            ys, _ = _ssd_layer(ys, mod_s, sw, state_ssd[:, k], ln_mix_g[i], ln_mix_b[i])
            new_states.append(fin.astype(x_prompt.dtype))
        yp = _moe_layer(yp, mod_p, router_w[i], i, w_gate, w_up, w_down, ln_ffn_g[i], ln_ffn_b[i])
        ys = _moe_layer(ys, mod_s, router_w[i], i, w_gate, w_up, w_down, ln_ffn_g[i], ln_ffn_b[i])
    return (yp, ys, jnp.stack(new_states, axis=1))
```

```python
import functools

import jax
import jax.numpy as jnp
from jax import lax
from jax.experimental import pallas as pl
from jax.experimental.pallas import tpu as pltpu

F32 = jnp.float32
BF16 = jnp.bfloat16
HIGHEST = lax.Precision.HIGHEST

D_MODEL = 1024
DEPTH = 4
GRID_W = 64
CONV_K = 31
SSM_INNER = 2 * D_MODEL
SSM_HEAD_DIM = 64
SSM_HEADS = SSM_INNER // SSM_HEAD_DIM
SSM_GROUPS = 4
D_STATE = 128
SSM_CONV_K = 5
SSM_BC = SSM_GROUPS * D_STATE
SSM_CONV_DIM = SSM_INNER + 2 * SSM_BC
SSD_CHUNK = 128
N_EXPERTS = 16
EC_CAPACITY_FACTOR = 2
ALPHA = (2 * DEPTH) ** 0.25
LN_EPS = 1e-5

LANES = 128
SUBLANES = 8
VMEM_LIMIT = 56 * 1024 * 1024
MOD_ROWS = 8
MOD_VECS = 6
CONV_PAD = 16
MOE_TOKEN_BLOCK = 1024
MOE_ROWS = 160
MOE_GROUP = 128
MOE_EXPERTS_PER_STEP = 2


def _params(*sem):
    return pltpu.CompilerParams(dimension_semantics=sem, vmem_limit_bytes=VMEM_LIMIT)


def _layer_norm(x, g, b):
    mu = jnp.mean(x, axis=-1, keepdims=True)
    xc = x - mu
    var = jnp.mean(xc * xc, axis=-1, keepdims=True)
    return xc * lax.rsqrt(var + LN_EPS) * g + b


def _silu(x):
    return x * jax.nn.sigmoid(x)


def _softplus(x):
    return jnp.maximum(x, 0.0) + jnp.log(1.0 + jnp.exp(-jnp.abs(x)))


def _ada_kernel(c_ref, w_ref, b_ref, o_ref):
    x = _silu(c_ref[...])
    o_ref[0] = jnp.dot(x, w_ref[0], precision=HIGHEST, preferred_element_type=F32) + b_ref[0]


def _ada_mod(cond, ada_w, ada_b):
    depth, d, n = ada_w.shape
    tn = 1536
    return pl.pallas_call(
        _ada_kernel,
        grid=(depth, n // tn),
        in_specs=[
            pl.BlockSpec((MOD_ROWS, d), lambda i, j: (0, 0)),
            pl.BlockSpec((1, d, tn), lambda i, j: (i, 0, j)),
            pl.BlockSpec((1, 1, tn), lambda i, j: (i, 0, j)),
        ],
        out_specs=pl.BlockSpec((1, MOD_ROWS, tn), lambda i, j: (i, 0, j)),
        out_shape=jax.ShapeDtypeStruct((depth, MOD_ROWS, n), F32),
        compiler_params=_params("parallel", "parallel"),
        name="ada_mod",
    )(cond, ada_w, ada_b.reshape(depth, 1, n))


def _mod_index(mod):
    if mod.shape[0] == 1:
        return lambda b: 0
    return lambda b: b


def _pw1_glu_kernel(x_ref, mod_ref, w_ref, b_ref, o_ref):
    d = x_ref.shape[-1]
    x = x_ref[0]
    h = (x * (1.0 + mod_ref[0, 1:2, :]) + mod_ref[0, 0:1, :]).astype(BF16)
    u = jnp.dot(h, w_ref[...], preferred_element_type=F32) + b_ref[...]
    o_ref[0] = u[:, :d] * jax.nn.sigmoid(u[:, d:])


def _pw1_glu(x, mod, w, b, tl):
    bsz, L, d = x.shape
    mi = _mod_index(mod)
    return pl.pallas_call(
        _pw1_glu_kernel,
        grid=(bsz, L // tl),
        in_specs=[
            pl.BlockSpec((1, tl, d), lambda i, j: (i, j, 0)),
            pl.BlockSpec((1, MOD_VECS, d), lambda i, j: (mi(i), 0, 0)),
            pl.BlockSpec((d, 2 * d), lambda i, j: (0, 0)),
            pl.BlockSpec((1, 2 * d), lambda i, j: (0, 0)),
        ],
        out_specs=pl.BlockSpec((1, tl, d), lambda i, j: (i, j, 0)),
        out_shape=jax.ShapeDtypeStruct((bsz, L, d), F32),
        compiler_params=_params("parallel", "parallel"),
        name="conv_pw1_glu",
    )(x, mod, w, b)


def _conv_tail(v, y, mod_ref, lng_ref, lnb_ref, w2_ref, b2_ref, mixg_ref, mixb_ref):
    v = _silu(_layer_norm(v, lng_ref[...], lnb_ref[...]))
    m = jnp.dot(v.astype(BF16), w2_ref[...], preferred_element_type=F32) + b2_ref[...]
    return _layer_norm(ALPHA * y + mod_ref[0, 2:3, :] * m, mixg_ref[...], mixb_ref[...])


def _dwconv_seq_kernel(u_ref, y_ref, mod_ref, dww_ref, dwb_ref, lng_ref, lnb_ref, w2_ref, b2_ref,
                       mixg_ref, mixb_ref, o_ref, pad_ref, sh_ref, v_ref, *, seg, nseg):
    d = u_ref.shape[-1]
    sp = seg + 2 * CONV_PAD
    n = nseg * sp
    zeros = jnp.zeros((CONV_PAD, d), F32)
    for s in range(nseg):
        pad_ref[s * sp:s * sp + CONV_PAD, :] = zeros
        pad_ref[s * sp + CONV_PAD:s * sp + CONV_PAD + seg, :] = u_ref[0, s * seg:(s + 1) * seg, :]
        pad_ref[s * sp + CONV_PAD + seg:(s + 1) * sp, :] = zeros
    first = CONV_PAD - CONV_K // 2
    rows = min(seg, 64)
    for c in range(d // LANES):
        ln = slice(c * LANES, (c + 1) * LANES)
        for q in range(1, SUBLANES):
            sh_ref[q, 0:n - SUBLANES, :] = pad_ref[q:q + n - SUBLANES, ln]
        for s in range(nseg):
            for r in range(seg // rows):
                acc = jnp.zeros((rows, LANES), F32) + dwb_ref[:, ln]
                for k in range(CONV_K):
                    q = (first + k) % SUBLANES
                    base = s * sp + r * rows + (first + k) - q
                    tap = pad_ref[base:base + rows, ln] if q == 0 else sh_ref[q, base:base + rows, :]
                    acc = acc + dww_ref[k:k + 1, ln] * tap
                v_ref[s * seg + r * rows:s * seg + (r + 1) * rows, ln] = acc
    o_ref[0] = _conv_tail(v_ref[...], y_ref[0], mod_ref, lng_ref, lnb_ref, w2_ref, b2_ref,
                          mixg_ref, mixb_ref)


def _dwconv_col_kernel(u_ref, y_ref, mod_ref, dww_ref, dwb_ref, lng_ref, lnb_ref, w2_ref, b2_ref,
                       mixg_ref, mixb_ref, o_ref, pad_ref, v_ref):
    hh, wt, d = u_ref.shape[1:]
    zeros = jnp.zeros((CONV_PAD, wt, d), F32)
    pad_ref[0:CONV_PAD] = zeros
    pad_ref[CONV_PAD:CONV_PAD + hh] = u_ref[0]
    pad_ref[CONV_PAD + hh:CONV_PAD + hh + CONV_PAD] = zeros
    first = CONV_PAD - CONV_K // 2
    rows = 32
    for r in range(hh // rows):
        for c in range(d // LANES):
            ln = slice(c * LANES, (c + 1) * LANES)
            base = first + r * rows
            acc = jnp.zeros((rows, wt, LANES), F32) + dwb_ref[:, ln]
            for k in range(CONV_K):
                acc = acc + dww_ref[k:k + 1, ln] * pad_ref[base + k:base + k + rows, :, ln]
            v_ref[r * rows:(r + 1) * rows, :, ln] = acc
    out = _conv_tail(v_ref[...].reshape(hh * wt, d), y_ref[0].reshape(hh * wt, d), mod_ref,
                     lng_ref, lnb_ref, w2_ref, b2_ref, mixg_ref, mixb_ref)
    o_ref[0] = out.reshape(hh, wt, d)


def _conv_weights_specs(d):
    vec = pl.BlockSpec((1, d), lambda i, j: (0, 0))
    return [
        pl.BlockSpec((CONV_K, d), lambda i, j: (0, 0)),
        vec, vec, vec,
        pl.BlockSpec((d, d), lambda i, j: (0, 0)),
        vec, vec, vec,
    ]


def _dwconv_seq(u, y, mod, weights, seg, nseg):
    bsz, L, d = u.shape
    tl = seg * nseg
    mi = _mod_index(mod)
    tok = pl.BlockSpec((1, tl, d), lambda i, j: (i, j, 0))
    return pl.pallas_call(
        functools.partial(_dwconv_seq_kernel, seg=seg, nseg=nseg),
        grid=(bsz, L // tl),
        in_specs=[tok, tok, pl.BlockSpec((1, MOD_VECS, d), lambda i, j: (mi(i), 0, 0))]
        + _conv_weights_specs(d),
        out_specs=tok,
        out_shape=jax.ShapeDtypeStruct((bsz, L, d), F32),
        scratch_shapes=[pltpu.VMEM((nseg * (seg + 2 * CONV_PAD), d), F32),
                        pltpu.VMEM((SUBLANES, nseg * (seg + 2 * CONV_PAD), LANES), F32),
                        pltpu.VMEM((tl, d), F32)],
        compiler_params=_params("parallel", "parallel"),
        name="conv_dw_seq",
    )(u, y, mod, *weights)


def _dwconv_col(u, y, mod, weights):
    bsz, L, d = u.shape
    hh = L // GRID_W
    u4 = u.reshape(bsz, hh, GRID_W, d)
    y4 = y.reshape(bsz, hh, GRID_W, d)
    mi = _mod_index(mod)
    tok = pl.BlockSpec((1, hh, SUBLANES, d), lambda i, j: (i, 0, j, 0))
    out = pl.pallas_call(
        _dwconv_col_kernel,
        grid=(bsz, GRID_W // SUBLANES),
        in_specs=[tok, tok, pl.BlockSpec((1, MOD_VECS, d), lambda i, j: (mi(i), 0, 0))]
        + _conv_weights_specs(d),
        out_specs=tok,
        out_shape=jax.ShapeDtypeStruct((bsz, hh, GRID_W, d), F32),
        scratch_shapes=[pltpu.VMEM((hh + 2 * CONV_PAD, SUBLANES, d), F32),
                        pltpu.VMEM((hh, SUBLANES, d), F32)],
        compiler_params=_params("parallel", "parallel"),
        name="conv_dw_col",
    )(u4, y4, mod, *weights)
    return out.reshape(bsz, L, d)


def _conformer_layer(y, mod, cw, axis, mixg, mixb):
    pw1_w, pw1_b, dw_w, dw_b, ln_g, ln_b, pw2_w, pw2_b = cw
    bsz, L, d = y.shape
    row = lambda a: a.reshape(1, -1)
    u = _pw1_glu(y, mod, pw1_w.astype(BF16), row(pw1_b), tl=min(L, 512))
    weights = (dw_w, row(dw_b), row(ln_g), row(ln_b), pw2_w.astype(BF16), row(pw2_b),
               row(mixg), row(mixb))
    if axis == "seq":
        return _dwconv_seq(u, y, mod, weights, seg=L, nseg=1)
    if axis == "row":
        return _dwconv_seq(u, y, mod, weights, seg=GRID_W, nseg=min(8, L // GRID_W))
    return _dwconv_col(u, y, mod, weights)


def _ssd_in_kernel(x_ref, mod_ref, wz_ref, wx_ref, wdt_ref, z_ref, xbc_ref, dt_ref):
    x = x_ref[0]
    h32 = x * (1.0 + mod_ref[0, 1:2, :]) + mod_ref[0, 0:1, :]
    h = h32.astype(BF16)
    z_ref[0] = jnp.dot(h, wz_ref[...], preferred_element_type=F32).astype(z_ref.dtype)
    xbc_ref[0] = jnp.dot(h, wx_ref[...], preferred_element_type=F32).astype(xbc_ref.dtype)
    h_lo = (h32 - h.astype(F32)).astype(BF16)
    ndt = dt_ref.shape[-1]
    hi = jnp.dot(h, wdt_ref[...], preferred_element_type=F32)
    dt_ref[0] = hi[:, :ndt] + hi[:, ndt:] + jnp.dot(h_lo, wdt_ref[:, :ndt], preferred_element_type=F32)


def _ssd_in_proj(y, mod, wz, wx, wdt2, tl):
    bsz, L, d = y.shape
    mi = _mod_index(mod)
    ndt = wdt2.shape[1] // 2
    full = lambda a: pl.BlockSpec(a.shape, lambda i, j: (0, 0))
    return pl.pallas_call(
        _ssd_in_kernel,
        grid=(bsz, L // tl),
        in_specs=[
            pl.BlockSpec((1, tl, d), lambda i, j: (i, j, 0)),
            pl.BlockSpec((1, MOD_VECS, d), lambda i, j: (mi(i), 0, 0)),
            full(wz), full(wx), full(wdt2),
        ],
        out_specs=[
            pl.BlockSpec((1, tl, SSM_INNER), lambda i, j: (i, j, 0)),
            pl.BlockSpec((1, tl, SSM_CONV_DIM), lambda i, j: (i, j, 0)),
            pl.BlockSpec((1, tl, ndt), lambda i, j: (i, j, 0)),
        ],
        out_shape=[
            jax.ShapeDtypeStruct((bsz, L, SSM_INNER), BF16),
            jax.ShapeDtypeStruct((bsz, L, SSM_CONV_DIM), BF16),
            jax.ShapeDtypeStruct((bsz, L, ndt), F32),
        ],
        compiler_params=_params("parallel", "parallel"),
        name="ssd_in_proj",
    )(y, mod, wz, wx, wdt2)


def _ssd_conv_kernel(x_ref, w_ref, b_ref, o_ref, pad_ref):
    L, cb = x_ref.shape[1:]
    zeros = jnp.zeros((SUBLANES, cb), F32)
    pad_ref[0:SUBLANES, :] = zeros
    pad_ref[SUBLANES:SUBLANES + L, :] = x_ref[0].astype(F32)
    pad_ref[SUBLANES + L:2 * SUBLANES + L, :] = zeros
    first = SUBLANES - SSM_CONV_K // 2
    rows = min(L, 256)
    for r in range(L // rows):
        for c in range(cb // LANES):
            ln = slice(c * LANES, (c + 1) * LANES)
            base = first + r * rows
            acc = jnp.zeros((rows, LANES), F32) + b_ref[:, ln]
            for k in range(SSM_CONV_K):
                acc = acc + w_ref[k:k + 1, ln] * pad_ref[base + k:base + k + rows, ln]
            o_ref[0, r * rows:(r + 1) * rows, ln] = _silu(acc).astype(o_ref.dtype)


def _ssd_conv(xbc, w, b):
    bsz, L, cd = xbc.shape
    cb = max(w for w in range(LANES, cd + 1, LANES) if cd % w == 0 and (w * L <= (1 << 20) or w == LANES))
    return pl.pallas_call(
        _ssd_conv_kernel,
        grid=(bsz, cd // cb),
        in_specs=[
            pl.BlockSpec((1, L, cb), lambda i, j: (i, 0, j)),
            pl.BlockSpec((SSM_CONV_K, cb), lambda i, j: (0, j)),
            pl.BlockSpec((1, cb), lambda i, j: (0, j)),
        ],
        out_specs=pl.BlockSpec((1, L, cb), lambda i, j: (i, 0, j)),
        out_shape=jax.ShapeDtypeStruct((bsz, L, cd), BF16),
        scratch_shapes=[pltpu.VMEM((L + 2 * SUBLANES, cb), F32)],
        compiler_params=_params("parallel", "parallel"),
        name="ssd_conv",
    )(xbc, w, b.reshape(1, cd))


def _split2(x):
    hi = x.astype(BF16)
    return jnp.concatenate([hi, (x - hi.astype(F32)).astype(BF16)], axis=1)


def _split3(x):
    hi = x.astype(BF16)
    r1 = x - hi.astype(F32)
    mid = r1.astype(BF16)
    lo = (r1 - mid.astype(F32)).astype(BF16)
    return jnp.concatenate([hi, mid, lo], axis=1)


def _ssd_scan_kernel(xs_ref, b_ref, c_ref, dt_ref, bias_row_ref, alog_row_ref, tri_ref, e3_ref, *rest,
                     has_init):
    init_ref = rest[0] if has_init else None
    y_ref, fin_ref, s_ref = rest[-3:]
    d = pl.program_id(0)
    c = pl.program_id(2)
    q = SSD_CHUNK
    nh = SSM_HEADS
    gw = SSM_INNER // SSM_GROUPS
    hpl = LANES // SSM_HEAD_DIM

    @pl.when(c == 0)
    def _():
        if has_init:
            for i in range(nh // hpl):
                blk = jnp.concatenate([init_ref[0, 0, 0, hpl * i + j] for j in range(hpl)], axis=0)
                s_ref[:, i * LANES:(i + 1) * LANES] = blk.T
        else:
            s_ref[...] = jnp.zeros(s_ref.shape, F32)

    is_f = d == 0
    tri = tri_ref[0]
    mask = tri > 0.5
    dt2 = _softplus(dt_ref[0] + bias_row_ref[...])
    a2 = dt2 * -jnp.exp(alog_row_ref[...])
    dt2T = dt2.T
    c3 = jnp.dot(tri.astype(BF16), _split3(a2), preferred_element_type=F32)
    cum2 = c3[:, :LANES] + c3[:, LANES:2 * LANES] + c3[:, 2 * LANES:]
    cumT2 = cum2.T
    cum = jnp.where(is_f, cum2[:, :nh], cum2[:, nh:2 * nh])
    cumT = jnp.where(is_f, cumT2[:nh], cumT2[nh:2 * nh])
    dtT = jnp.where(is_f, dt2T[:nh], dt2T[nh:2 * nh])
    tot2 = jnp.where(is_f, cum2[q - 1:q], cum2[0:1])
    small = jnp.concatenate([dt2 * jnp.exp(jnp.minimum(tot2 - cum2, 0.0)), jnp.exp(cum2),
                             jnp.broadcast_to(jnp.exp(tot2), (SUBLANES, LANES))], axis=0)
    wide = jnp.dot(_split2(small), e3_ref[0], preferred_element_type=F32)
    w_state = wide[0:q]
    w_off = wide[q:2 * q]
    w_tot = wide[2 * q:2 * q + 1]

    xs_b = xs_ref[0]
    xs = xs_b.astype(F32)
    xdec = (xs * w_state).astype(BF16)
    lane = lax.broadcasted_iota(jnp.int32, (q, SSM_INNER), 1) % LANES
    zero = jnp.zeros((), BF16)
    x_stack = jnp.concatenate([jnp.where(lane < SSM_HEAD_DIM, xs_b, zero),
                               jnp.where(lane >= SSM_HEAD_DIM, xs_b, zero)], axis=0)
    for g in range(SSM_GROUPS):
        gs = slice(g * gw, (g + 1) * gw)
        bg = b_ref[0, :, g * D_STATE:(g + 1) * D_STATE]
        cg = c_ref[0, :, g * D_STATE:(g + 1) * D_STATE]
        cb = lax.dot_general(cg, bg, (((1,), (1,)), ((), ())), preferred_element_type=F32)
        s_g = s_ref[:, gs]
        y_off = jnp.dot(cg, s_g.astype(BF16), preferred_element_type=F32) * w_off[:, gs]
        s_ref[:, gs] = w_tot[:, gs] * s_g + jnp.dot(bg.astype(F32).T.astype(BF16), xdec[:, gs],
                                                    preferred_element_type=F32)
        pairs = gw // LANES
        for p in range(pairs):
            ls = slice(g * gw + p * LANES, g * gw + (p + 1) * LANES)
            ms = []
            for j in range(LANES // SSM_HEAD_DIM):
                h = (g * gw + p * LANES) // SSM_HEAD_DIM + j
                diff = cum[:, h:h + 1] - cumT[h:h + 1, :]
                lm = jnp.exp(jnp.where(mask, diff, -1e30))
                ms.append((cb * lm * dtT[h:h + 1, :]).astype(BF16))
            y_diag = jnp.dot(jnp.concatenate(ms, axis=1), x_stack[:, ls], preferred_element_type=F32)
            y_ref[0, 0, :, ls] = (y_diag + y_off[:, p * LANES:(p + 1) * LANES]).astype(y_ref.dtype)

    @pl.when(c == pl.num_programs(2) - 1)
    def _():
        for i in range(nh // hpl):
            blk = s_ref[:, i * LANES:(i + 1) * LANES].T
            for j in range(hpl):
                fin_ref[0, 0, hpl * i + j] = blk[j * SSM_HEAD_DIM:(j + 1) * SSM_HEAD_DIM]


def _ssd_scan(xbc, dt_raw, dt_bias, a_log, state, k):
    bsz, L, _ = xbc.shape
    cached = (1, 1, SSM_HEADS, SSM_HEAD_DIM, D_STATE)
    nc = L // SSD_CHUNK
    q = SSD_CHUNK
    ndt = LANES
    chunk = lambda d, c: c + d * (nc - 1 - 2 * c)
    r = jnp.arange(q)
    tri = jnp.stack([r[:, None] >= r[None, :], r[:, None] <= r[None, :]]).astype(F32)
    src = jnp.arange(ndt)[None, :, None] - SSM_HEADS * jnp.arange(2)[:, None, None]
    e1 = (src == (jnp.arange(SSM_INNER) // SSM_HEAD_DIM)[None, None, :]).astype(BF16)
    e3 = jnp.concatenate([e1, e1], axis=1)
    pad = lambda a: jnp.pad(a.reshape(-1), (0, ndt - a.size))
    dt_bias, a_log = pad(dt_bias), pad(a_log)
    nblk = SSM_INNER // SSM_BC
    const = lambda shape: pl.BlockSpec(shape, lambda d, b, c: tuple(0 for _ in shape))
    init_specs, init_args = [], []
    if state is not None:
        init_specs = [pl.BlockSpec((1,) + cached, lambda d, b, c: (b, k, d, 0, 0, 0))]
        init_args = [state]
    return pl.pallas_call(
        functools.partial(_ssd_scan_kernel, has_init=state is not None),
        grid=(2, bsz, nc),
        in_specs=[
            pl.BlockSpec((1, q, SSM_INNER), lambda d, b, c: (b, chunk(d, c), 0)),
            pl.BlockSpec((1, q, SSM_BC), lambda d, b, c: (b, chunk(d, c), nblk)),
            pl.BlockSpec((1, q, SSM_BC), lambda d, b, c: (b, chunk(d, c), nblk + 1)),
            pl.BlockSpec((1, q, ndt), lambda d, b, c: (b, chunk(d, c), 0)),
            const((1, ndt)), const((1, ndt)),
            pl.BlockSpec((1, q, q), lambda d, b, c: (d, 0, 0)),
            pl.BlockSpec((1, 2 * ndt, SSM_INNER), lambda d, b, c: (d, 0, 0)),
        ] + init_specs,
        out_specs=[
            pl.BlockSpec((1, 1, q, SSM_INNER), lambda d, b, c: (d, b, chunk(d, c), 0)),
            pl.BlockSpec(cached, lambda d, b, c: (b, d, 0, 0, 0)),
        ],
        out_shape=[
            jax.ShapeDtypeStruct((2, bsz, L, SSM_INNER), BF16),
            jax.ShapeDtypeStruct((bsz, 2) + cached[2:], F32),
        ],
        scratch_shapes=[pltpu.VMEM((D_STATE, SSM_INNER), F32)],
        compiler_params=_params("parallel", "parallel", "arbitrary"),
        name="ssd_scan",
    )(xbc, xbc, xbc, dt_raw, dt_bias.reshape(1, ndt), a_log.reshape(1, ndt), tri, e3, *init_args)


def _ssd_out_kernel(yf_ref, yb_ref, xs_ref, z_ref, res_ref, mod_ref, dskip_ref, nw_ref, wo_ref,
                    mixg_ref, mixb_ref, o_ref):
    y = yf_ref[0, 0].astype(F32) + yb_ref[0, 0].astype(F32) + dskip_ref[...] * xs_ref[0].astype(F32)
    y = y * _silu(z_ref[0].astype(F32))
    gw = SSM_INNER // SSM_GROUPS
    parts = []
    for g in range(SSM_GROUPS):
        yg = y[:, g * gw:(g + 1) * gw]
        ms = jnp.mean(yg * yg, axis=-1, keepdims=True)
        parts.append(yg * lax.rsqrt(ms + LN_EPS))
    yn = jnp.concatenate(parts, axis=1) * nw_ref[...]
    m = jnp.dot(yn.astype(BF16), wo_ref[...], preferred_element_type=F32)
    o_ref[0] = _layer_norm(ALPHA * res_ref[0] + mod_ref[0, 2:3, :] * m, mixg_ref[...], mixb_ref[...])


def _ssd_out(yscan, xbc, z, y, mod, d_skip, norm_w, out_w, mixg, mixb, tl):
    bsz, L, d = y.shape
    mi = _mod_index(mod)
    inner = pl.BlockSpec((1, tl, SSM_INNER), lambda i, j: (i, j, 0))
    vec = lambda n: pl.BlockSpec((1, n), lambda i, j: (0, 0))
    return pl.pallas_call(
        _ssd_out_kernel,
        grid=(bsz, L // tl),
        in_specs=[
            pl.BlockSpec((1, 1, tl, SSM_INNER), lambda i, j: (0, i, j, 0)),
            pl.BlockSpec((1, 1, tl, SSM_INNER), lambda i, j: (1, i, j, 0)),
            inner, inner,
            pl.BlockSpec((1, tl, d), lambda i, j: (i, j, 0)),
            pl.BlockSpec((1, MOD_VECS, d), lambda i, j: (mi(i), 0, 0)),
            vec(SSM_INNER), vec(SSM_INNER),
            pl.BlockSpec((SSM_INNER, d), lambda i, j: (0, 0)),
            vec(d), vec(d),
        ],
        out_specs=pl.BlockSpec((1, tl, d), lambda i, j: (i, j, 0)),
        out_shape=jax.ShapeDtypeStruct((bsz, L, d), F32),
        compiler_params=_params("parallel", "parallel"),
        name="ssd_out",
    )(yscan, yscan, xbc, z, y, mod, jnp.repeat(d_skip, SSM_HEAD_DIM).reshape(1, SSM_INNER),
      norm_w.reshape(1, SSM_INNER), out_w, mixg.reshape(1, d), mixb.reshape(1, d))


def _cast_cols_kernel(x_ref, o_ref):
    o_ref[...] = x_ref[0].astype(o_ref.dtype)


def _cast_cols(w, k, first, count, bw):
    rows = w.shape[1]
    return pl.pallas_call(
        _cast_cols_kernel,
        grid=(count,),
        in_specs=[pl.BlockSpec((1, rows, bw), lambda j: (k, 0, first + j))],
        out_specs=pl.BlockSpec((rows, bw), lambda j: (0, j)),
        out_shape=jax.ShapeDtypeStruct((rows, count * bw), BF16),
        compiler_params=_params("parallel"),
        name="cast_cols",
    )(w)


def _dt_weight_kernel(w_ref, o_ref, *, valid):
    w = w_ref[0]
    lane = lax.broadcasted_iota(jnp.int32, w.shape, 1)
    w = jnp.where(lane < valid, w, 0.0)
    o_ref[...] = _split2(w)


def _dt_weight(w, k, first_col, valid):
    rows = w.shape[1]
    assert first_col % LANES == 0 and valid <= LANES
    return pl.pallas_call(
        functools.partial(_dt_weight_kernel, valid=valid),
        grid=(1,),
        in_specs=[pl.BlockSpec((1, rows, LANES), lambda j: (k, 0, first_col // LANES))],
        out_specs=pl.BlockSpec((rows, 2 * LANES), lambda j: (0, 0)),
        out_shape=jax.ShapeDtypeStruct((rows, 2 * LANES), BF16),
        compiler_params=_params("arbitrary"),
        name="dt_weight",
    )(w)


def _ssd_layer(y, mod, sw, state, k, mixg, mixb):
    (wz, wx, wdt2), conv_w, conv_b, dt_bias, a_log, d_skip, norm_w, out_w = sw
    bsz, L, d = y.shape
    z, xbc, dt_raw = _ssd_in_proj(y, mod, wz, wx, wdt2, tl=min(L, 512))
    xbc = _ssd_conv(xbc, conv_w, conv_b)
    yscan, fin = _ssd_scan(xbc, dt_raw, dt_bias, a_log, state, k)
    out = _ssd_out(yscan, xbc, z, y, mod, d_skip, norm_w, out_w, mixg, mixb, tl=min(L, 512))
    return out, fin


def _moe_prep_kernel(y_ref, mod_ref, rwT_ref, h_ref, affT_ref):
    h32 = y_ref[0] * (1.0 + mod_ref[0, 4:5, :]) + mod_ref[0, 3:4, :]
    h = h32.astype(BF16)
    h_ref[...] = h
    ne = affT_ref.shape[0]
    nt = (((1,), (1,)), ((), ()))
    both = lax.dot_general(rwT_ref[...], h, nt, preferred_element_type=F32)
    h_lo = (h32 - h.astype(F32)).astype(BF16)
    logT = both[:ne] + both[ne:] + lax.dot_general(rwT_ref[:ne], h_lo, nt, preferred_element_type=F32)
    ex = jnp.exp(logT - jnp.max(logT, axis=0, keepdims=True))
    affT_ref[...] = ex / jnp.sum(ex, axis=0, keepdims=True)


def _moe_prep(y, mod, router_w, tl):
    bsz, L, d = y.shape
    mi = _mod_index(mod)
    nt = L // tl
    ne = router_w.shape[1]
    rwT = router_w.T
    rw_hi = rwT.astype(BF16)
    return pl.pallas_call(
        _moe_prep_kernel,
        grid=(bsz, nt),
        in_specs=[
            pl.BlockSpec((1, tl, d), lambda i, j: (i, j, 0)),
            pl.BlockSpec((1, MOD_VECS, d), lambda i, j: (mi(i), 0, 0)),
            pl.BlockSpec((2 * ne, d), lambda i, j: (0, 0)),
        ],
        out_specs=[
            pl.BlockSpec((tl, d), lambda i, j: (i * nt + j, 0)),
            pl.BlockSpec((ne, tl), lambda i, j: (0, i * nt + j)),
        ],
        out_shape=[
            jax.ShapeDtypeStruct((bsz * L, d), BF16),
            jax.ShapeDtypeStruct((ne, bsz * L), F32),
        ],
        compiler_params=_params("parallel", "parallel"),
        name="moe_prep",
    )(y, mod, jnp.concatenate([rw_hi, (rwT - rw_hi.astype(F32)).astype(BF16)], axis=0))


def _moe_select_kernel(aff_ref, upper_ref, gate_ref, *, cap, nb):
    ne, T = aff_ref.shape
    aff = aff_ref[...]
    bits = lax.bitcast_convert_type(aff, jnp.int32)

    def search(i, v):
        cand = v | lax.shift_left(jnp.int32(1), 30 - i)
        cnt = jnp.sum(jnp.where(bits >= cand, 1.0, 0.0), axis=1, keepdims=True)
        return jnp.where(cnt >= cap, cand, v)

    thr = lax.fori_loop(0, 31, search, jnp.zeros((ne, 1), jnp.int32))
    need = cap - jnp.sum(jnp.where(bits > thr, 1.0, 0.0), axis=1, keepdims=True)
    upper = upper_ref[...]
    tie_rank = jnp.zeros((ne, 1), F32)
    for j in range(T // LANES):
        ls = slice(j * LANES, (j + 1) * LANES)
        blk = bits[:, ls]
        eq = jnp.where(blk == thr, 1.0, 0.0)
        eq_incl = jnp.dot(eq.astype(BF16), upper, preferred_element_type=F32)
        sel = (blk > thr) | ((blk == thr) & (eq_incl - eq + tie_rank < need))
        tie_rank = tie_rank + eq_incl[:, LANES - 1:LANES]
        dst = ((j % nb) * (T // LANES // nb) + j // nb) * LANES
        gate_ref[:, dst:dst + LANES] = jnp.where(sel, aff[:, ls], -1.0)


def _moe_slot_kernel(gate_ref, upper_ref, pos_ref, total_ref, *, tb):
    ne, T = gate_ref.shape
    upper = upper_ref[...]
    lane = lax.broadcasted_iota(jnp.int32, (ne, LANES), 1)
    slot = jnp.zeros((ne, 1), F32)
    totals = jnp.zeros((ne, LANES), F32)
    for j in range(T // LANES):
        t0 = j * LANES
        sel = gate_ref[:, t0:t0 + LANES] >= 0.0
        m = jnp.where(sel, 1.0, 0.0)
        m_incl = jnp.dot(m.astype(BF16), upper, preferred_element_type=F32)
        pos_ref[:, t0:t0 + LANES] = jnp.where(sel, m_incl - m + slot, -1.0).astype(jnp.int32)
        slot = slot + m_incl[:, LANES - 1:LANES]
        if (t0 + LANES) % tb == 0:
            totals = jnp.where(lane == t0 // tb, slot, totals)
            slot = jnp.zeros((ne, 1), F32)
    total_ref[...] = totals.astype(jnp.int32)


def _moe_select(affT, cap, tb, nb):
    ne, T = affT.shape
    assert MOE_GROUP == LANES
    r = jnp.arange(LANES)
    upper = (r[:, None] <= r[None, :]).astype(BF16)
    full = pl.BlockSpec((ne, T), lambda i: (0, 0))
    tri = pl.BlockSpec((LANES, LANES), lambda i: (0, 0))
    small = pl.BlockSpec((ne, LANES), lambda i: (0, 0))
    gate = pl.pallas_call(
        functools.partial(_moe_select_kernel, cap=cap, nb=nb),
        grid=(1,),
        in_specs=[full, tri],
        out_specs=full,
        out_shape=jax.ShapeDtypeStruct((ne, T), F32),
        compiler_params=_params("arbitrary"),
        name="moe_select",
    )(affT, upper)
    pos, totals = pl.pallas_call(
        functools.partial(_moe_slot_kernel, tb=tb),
        grid=(1,),
        in_specs=[full, tri],
        out_specs=[full, small],
        out_shape=[jax.ShapeDtypeStruct((ne, T), jnp.int32), jax.ShapeDtypeStruct((ne, LANES), jnp.int32)],
        compiler_params=_params("arbitrary"),
        name="moe_slot",
    )(gate, upper)
    return gate, pos, totals


def _moe_ffn_kernel(tot_ref, h_ref, pos_ref, gate_ref, wg_ref, wu_ref, wd_ref, y_ref, mod_ref, lng_ref,
                    lnb_ref, o_ref, acc_ref):
    b = pl.program_id(0)
    eg = pl.program_id(1)
    rows = MOE_ROWS
    tb, d = acc_ref.shape
    per = pos_ref.shape[1]
    ne = pl.num_programs(1) * per

    @pl.when(eg == 0)
    def _():
        acc_ref[...] = jnp.zeros(acc_ref.shape, F32)

    slot_iota = lax.broadcasted_iota(jnp.int32, (rows, tb), 0)
    for j in range(per):
        def chunk(k, carry, j=j):
            hit = pos_ref[0, j:j + 1, :] == slot_iota + k * rows
            onehot = jnp.where(hit, 1.0, 0.0).astype(BF16)
            x = jnp.dot(onehot, h_ref[...].reshape(tb, d), preferred_element_type=F32).astype(BF16)
            gate = jnp.sum(jnp.where(hit, gate_ref[0, j:j + 1, :], 0.0), axis=1, keepdims=True)
            hid = _silu(jnp.dot(x, wg_ref[0, j], preferred_element_type=F32)) * jnp.dot(
                x, wu_ref[0, j], preferred_element_type=F32)
            yk = jnp.dot(hid.astype(BF16), wd_ref[0, j], preferred_element_type=F32)
            yk = (yk * gate).astype(BF16)
            acc_ref[...] += lax.dot_general(onehot, yk, (((0,), (0,)), ((), ())),
                                            preferred_element_type=F32)
            return carry

        lax.fori_loop(0, (tot_ref[b * ne + eg * per + j] + rows - 1) // rows, chunk, 0)

    @pl.when(eg == pl.num_programs(1) - 1)
    def _():
        nmod = mod_ref.shape[0]
        seg = tb // nmod
        y = y_ref[...].reshape(tb, d)
        for s in range(nmod):
            rs = slice(s * seg, (s + 1) * seg)
            out = _layer_norm(ALPHA * y[rs] + mod_ref[s, 5:6, :] * acc_ref[rs, :], lng_ref[...],
                              lnb_ref[...])
            o_ref[s * seg // MOE_GROUP:(s + 1) * seg // MOE_GROUP] = out.reshape(
                seg // MOE_GROUP, 1, MOE_GROUP, d)


def _moe_ffn(y, h, pos, gate, totals, mod, layer, w_gate, w_up, w_down, g, b, tb):
    bsz, L, d = y.shape
    T = bsz * L
    _, ne, _, ff = w_gate.shape
    per = MOE_EXPERTS_PER_STEP
    nb = T // tb
    ng = tb // MOE_GROUP
    nmod = mod.shape[0]
    assert nmod == 1 or (nmod == bsz and L % (MOE_GROUP * nb) == 0)
    tot = totals[:, :nb].T.reshape(-1)
    row = pl.BlockSpec((1, per, tb), lambda i, e, t: (e, 0, i))
    tok = pl.BlockSpec((ng, 1, MOE_GROUP, d), lambda i, e, t: (0, i, 0, 0))
    vec = pl.BlockSpec((1, d), lambda i, e, t: (0, 0))
    grid_spec = pltpu.PrefetchScalarGridSpec(
        num_scalar_prefetch=1,
        grid=(nb, ne // per),
        in_specs=[
            tok, row, row,
            pl.BlockSpec((1, per, d, ff), lambda i, e, t: (layer, e, 0, 0)),
            pl.BlockSpec((1, per, d, ff), lambda i, e, t: (layer, e, 0, 0)),
            pl.BlockSpec((1, per, ff, d), lambda i, e, t: (layer, e, 0, 0)),
            tok,
            pl.BlockSpec((nmod, MOD_VECS, d), lambda i, e, t: (0, 0, 0)),
            vec, vec,
        ],
        out_specs=tok,
        scratch_shapes=[pltpu.VMEM((tb, d), F32)],
    )
    out = pl.pallas_call(
        _moe_ffn_kernel,
        grid_spec=grid_spec,
        out_shape=jax.ShapeDtypeStruct((ng, nb, MOE_GROUP, d), F32),
        compiler_params=_params("parallel", "arbitrary"),
        name="moe_ffn",
    )(tot, h.reshape(ng, nb, MOE_GROUP, d), pos.reshape(ne // per, per, T), gate.reshape(ne // per, per, T),
      w_gate, w_up, w_down, y.reshape(ng, nb, MOE_GROUP, d), mod, g.reshape(1, d), b.reshape(1, d))
    return out.reshape(bsz, L, d)


def _moe_layer(y, mod, router_w, layer, w_gate, w_up, w_down, g, b):
    bsz, L, d = y.shape
    T = bsz * L
    ne = router_w.shape[1]
    cap = EC_CAPACITY_FACTOR * T // ne
    tb = min(T, MOE_TOKEN_BLOCK)
    h, affT = _moe_prep(y, mod, router_w, tl=min(L, 512))
    gate, pos, totals = _moe_select(affT, cap, tb, T // tb)
    return _moe_ffn(y, h, pos, gate, totals, mod, layer, w_gate, w_up, w_down, g, b, tb)


def _cast_kernel(x_ref, o_ref):
    o_ref[...] = x_ref[...].astype(o_ref.dtype)


def _cast_bf16(w):
    depth, ne, r, c = w.shape
    blk = pl.BlockSpec((1, 1, r, c), lambda i, j: (i, j, 0, 0))
    return pl.pallas_call(
        _cast_kernel,
        grid=(depth, ne),
        in_specs=[blk],
        out_specs=blk,
        out_shape=jax.ShapeDtypeStruct(w.shape, BF16),
        compiler_params=_params("parallel", "parallel"),
        name="cast_bf16",
    )(w)


def kernel(x_prompt, x_sample, state_ssd, c, c_ctx, ada_w, ada_b, ln_mix_g, ln_mix_b, ln_ffn_g, ln_ffn_b, conv_pw1_w, conv_pw1_b, conv_dw_w, conv_dw_b, conv_ln_g, conv_ln_b, conv_pw2_w, conv_pw2_b, ssd_in_w, ssd_conv_w, ssd_conv_b, ssd_dt_bias, ssd_a_log, ssd_d_skip, ssd_norm_w, ssd_out_w, router_w, moe_w_gate, moe_w_up, moe_w_down):
    d = x_prompt.shape[-1]
    nb_s = x_sample.shape[0]
    cond = jnp.concatenate([c_ctx[None], c, jnp.zeros((MOD_ROWS - 1 - nb_s, d), F32)], axis=0)
    mods = _ada_mod(cond, ada_w, ada_b).reshape(DEPTH, MOD_ROWS, MOD_VECS, d)
    w_gate, w_up, w_down = _cast_bf16(moe_w_gate), _cast_bf16(moe_w_up), _cast_bf16(moe_w_down)
    yp, ys = x_prompt, x_sample
    new_states = []
    for i in range(DEPTH):
        mod_p = mods[i, 0:1]
        mod_s = mods[i, 1:1 + nb_s]
        k = i // 2
        if i % 2 == 0:
            cw = (conv_pw1_w[k], conv_pw1_b[k], conv_dw_w[k], conv_dw_b[k], conv_ln_g[k],
                  conv_ln_b[k], conv_pw2_w[k], conv_pw2_b[k])
            yp = _conformer_layer(yp, mod_p, cw, "seq", ln_mix_g[i], ln_mix_b[i])
            ys = _conformer_layer(ys, mod_s, cw, "row" if k % 2 == 0 else "col", ln_mix_g[i],
                                  ln_mix_b[i])
        else:
            bw = SSM_INNER // 2
            in_split = (_cast_cols(ssd_in_w, k, 0, SSM_INNER // bw, bw),
                        _cast_cols(ssd_in_w, k, SSM_INNER // bw, SSM_CONV_DIM // bw, bw),
                        _dt_weight(ssd_in_w, k, SSM_INNER + SSM_CONV_DIM, 2 * SSM_HEADS))
            sw = (in_split, ssd_conv_w[k], ssd_conv_b[k], ssd_dt_bias[k], ssd_a_log[k],
                  ssd_d_skip[k], ssd_norm_w[k], ssd_out_w[k].astype(BF16))
            yp, fin = _ssd_layer(yp, mod_p, sw, None, k, ln_mix_g[i], ln_mix_b[i])
            ys, _ = _ssd_layer(ys, mod_s, sw, state_ssd, k, ln_mix_g[i], ln_mix_b[i])
            new_states.append(fin.astype(x_prompt.dtype))
        yp = _moe_layer(yp, mod_p, router_w[i], i, w_gate, w_up, w_down, ln_ffn_g[i], ln_ffn_b[i])
        ys = _moe_layer(ys, mod_s, router_w[i], i, w_gate, w_up, w_down, ln_ffn_g[i], ln_ffn_b[i])
    return (yp, ys, jnp.stack(new_states, axis=1))
```

```python
import functools

import jax
import jax.numpy as jnp
from jax import lax
from jax.experimental import pallas as pl
from jax.experimental.pallas import tpu as pltpu

F32 = jnp.float32
BF16 = jnp.bfloat16
HIGHEST = lax.Precision.HIGHEST

D_MODEL = 1024
DEPTH = 4
GRID_W = 64
CONV_K = 31
SSM_INNER = 2 * D_MODEL
SSM_HEAD_DIM = 64
SSM_HEADS = SSM_INNER // SSM_HEAD_DIM
SSM_GROUPS = 4
D_STATE = 128
SSM_CONV_K = 5
SSM_BC = SSM_GROUPS * D_STATE
SSM_CONV_DIM = SSM_INNER + 2 * SSM_BC
SSD_CHUNK = 128
N_EXPERTS = 16
EC_CAPACITY_FACTOR = 2
ALPHA = (2 * DEPTH) ** 0.25
LN_EPS = 1e-5

LANES = 128
SUBLANES = 8
VMEM_LIMIT = 56 * 1024 * 1024
MOD_ROWS = 8
MOD_VECS = 6
CONV_PAD = 16
MOE_TOKEN_BLOCK = 1024
MOE_ROWS = 144
MOE_GROUP = 128
MOE_EXPERTS_PER_STEP = 2


def _params(*sem):
    return pltpu.CompilerParams(dimension_semantics=sem, vmem_limit_bytes=VMEM_LIMIT)


def _layer_norm(x, g, b):
    mu = jnp.mean(x, axis=-1, keepdims=True)
    xc = x - mu
    var = jnp.mean(xc * xc, axis=-1, keepdims=True)
    return xc * lax.rsqrt(var + LN_EPS) * g + b


def _silu(x):
    return x * jax.nn.sigmoid(x)


def _softplus(x):
    return jnp.maximum(x, 0.0) + jnp.log(1.0 + jnp.exp(-jnp.abs(x)))


def _ada_kernel(c_ref, w_ref, b_ref, o_ref):
    x = _silu(c_ref[...])
    o_ref[0] = jnp.dot(x, w_ref[0], precision=HIGHEST, preferred_element_type=F32) + b_ref[0]


def _ada_mod(cond, ada_w, ada_b):
    depth, d, n = ada_w.shape
    tn = 1536
    return pl.pallas_call(
        _ada_kernel,
        grid=(depth, n // tn),
        in_specs=[
            pl.BlockSpec((MOD_ROWS, d), lambda i, j: (0, 0)),
            pl.BlockSpec((1, d, tn), lambda i, j: (i, 0, j)),
            pl.BlockSpec((1, 1, tn), lambda i, j: (i, 0, j)),
        ],
        out_specs=pl.BlockSpec((1, MOD_ROWS, tn), lambda i, j: (i, 0, j)),
        out_shape=jax.ShapeDtypeStruct((depth, MOD_ROWS, n), F32),
        compiler_params=_params("parallel", "parallel"),
        name="ada_mod",
    )(cond, ada_w, ada_b.reshape(depth, 1, n))


def _mod_index(mod):
    if mod.shape[0] == 1:
        return lambda b: 0
    return lambda b: b


def _pw1_glu_kernel(x_ref, mod_ref, w_ref, b_ref, o_ref):
    d = x_ref.shape[-1]
    x = x_ref[0]
    h = (x * (1.0 + mod_ref[0, 1:2, :]) + mod_ref[0, 0:1, :]).astype(BF16)
    u = jnp.dot(h, w_ref[...], preferred_element_type=F32) + b_ref[...]
    o_ref[0] = u[:, :d] * jax.nn.sigmoid(u[:, d:])


def _pw1_glu(x, mod, w, b, tl):
    bsz, L, d = x.shape
    mi = _mod_index(mod)
    return pl.pallas_call(
        _pw1_glu_kernel,
        grid=(bsz, L // tl),
        in_specs=[
            pl.BlockSpec((1, tl, d), lambda i, j: (i, j, 0)),
            pl.BlockSpec((1, MOD_VECS, d), lambda i, j: (mi(i), 0, 0)),
            pl.BlockSpec((d, 2 * d), lambda i, j: (0, 0)),
            pl.BlockSpec((1, 2 * d), lambda i, j: (0, 0)),
        ],
        out_specs=pl.BlockSpec((1, tl, d), lambda i, j: (i, j, 0)),
        out_shape=jax.ShapeDtypeStruct((bsz, L, d), F32),
        compiler_params=_params("parallel", "parallel"),
        name="conv_pw1_glu",
    )(x, mod, w, b)


def _conv_tail(v, y, mod_ref, lng_ref, lnb_ref, w2_ref, b2_ref, mixg_ref, mixb_ref):
    v = _silu(_layer_norm(v, lng_ref[...], lnb_ref[...]))
    m = jnp.dot(v.astype(BF16), w2_ref[...], preferred_element_type=F32) + b2_ref[...]
    return _layer_norm(ALPHA * y + mod_ref[0, 2:3, :] * m, mixg_ref[...], mixb_ref[...])


def _dwconv_seq_kernel(u_ref, y_ref, mod_ref, dww_ref, dwb_ref, lng_ref, lnb_ref, w2_ref, b2_ref,
                       mixg_ref, mixb_ref, o_ref, pad_ref, sh_ref, v_ref, *, seg, nseg):
    d = u_ref.shape[-1]
    sp = seg + 2 * CONV_PAD
    n = nseg * sp
    zeros = jnp.zeros((CONV_PAD, d), F32)
    for s in range(nseg):
        pad_ref[s * sp:s * sp + CONV_PAD, :] = zeros
        pad_ref[s * sp + CONV_PAD:s * sp + CONV_PAD + seg, :] = u_ref[0, s * seg:(s + 1) * seg, :]
        pad_ref[s * sp + CONV_PAD + seg:(s + 1) * sp, :] = zeros
    first = CONV_PAD - CONV_K // 2
    rows = min(seg, 64)
    for c in range(d // LANES):
        ln = slice(c * LANES, (c + 1) * LANES)
        for q in range(1, SUBLANES):
            sh_ref[q, 0:n - SUBLANES, :] = pad_ref[q:q + n - SUBLANES, ln]
        for s in range(nseg):
            for r in range(seg // rows):
                acc = jnp.zeros((rows, LANES), F32) + dwb_ref[:, ln]
                for k in range(CONV_K):
                    q = (first + k) % SUBLANES
                    base = s * sp + r * rows + (first + k) - q
                    tap = pad_ref[base:base + rows, ln] if q == 0 else sh_ref[q, base:base + rows, :]
                    acc = acc + dww_ref[k:k + 1, ln] * tap
                v_ref[s * seg + r * rows:s * seg + (r + 1) * rows, ln] = acc
    o_ref[0] = _conv_tail(v_ref[...], y_ref[0], mod_ref, lng_ref, lnb_ref, w2_ref, b2_ref,
                          mixg_ref, mixb_ref)


def _dwconv_col_kernel(u_ref, y_ref, mod_ref, dww_ref, dwb_ref, lng_ref, lnb_ref, w2_ref, b2_ref,
                       mixg_ref, mixb_ref, o_ref, pad_ref, v_ref):
    hh, wt, d = u_ref.shape[1:]
    zeros = jnp.zeros((CONV_PAD, wt, d), F32)
    pad_ref[0:CONV_PAD] = zeros
    pad_ref[CONV_PAD:CONV_PAD + hh] = u_ref[0]
    pad_ref[CONV_PAD + hh:CONV_PAD + hh + CONV_PAD] = zeros
    first = CONV_PAD - CONV_K // 2
    rows = 32
    for r in range(hh // rows):
        for c in range(d // LANES):
            ln = slice(c * LANES, (c + 1) * LANES)
            base = first + r * rows
            acc = jnp.zeros((rows, wt, LANES), F32) + dwb_ref[:, ln]
            for k in range(CONV_K):
                acc = acc + dww_ref[k:k + 1, ln] * pad_ref[base + k:base + k + rows, :, ln]
            v_ref[r * rows:(r + 1) * rows, :, ln] = acc
    out = _conv_tail(v_ref[...].reshape(hh * wt, d), y_ref[0].reshape(hh * wt, d), mod_ref,
                     lng_ref, lnb_ref, w2_ref, b2_ref, mixg_ref, mixb_ref)
    o_ref[0] = out.reshape(hh, wt, d)


def _conv_weights_specs(d):
    vec = pl.BlockSpec((1, d), lambda i, j: (0, 0))
    return [
        pl.BlockSpec((CONV_K, d), lambda i, j: (0, 0)),
        vec, vec, vec,
        pl.BlockSpec((d, d), lambda i, j: (0, 0)),
        vec, vec, vec,
    ]


def _dwconv_seq(u, y, mod, weights, seg, nseg):
    bsz, L, d = u.shape
    tl = seg * nseg
    mi = _mod_index(mod)
    tok = pl.BlockSpec((1, tl, d), lambda i, j: (i, j, 0))
    return pl.pallas_call(
        functools.partial(_dwconv_seq_kernel, seg=seg, nseg=nseg),
        grid=(bsz, L // tl),
        in_specs=[tok, tok, pl.BlockSpec((1, MOD_VECS, d), lambda i, j: (mi(i), 0, 0))]
        + _conv_weights_specs(d),
        out_specs=tok,
        out_shape=jax.ShapeDtypeStruct((bsz, L, d), F32),
        scratch_shapes=[pltpu.VMEM((nseg * (seg + 2 * CONV_PAD), d), F32),
                        pltpu.VMEM((SUBLANES, nseg * (seg + 2 * CONV_PAD), LANES), F32),
                        pltpu.VMEM((tl, d), F32)],
        compiler_params=_params("parallel", "parallel"),
        name="conv_dw_seq",
    )(u, y, mod, *weights)


def _dwconv_col(u, y, mod, weights):
    bsz, L, d = u.shape
    hh = L // GRID_W
    u4 = u.reshape(bsz, hh, GRID_W, d)
    y4 = y.reshape(bsz, hh, GRID_W, d)
    mi = _mod_index(mod)
    tok = pl.BlockSpec((1, hh, SUBLANES, d), lambda i, j: (i, 0, j, 0))
    out = pl.pallas_call(
        _dwconv_col_kernel,
        grid=(bsz, GRID_W // SUBLANES),
        in_specs=[tok, tok, pl.BlockSpec((1, MOD_VECS, d), lambda i, j: (mi(i), 0, 0))]
        + _conv_weights_specs(d),
        out_specs=tok,
        out_shape=jax.ShapeDtypeStruct((bsz, hh, GRID_W, d), F32),
        scratch_shapes=[pltpu.VMEM((hh + 2 * CONV_PAD, SUBLANES, d), F32),
                        pltpu.VMEM((hh, SUBLANES, d), F32)],
        compiler_params=_params("parallel", "parallel"),
        name="conv_dw_col",
    )(u4, y4, mod, *weights)
    return out.reshape(bsz, L, d)


def _conformer_layer(y, mod, cw, axis, mixg, mixb):
    pw1_w, pw1_b, dw_w, dw_b, ln_g, ln_b, pw2_w, pw2_b = cw
    bsz, L, d = y.shape
    row = lambda a: a.reshape(1, -1)
    u = _pw1_glu(y, mod, pw1_w.astype(BF16), row(pw1_b), tl=min(L, 512))
    weights = (dw_w, row(dw_b), row(ln_g), row(ln_b), pw2_w.astype(BF16), row(pw2_b),
               row(mixg), row(mixb))
    if axis == "seq":
        return _dwconv_seq(u, y, mod, weights, seg=L, nseg=1)
    if axis == "row":
        return _dwconv_seq(u, y, mod, weights, seg=GRID_W, nseg=min(8, L // GRID_W))
    return _dwconv_col(u, y, mod, weights)


def _ssd_in_kernel(x_ref, mod_ref, wz_ref, wx_ref, wdt_ref, z_ref, xbc_ref, dt_ref):
    x = x_ref[0]
    h32 = x * (1.0 + mod_ref[0, 1:2, :]) + mod_ref[0, 0:1, :]
    h = h32.astype(BF16)
    z_ref[0] = jnp.dot(h, wz_ref[...], preferred_element_type=F32).astype(z_ref.dtype)
    xbc_ref[0] = jnp.dot(h, wx_ref[...], preferred_element_type=F32).astype(xbc_ref.dtype)
    h_lo = (h32 - h.astype(F32)).astype(BF16)
    ndt = dt_ref.shape[-1]
    hi = jnp.dot(h, wdt_ref[...], preferred_element_type=F32)
    dt_ref[0] = hi[:, :ndt] + hi[:, ndt:] + jnp.dot(h_lo, wdt_ref[:, :ndt], preferred_element_type=F32)


def _ssd_in_proj(y, mod, wz, wx, wdt2, tl):
    bsz, L, d = y.shape
    mi = _mod_index(mod)
    ndt = wdt2.shape[1] // 2
    full = lambda a: pl.BlockSpec(a.shape, lambda i, j: (0, 0))
    return pl.pallas_call(
        _ssd_in_kernel,
        grid=(bsz, L // tl),
        in_specs=[
            pl.BlockSpec((1, tl, d), lambda i, j: (i, j, 0)),
            pl.BlockSpec((1, MOD_VECS, d), lambda i, j: (mi(i), 0, 0)),
            full(wz), full(wx), full(wdt2),
        ],
        out_specs=[
            pl.BlockSpec((1, tl, SSM_INNER), lambda i, j: (i, j, 0)),
            pl.BlockSpec((1, tl, SSM_CONV_DIM), lambda i, j: (i, j, 0)),
            pl.BlockSpec((1, tl, ndt), lambda i, j: (i, j, 0)),
        ],
        out_shape=[
            jax.ShapeDtypeStruct((bsz, L, SSM_INNER), BF16),
            jax.ShapeDtypeStruct((bsz, L, SSM_CONV_DIM), BF16),
            jax.ShapeDtypeStruct((bsz, L, ndt), F32),
        ],
        compiler_params=_params("parallel", "parallel"),
        name="ssd_in_proj",
    )(y, mod, wz, wx, wdt2)


def _ssd_conv_kernel(x_ref, w_ref, b_ref, o_ref, pad_ref):
    L, cb = x_ref.shape[1:]
    zeros = jnp.zeros((SUBLANES, cb), F32)
    pad_ref[0:SUBLANES, :] = zeros
    pad_ref[SUBLANES:SUBLANES + L, :] = x_ref[0].astype(F32)
    pad_ref[SUBLANES + L:2 * SUBLANES + L, :] = zeros
    first = SUBLANES - SSM_CONV_K // 2
    rows = min(L, 256)
    for r in range(L // rows):
        for c in range(cb // LANES):
            ln = slice(c * LANES, (c + 1) * LANES)
            base = first + r * rows
            acc = jnp.zeros((rows, LANES), F32) + b_ref[:, ln]
            for k in range(SSM_CONV_K):
                acc = acc + w_ref[k:k + 1, ln] * pad_ref[base + k:base + k + rows, ln]
            o_ref[0, r * rows:(r + 1) * rows, ln] = _silu(acc).astype(o_ref.dtype)


def _ssd_conv(xbc, w, b):
    bsz, L, cd = xbc.shape
    cb = max(w for w in range(LANES, cd + 1, LANES) if cd % w == 0 and (w * L <= (1 << 20) or w == LANES))
    return pl.pallas_call(
        _ssd_conv_kernel,
        grid=(bsz, cd // cb),
        in_specs=[
            pl.BlockSpec((1, L, cb), lambda i, j: (i, 0, j)),
            pl.BlockSpec((SSM_CONV_K, cb), lambda i, j: (0, j)),
            pl.BlockSpec((1, cb), lambda i, j: (0, j)),
        ],
        out_specs=pl.BlockSpec((1, L, cb), lambda i, j: (i, 0, j)),
        out_shape=jax.ShapeDtypeStruct((bsz, L, cd), BF16),
        scratch_shapes=[pltpu.VMEM((L + 2 * SUBLANES, cb), F32)],
        compiler_params=_params("parallel", "parallel"),
        name="ssd_conv",
    )(xbc, w, b.reshape(1, cd))


def _split2(x):
    hi = x.astype(BF16)
    return jnp.concatenate([hi, (x - hi.astype(F32)).astype(BF16)], axis=1)


def _split3(x):
    hi = x.astype(BF16)
    r1 = x - hi.astype(F32)
    mid = r1.astype(BF16)
    lo = (r1 - mid.astype(F32)).astype(BF16)
    return jnp.concatenate([hi, mid, lo], axis=1)


def _ssd_scan_kernel(xs_ref, b_ref, c_ref, dt_ref, bias_row_ref, alog_row_ref, tri_ref, e3_ref, *rest,
                     has_init):
    init_ref = rest[0] if has_init else None
    y_ref, fin_ref, s_ref = rest[-3:]
    d = pl.program_id(0)
    c = pl.program_id(2)
    q = SSD_CHUNK
    nh = SSM_HEADS
    gw = SSM_INNER // SSM_GROUPS
    hpl = LANES // SSM_HEAD_DIM

    @pl.when(c == 0)
    def _():
        if has_init:
            for i in range(nh // hpl):
                blk = jnp.concatenate([init_ref[0, 0, 0, hpl * i + j] for j in range(hpl)], axis=0)
                s_ref[:, i * LANES:(i + 1) * LANES] = blk.T
        else:
            s_ref[...] = jnp.zeros(s_ref.shape, F32)

    is_f = d == 0
    tri = tri_ref[0]
    mask = tri > 0.5
    dt2 = _softplus(dt_ref[0] + bias_row_ref[...])
    a2 = dt2 * -jnp.exp(alog_row_ref[...])
    dt2T = dt2.T
    c3 = jnp.dot(tri.astype(BF16), _split3(a2), preferred_element_type=F32)
    cum2 = c3[:, :LANES] + c3[:, LANES:2 * LANES] + c3[:, 2 * LANES:]
    cumT2 = cum2.T
    cum = jnp.where(is_f, cum2[:, :nh], cum2[:, nh:2 * nh])
    cumT = jnp.where(is_f, cumT2[:nh], cumT2[nh:2 * nh])
    dtT = jnp.where(is_f, dt2T[:nh], dt2T[nh:2 * nh])
    tot2 = jnp.where(is_f, cum2[q - 1:q], cum2[0:1])
    small = jnp.concatenate([dt2 * jnp.exp(jnp.minimum(tot2 - cum2, 0.0)), jnp.exp(cum2),
                             jnp.broadcast_to(jnp.exp(tot2), (SUBLANES, LANES))], axis=0)
    wide = jnp.dot(_split2(small), e3_ref[0], preferred_element_type=F32)
    w_state = wide[0:q]
    w_off = wide[q:2 * q]
    w_tot = wide[2 * q:2 * q + 1]

    xs_b = xs_ref[0]
    xs = xs_b.astype(F32)
    xdec = (xs * w_state).astype(BF16)
    lane = lax.broadcasted_iota(jnp.int32, (q, SSM_INNER), 1) % LANES
    zero = jnp.zeros((), BF16)
    x_stack = jnp.concatenate([jnp.where(lane < SSM_HEAD_DIM, xs_b, zero),
                               jnp.where(lane >= SSM_HEAD_DIM, xs_b, zero)], axis=0)
    for g in range(SSM_GROUPS):
        gs = slice(g * gw, (g + 1) * gw)
        bg = b_ref[0, :, g * D_STATE:(g + 1) * D_STATE]
        cg = c_ref[0, :, g * D_STATE:(g + 1) * D_STATE]
        cb = lax.dot_general(cg, bg, (((1,), (1,)), ((), ())), preferred_element_type=F32)
        s_g = s_ref[:, gs]
        y_off = jnp.dot(cg, s_g.astype(BF16), preferred_element_type=F32) * w_off[:, gs]
        s_ref[:, gs] = w_tot[:, gs] * s_g + jnp.dot(bg.astype(F32).T.astype(BF16), xdec[:, gs],
                                                    preferred_element_type=F32)
        pairs = gw // LANES
        for p in range(pairs):
            ls = slice(g * gw + p * LANES, g * gw + (p + 1) * LANES)
            ms = []
            for j in range(LANES // SSM_HEAD_DIM):
                h = (g * gw + p * LANES) // SSM_HEAD_DIM + j
                diff = cum[:, h:h + 1] - cumT[h:h + 1, :]
                lm = jnp.exp(jnp.where(mask, diff, -1e30))
                ms.append((cb * lm * dtT[h:h + 1, :]).astype(BF16))
            y_diag = jnp.dot(jnp.concatenate(ms, axis=1), x_stack[:, ls], preferred_element_type=F32)
            y_ref[0, 0, :, ls] = (y_diag + y_off[:, p * LANES:(p + 1) * LANES]).astype(y_ref.dtype)

    @pl.when(c == pl.num_programs(2) - 1)
    def _():
        for i in range(nh // hpl):
            blk = s_ref[:, i * LANES:(i + 1) * LANES].T
            for j in range(hpl):
                fin_ref[0, 0, hpl * i + j] = blk[j * SSM_HEAD_DIM:(j + 1) * SSM_HEAD_DIM]


def _ssd_scan(xbc, dt_raw, dt_bias, a_log, state, k):
    bsz, L, _ = xbc.shape
    cached = (1, 1, SSM_HEADS, SSM_HEAD_DIM, D_STATE)
    nc = L // SSD_CHUNK
    q = SSD_CHUNK
    ndt = LANES
    chunk = lambda d, c: c + d * (nc - 1 - 2 * c)
    r = jnp.arange(q)
    tri = jnp.stack([r[:, None] >= r[None, :], r[:, None] <= r[None, :]]).astype(F32)
    src = jnp.arange(ndt)[None, :, None] - SSM_HEADS * jnp.arange(2)[:, None, None]
    e1 = (src == (jnp.arange(SSM_INNER) // SSM_HEAD_DIM)[None, None, :]).astype(BF16)
    e3 = jnp.concatenate([e1, e1], axis=1)
    pad = lambda a: jnp.pad(a.reshape(-1), (0, ndt - a.size))
    dt_bias, a_log = pad(dt_bias), pad(a_log)
    nblk = SSM_INNER // SSM_BC
    const = lambda shape: pl.BlockSpec(shape, lambda d, b, c: tuple(0 for _ in shape))
    init_specs, init_args = [], []
    if state is not None:
        init_specs = [pl.BlockSpec((1,) + cached, lambda d, b, c: (b, k, d, 0, 0, 0))]
        init_args = [state]
    return pl.pallas_call(
        functools.partial(_ssd_scan_kernel, has_init=state is not None),
        grid=(2, bsz, nc),
        in_specs=[
            pl.BlockSpec((1, q, SSM_INNER), lambda d, b, c: (b, chunk(d, c), 0)),
            pl.BlockSpec((1, q, SSM_BC), lambda d, b, c: (b, chunk(d, c), nblk)),
            pl.BlockSpec((1, q, SSM_BC), lambda d, b, c: (b, chunk(d, c), nblk + 1)),
            pl.BlockSpec((1, q, ndt), lambda d, b, c: (b, chunk(d, c), 0)),
            const((1, ndt)), const((1, ndt)),
            pl.BlockSpec((1, q, q), lambda d, b, c: (d, 0, 0)),
            pl.BlockSpec((1, 2 * ndt, SSM_INNER), lambda d, b, c: (d, 0, 0)),
        ] + init_specs,
        out_specs=[
            pl.BlockSpec((1, 1, q, SSM_INNER), lambda d, b, c: (d, b, chunk(d, c), 0)),
            pl.BlockSpec(cached, lambda d, b, c: (b, d, 0, 0, 0)),
        ],
        out_shape=[
            jax.ShapeDtypeStruct((2, bsz, L, SSM_INNER), BF16),
            jax.ShapeDtypeStruct((bsz, 2) + cached[2:], F32),
        ],
        scratch_shapes=[pltpu.VMEM((D_STATE, SSM_INNER), F32)],
        compiler_params=_params("parallel", "parallel", "arbitrary"),
        name="ssd_scan",
    )(xbc, xbc, xbc, dt_raw, dt_bias.reshape(1, ndt), a_log.reshape(1, ndt), tri, e3, *init_args)


def _ssd_out_kernel(yf_ref, yb_ref, xs_ref, z_ref, res_ref, mod_ref, dskip_ref, nw_ref, wo_ref,
                    mixg_ref, mixb_ref, o_ref):
    y = yf_ref[0, 0].astype(F32) + yb_ref[0, 0].astype(F32) + dskip_ref[...] * xs_ref[0].astype(F32)
    y = y * _silu(z_ref[0].astype(F32))
    gw = SSM_INNER // SSM_GROUPS
    parts = []
    for g in range(SSM_GROUPS):
        yg = y[:, g * gw:(g + 1) * gw]
        ms = jnp.mean(yg * yg, axis=-1, keepdims=True)
        parts.append(yg * lax.rsqrt(ms + LN_EPS))
    yn = jnp.concatenate(parts, axis=1) * nw_ref[...]
    m = jnp.dot(yn.astype(BF16), wo_ref[...], preferred_element_type=F32)
    o_ref[0] = _layer_norm(ALPHA * res_ref[0] + mod_ref[0, 2:3, :] * m, mixg_ref[...], mixb_ref[...])


def _ssd_out(yscan, xbc, z, y, mod, d_skip, norm_w, out_w, mixg, mixb, tl):
    bsz, L, d = y.shape
    mi = _mod_index(mod)
    inner = pl.BlockSpec((1, tl, SSM_INNER), lambda i, j: (i, j, 0))
    vec = lambda n: pl.BlockSpec((1, n), lambda i, j: (0, 0))
    return pl.pallas_call(
        _ssd_out_kernel,
        grid=(bsz, L // tl),
        in_specs=[
            pl.BlockSpec((1, 1, tl, SSM_INNER), lambda i, j: (0, i, j, 0)),
            pl.BlockSpec((1, 1, tl, SSM_INNER), lambda i, j: (1, i, j, 0)),
            inner, inner,
            pl.BlockSpec((1, tl, d), lambda i, j: (i, j, 0)),
            pl.BlockSpec((1, MOD_VECS, d), lambda i, j: (mi(i), 0, 0)),
            vec(SSM_INNER), vec(SSM_INNER),
            pl.BlockSpec((SSM_INNER, d), lambda i, j: (0, 0)),
            vec(d), vec(d),
        ],
        out_specs=pl.BlockSpec((1, tl, d), lambda i, j: (i, j, 0)),
        out_shape=jax.ShapeDtypeStruct((bsz, L, d), F32),
        compiler_params=_params("parallel", "parallel"),
        name="ssd_out",
    )(yscan, yscan, xbc, z, y, mod, jnp.repeat(d_skip, SSM_HEAD_DIM).reshape(1, SSM_INNER),
      norm_w.reshape(1, SSM_INNER), out_w, mixg.reshape(1, d), mixb.reshape(1, d))


def _cast_cols_kernel(x_ref, o_ref):
    o_ref[...] = x_ref[0].astype(o_ref.dtype)


def _cast_cols(w, k, first, count, bw):
    rows = w.shape[1]
    return pl.pallas_call(
        _cast_cols_kernel,
        grid=(count,),
        in_specs=[pl.BlockSpec((1, rows, bw), lambda j: (k, 0, first + j))],
        out_specs=pl.BlockSpec((rows, bw), lambda j: (0, j)),
        out_shape=jax.ShapeDtypeStruct((rows, count * bw), BF16),
        compiler_params=_params("parallel"),
        name="cast_cols",
    )(w)


def _dt_weight_kernel(w_ref, o_ref, *, valid):
    w = w_ref[0]
    lane = lax.broadcasted_iota(jnp.int32, w.shape, 1)
    w = jnp.where(lane < valid, w, 0.0)
    o_ref[...] = _split2(w)


def _dt_weight(w, k, first_col, valid):
    rows = w.shape[1]
    assert first_col % LANES == 0 and valid <= LANES
    return pl.pallas_call(
        functools.partial(_dt_weight_kernel, valid=valid),
        grid=(1,),
        in_specs=[pl.BlockSpec((1, rows, LANES), lambda j: (k, 0, first_col // LANES))],
        out_specs=pl.BlockSpec((rows, 2 * LANES), lambda j: (0, 0)),
        out_shape=jax.ShapeDtypeStruct((rows, 2 * LANES), BF16),
        compiler_params=_params("arbitrary"),
        name="dt_weight",
    )(w)


def _ssd_layer(y, mod, sw, state, k, mixg, mixb):
    (wz, wx, wdt2), conv_w, conv_b, dt_bias, a_log, d_skip, norm_w, out_w = sw
    bsz, L, d = y.shape
    z, xbc, dt_raw = _ssd_in_proj(y, mod, wz, wx, wdt2, tl=min(L, 512))
    xbc = _ssd_conv(xbc, conv_w, conv_b)
    yscan, fin = _ssd_scan(xbc, dt_raw, dt_bias, a_log, state, k)
    out = _ssd_out(yscan, xbc, z, y, mod, d_skip, norm_w, out_w, mixg, mixb, tl=min(L, 512))
    return out, fin


def _moe_prep_kernel(y_ref, mod_ref, rwT_ref, h_ref, affT_ref):
    h32 = y_ref[0] * (1.0 + mod_ref[0, 4:5, :]) + mod_ref[0, 3:4, :]
    h = h32.astype(BF16)
    h_ref[...] = h
    ne = affT_ref.shape[0]
    nt = (((1,), (1,)), ((), ()))
    both = lax.dot_general(rwT_ref[...], h, nt, preferred_element_type=F32)
    h_lo = (h32 - h.astype(F32)).astype(BF16)
    logT = both[:ne] + both[ne:] + lax.dot_general(rwT_ref[:ne], h_lo, nt, preferred_element_type=F32)
    ex = jnp.exp(logT - jnp.max(logT, axis=0, keepdims=True))
    affT_ref[...] = ex / jnp.sum(ex, axis=0, keepdims=True)


def _moe_prep(y, mod, router_w, tl):
    bsz, L, d = y.shape
    mi = _mod_index(mod)
    nt = L // tl
    ne = router_w.shape[1]
    rwT = router_w.T
    rw_hi = rwT.astype(BF16)
    return pl.pallas_call(
        _moe_prep_kernel,
        grid=(bsz, nt),
        in_specs=[
            pl.BlockSpec((1, tl, d), lambda i, j: (i, j, 0)),
            pl.BlockSpec((1, MOD_VECS, d), lambda i, j: (mi(i), 0, 0)),
            pl.BlockSpec((2 * ne, d), lambda i, j: (0, 0)),
        ],
        out_specs=[
            pl.BlockSpec((tl, d), lambda i, j: (i * nt + j, 0)),
            pl.BlockSpec((ne, tl), lambda i, j: (0, i * nt + j)),
        ],
        out_shape=[
            jax.ShapeDtypeStruct((bsz * L, d), BF16),
            jax.ShapeDtypeStruct((ne, bsz * L), F32),
        ],
        compiler_params=_params("parallel", "parallel"),
        name="moe_prep",
    )(y, mod, jnp.concatenate([rw_hi, (rwT - rw_hi.astype(F32)).astype(BF16)], axis=0))


def _moe_select_kernel(aff_ref, upper_ref, gate_ref, *, cap, nb):
    ne, T = aff_ref.shape
    aff = aff_ref[...]
    bits = lax.bitcast_convert_type(aff, jnp.int32)

    def search(i, v):
        cand = v | lax.shift_left(jnp.int32(1), 30 - i)
        cnt = jnp.sum(jnp.where(bits >= cand, 1.0, 0.0), axis=1, keepdims=True)
        return jnp.where(cnt >= cap, cand, v)

    thr = lax.fori_loop(0, 31, search, jnp.zeros((ne, 1), jnp.int32))
    need = cap - jnp.sum(jnp.where(bits > thr, 1.0, 0.0), axis=1, keepdims=True)
    upper = upper_ref[...]
    tie_rank = jnp.zeros((ne, 1), F32)
    for j in range(T // LANES):
        ls = slice(j * LANES, (j + 1) * LANES)
        blk = bits[:, ls]
        eq = jnp.where(blk == thr, 1.0, 0.0)
        eq_incl = jnp.dot(eq.astype(BF16), upper, preferred_element_type=F32)
        sel = (blk > thr) | ((blk == thr) & (eq_incl - eq + tie_rank < need))
        tie_rank = tie_rank + eq_incl[:, LANES - 1:LANES]
        dst = ((j % nb) * (T // LANES // nb) + j // nb) * LANES
        gate_ref[:, dst:dst + LANES] = jnp.where(sel, aff[:, ls], -1.0)


def _moe_slot_kernel(gate_ref, upper_ref, pos_ref, total_ref, *, tb):
    ne, T = gate_ref.shape
    upper = upper_ref[...]
    lane = lax.broadcasted_iota(jnp.int32, (ne, LANES), 1)
    slot = jnp.zeros((ne, 1), F32)
    totals = jnp.zeros((ne, LANES), F32)
    for j in range(T // LANES):
        t0 = j * LANES
        sel = gate_ref[:, t0:t0 + LANES] >= 0.0
        m = jnp.where(sel, 1.0, 0.0)
        m_incl = jnp.dot(m.astype(BF16), upper, preferred_element_type=F32)
        pos_ref[:, t0:t0 + LANES] = jnp.where(sel, m_incl - m + slot, -1.0).astype(jnp.int32)
        slot = slot + m_incl[:, LANES - 1:LANES]
        if (t0 + LANES) % tb == 0:
            totals = jnp.where(lane == t0 // tb, slot, totals)
            slot = jnp.zeros((ne, 1), F32)
    total_ref[...] = totals.astype(jnp.int32)


def _moe_select(affT, cap, tb, nb):
    ne, T = affT.shape
    assert MOE_GROUP == LANES
    r = jnp.arange(LANES)
    upper = (r[:, None] <= r[None, :]).astype(BF16)
    full = pl.BlockSpec((ne, T), lambda i: (0, 0))
    tri = pl.BlockSpec((LANES, LANES), lambda i: (0, 0))
    small = pl.BlockSpec((ne, LANES), lambda i: (0, 0))
    gate = pl.pallas_call(
        functools.partial(_moe_select_kernel, cap=cap, nb=nb),
        grid=(1,),
        in_specs=[full, tri],
        out_specs=full,
        out_shape=jax.ShapeDtypeStruct((ne, T), F32),
        compiler_params=_params("arbitrary"),
        name="moe_select",
    )(affT, upper)
    pos, totals = pl.pallas_call(
        functools.partial(_moe_slot_kernel, tb=tb),
        grid=(1,),
        in_specs=[full, tri],
        out_specs=[full, small],
        out_shape=[jax.ShapeDtypeStruct((ne, T), jnp.int32), jax.ShapeDtypeStruct((ne, LANES), jnp.int32)],
        compiler_params=_params("arbitrary"),
        name="moe_slot",
    )(gate, upper)
    return gate, pos, totals


def _moe_ffn_kernel(tot_ref, h_ref, pos_ref, gate_ref, wg_ref, wu_ref, wd_ref, y_ref, mod_ref, lng_ref,
                    lnb_ref, o_ref, acc_ref):
    b = pl.program_id(0)
    eg = pl.program_id(1)
    rows = MOE_ROWS
    tb, d = acc_ref.shape
    per = pos_ref.shape[1]
    ne = pl.num_programs(1) * per

    @pl.when(eg == 0)
    def _():
        acc_ref[...] = jnp.zeros(acc_ref.shape, F32)

    slot_iota = lax.broadcasted_iota(jnp.int32, (rows, tb), 0)
    for j in range(per):
        def chunk(k, carry, j=j):
            hit = pos_ref[0, j:j + 1, :] == slot_iota + k * rows
            onehot = jnp.where(hit, 1.0, 0.0).astype(BF16)
            x = jnp.dot(onehot, h_ref[...].reshape(tb, d), preferred_element_type=F32).astype(BF16)
            gate = jnp.sum(jnp.where(hit, gate_ref[0, j:j + 1, :], 0.0), axis=1, keepdims=True)
            hid = _silu(jnp.dot(x, wg_ref[0, j], preferred_element_type=F32)) * jnp.dot(
                x, wu_ref[0, j], preferred_element_type=F32)
            yk = jnp.dot(hid.astype(BF16), wd_ref[0, j], preferred_element_type=F32)
            yk = (yk * gate).astype(BF16)
            acc_ref[...] += lax.dot_general(onehot, yk, (((0,), (0,)), ((), ())),
                                            preferred_element_type=F32)
            return carry

        lax.fori_loop(0, (tot_ref[b * ne + eg * per + j] + rows - 1) // rows, chunk, 0)

    @pl.when(eg == pl.num_programs(1) - 1)
    def _():
        nmod = mod_ref.shape[0]
        seg = tb // nmod
        y = y_ref[...].reshape(tb, d)
        for s in range(nmod):
            rs = slice(s * seg, (s + 1) * seg)
            out = _layer_norm(ALPHA * y[rs] + mod_ref[s, 5:6, :] * acc_ref[rs, :], lng_ref[...],
                              lnb_ref[...])
            o_ref[s * seg // MOE_GROUP:(s + 1) * seg // MOE_GROUP] = out.reshape(
                seg // MOE_GROUP, 1, MOE_GROUP, d)


def _moe_ffn(y, h, pos, gate, totals, mod, layer, w_gate, w_up, w_down, g, b, tb):
    bsz, L, d = y.shape
    T = bsz * L
    _, ne, _, ff = w_gate.shape
    per = MOE_EXPERTS_PER_STEP
    nb = T // tb
    ng = tb // MOE_GROUP
    nmod = mod.shape[0]
    assert nmod == 1 or (nmod == bsz and L % (MOE_GROUP * nb) == 0)
    tot = totals[:, :nb].T.reshape(-1)
    row = pl.BlockSpec((1, per, tb), lambda i, e, t: (e, 0, i))
    tok = pl.BlockSpec((ng, 1, MOE_GROUP, d), lambda i, e, t: (0, i, 0, 0))
    vec = pl.BlockSpec((1, d), lambda i, e, t: (0, 0))
    grid_spec = pltpu.PrefetchScalarGridSpec(
        num_scalar_prefetch=1,
        grid=(nb, ne // per),
        in_specs=[
            tok, row, row,
            pl.BlockSpec((1, per, d, ff), lambda i, e, t: (layer, e, 0, 0)),
            pl.BlockSpec((1, per, d, ff), lambda i, e, t: (layer, e, 0, 0)),
            pl.BlockSpec((1, per, ff, d), lambda i, e, t: (layer, e, 0, 0)),
            tok,
            pl.BlockSpec((nmod, MOD_VECS, d), lambda i, e, t: (0, 0, 0)),
            vec, vec,
        ],
        out_specs=tok,
        scratch_shapes=[pltpu.VMEM((tb, d), F32)],
    )
    out = pl.pallas_call(
        _moe_ffn_kernel,
        grid_spec=grid_spec,
        out_shape=jax.ShapeDtypeStruct((ng, nb, MOE_GROUP, d), F32),
        compiler_params=_params("parallel", "arbitrary"),
        name="moe_ffn",
    )(tot, h.reshape(ng, nb, MOE_GROUP, d), pos.reshape(ne // per, per, T), gate.reshape(ne // per, per, T),
      w_gate, w_up, w_down, y.reshape(ng, nb, MOE_GROUP, d), mod, g.reshape(1, d), b.reshape(1, d))
    return out.reshape(bsz, L, d)


def _moe_layer(y, mod, router_w, layer, w_gate, w_up, w_down, g, b):
    bsz, L, d = y.shape
    T = bsz * L
    ne = router_w.shape[1]
    cap = EC_CAPACITY_FACTOR * T // ne
    tb = min(T, MOE_TOKEN_BLOCK)
    h, affT = _moe_prep(y, mod, router_w, tl=min(L, 512))
    gate, pos, totals = _moe_select(affT, cap, tb, T // tb)
    return _moe_ffn(y, h, pos, gate, totals, mod, layer, w_gate, w_up, w_down, g, b, tb)


def _cast_kernel(x_ref, o_ref):
    o_ref[...] = x_ref[...].astype(o_ref.dtype)


def _cast_bf16(w):
    depth, ne, r, c = w.shape
    blk = pl.BlockSpec((1, 1, r, c), lambda i, j: (i, j, 0, 0))
    return pl.pallas_call(
        _cast_kernel,
        grid=(depth, ne),
        in_specs=[blk],
        out_specs=blk,
        out_shape=jax.ShapeDtypeStruct(w.shape, BF16),
        compiler_params=_params("parallel", "parallel"),
        name="cast_bf16",
    )(w)


def kernel(x_prompt, x_sample, state_ssd, c, c_ctx, ada_w, ada_b, ln_mix_g, ln_mix_b, ln_ffn_g, ln_ffn_b, conv_pw1_w, conv_pw1_b, conv_dw_w, conv_dw_b, conv_ln_g, conv_ln_b, conv_pw2_w, conv_pw2_b, ssd_in_w, ssd_conv_w, ssd_conv_b, ssd_dt_bias, ssd_a_log, ssd_d_skip, ssd_norm_w, ssd_out_w, router_w, moe_w_gate, moe_w_up, moe_w_down):
    d = x_prompt.shape[-1]
    nb_s = x_sample.shape[0]
    cond = jnp.concatenate([c_ctx[None], c, jnp.zeros((MOD_ROWS - 1 - nb_s, d), F32)], axis=0)
    mods = _ada_mod(cond, ada_w, ada_b).reshape(DEPTH, MOD_ROWS, MOD_VECS, d)
    w_gate, w_up, w_down = _cast_bf16(moe_w_gate), _cast_bf16(moe_w_up), _cast_bf16(moe_w_down)
    yp, ys = x_prompt, x_sample
    new_states = []
    for i in range(DEPTH):
        mod_p = mods[i, 0:1]
        mod_s = mods[i, 1:1 + nb_s]
        k = i // 2
        if i % 2 == 0:
            cw = (conv_pw1_w[k], conv_pw1_b[k], conv_dw_w[k], conv_dw_b[k], conv_ln_g[k],
                  conv_ln_b[k], conv_pw2_w[k], conv_pw2_b[k])
            yp = _conformer_layer(yp, mod_p, cw, "seq", ln_mix_g[i], ln_mix_b[i])
            ys = _conformer_layer(ys, mod_s, cw, "row" if k % 2 == 0 else "col", ln_mix_g[i],
                                  ln_mix_b[i])
        else:
            bw = SSM_INNER // 2
            in_split = (_cast_cols(ssd_in_w, k, 0, SSM_INNER // bw, bw),
                        _cast_cols(ssd_in_w, k, SSM_INNER // bw, SSM_CONV_DIM // bw, bw),
                        _dt_weight(ssd_in_w, k, SSM_INNER + SSM_CONV_DIM, 2 * SSM_HEADS))
            sw = (in_split, ssd_conv_w[k], ssd_conv_b[k], ssd_dt_bias[k], ssd_a_log[k],
                  ssd_d_skip[k], ssd_norm_w[k], ssd_out_w[k].astype(BF16))
            yp, fin = _ssd_layer(yp, mod_p, sw, None, k, ln_mix_g[i], ln_mix_b[i])
            ys, _ = _ssd_layer(ys, mod_s, sw, state_ssd, k, ln_mix_g[i], ln_mix_b[i])
            new_states.append(fin.astype(x_prompt.dtype))
        yp = _moe_layer(yp, mod_p, router_w[i], i, w_gate, w_up, w_down, ln_ffn_g[i], ln_ffn_b[i])
        ys = _moe_layer(ys, mod_s, router_w[i], i, w_gate, w_up, w_down, ln_ffn_g[i], ln_ffn_b[i])
    return (yp, ys, jnp.stack(new_states, axis=1))
```

```python
import functools

import jax
import jax.numpy as jnp
from jax import lax
from jax.experimental import pallas as pl
from jax.experimental.pallas import tpu as pltpu

F32 = jnp.float32
BF16 = jnp.bfloat16
HIGHEST = lax.Precision.HIGHEST

D_MODEL = 1024
DEPTH = 4
GRID_W = 64
CONV_K = 31
SSM_INNER = 2 * D_MODEL
SSM_HEAD_DIM = 64
SSM_HEADS = SSM_INNER // SSM_HEAD_DIM
SSM_GROUPS = 4
D_STATE = 128
SSM_CONV_K = 5
SSM_BC = SSM_GROUPS * D_STATE
SSM_CONV_DIM = SSM_INNER + 2 * SSM_BC
SSD_CHUNK = 128
N_EXPERTS = 16
EC_CAPACITY_FACTOR = 2
ALPHA = (2 * DEPTH) ** 0.25
LN_EPS = 1e-5

LANES = 128
SUBLANES = 8
VMEM_LIMIT = 56 * 1024 * 1024
MOD_ROWS = 8
MOD_VECS = 6
CONV_PAD = 16
MOE_TOKEN_BLOCK = 1024
MOE_ROWS = 160
MOE_GROUP = 128
MOE_EXPERTS_PER_STEP = 2


def _params(*sem):
    return pltpu.CompilerParams(dimension_semantics=sem, vmem_limit_bytes=VMEM_LIMIT)


def _layer_norm(x, g, b):
    mu = jnp.mean(x, axis=-1, keepdims=True)
    xc = x - mu
    var = jnp.mean(xc * xc, axis=-1, keepdims=True)
    return xc * lax.rsqrt(var + LN_EPS) * g + b


def _silu(x):
    return x * jax.nn.sigmoid(x)


def _softplus(x):
    return jnp.maximum(x, 0.0) + jnp.log(1.0 + jnp.exp(-jnp.abs(x)))


def _ada_kernel(c_ref, w_ref, b_ref, o_ref):
    x = _silu(c_ref[...])
    o_ref[0] = jnp.dot(x, w_ref[0], precision=HIGHEST, preferred_element_type=F32) + b_ref[0]


def _ada_mod(cond, ada_w, ada_b):
    depth, d, n = ada_w.shape
    tn = 1536
    return pl.pallas_call(
        _ada_kernel,
        grid=(depth, n // tn),
        in_specs=[
            pl.BlockSpec((MOD_ROWS, d), lambda i, j: (0, 0)),
            pl.BlockSpec((1, d, tn), lambda i, j: (i, 0, j)),
            pl.BlockSpec((1, 1, tn), lambda i, j: (i, 0, j)),
        ],
        out_specs=pl.BlockSpec((1, MOD_ROWS, tn), lambda i, j: (i, 0, j)),
        out_shape=jax.ShapeDtypeStruct((depth, MOD_ROWS, n), F32),
        compiler_params=_params("parallel", "parallel"),
        name="ada_mod",
    )(cond, ada_w, ada_b.reshape(depth, 1, n))


def _mod_index(mod):
    if mod.shape[0] == 1:
        return lambda b: 0
    return lambda b: b


def _pw1_glu_kernel(x_ref, mod_ref, w_ref, b_ref, o_ref):
    d = x_ref.shape[-1]
    x = x_ref[0]
    h = (x * (1.0 + mod_ref[0, 1:2, :]) + mod_ref[0, 0:1, :]).astype(BF16)
    u = jnp.dot(h, w_ref[...], preferred_element_type=F32) + b_ref[...]
    o_ref[0] = u[:, :d] * jax.nn.sigmoid(u[:, d:])


def _pw1_glu(x, mod, w, b, tl):
    bsz, L, d = x.shape
    mi = _mod_index(mod)
    return pl.pallas_call(
        _pw1_glu_kernel,
        grid=(bsz, L // tl),
        in_specs=[
            pl.BlockSpec((1, tl, d), lambda i, j: (i, j, 0)),
            pl.BlockSpec((1, MOD_VECS, d), lambda i, j: (mi(i), 0, 0)),
            pl.BlockSpec((d, 2 * d), lambda i, j: (0, 0)),
            pl.BlockSpec((1, 2 * d), lambda i, j: (0, 0)),
        ],
        out_specs=pl.BlockSpec((1, tl, d), lambda i, j: (i, j, 0)),
        out_shape=jax.ShapeDtypeStruct((bsz, L, d), F32),
        compiler_params=_params("parallel", "parallel"),
        name="conv_pw1_glu",
    )(x, mod, w, b)


def _conv_tail(v, y, mod_ref, lng_ref, lnb_ref, w2_ref, b2_ref, mixg_ref, mixb_ref):
    v = _silu(_layer_norm(v, lng_ref[...], lnb_ref[...]))
    m = jnp.dot(v.astype(BF16), w2_ref[...], preferred_element_type=F32) + b2_ref[...]
    return _layer_norm(ALPHA * y + mod_ref[0, 2:3, :] * m, mixg_ref[...], mixb_ref[...])


def _dwconv_seq_kernel(u_ref, y_ref, mod_ref, dww_ref, dwb_ref, lng_ref, lnb_ref, w2_ref, b2_ref,
                       mixg_ref, mixb_ref, o_ref, pad_ref, sh_ref, v_ref, *, seg, nseg):
    d = u_ref.shape[-1]
    sp = seg + 2 * CONV_PAD
    n = nseg * sp
    zeros = jnp.zeros((CONV_PAD, d), F32)
    for s in range(nseg):
        pad_ref[s * sp:s * sp + CONV_PAD, :] = zeros
        pad_ref[s * sp + CONV_PAD:s * sp + CONV_PAD + seg, :] = u_ref[0, s * seg:(s + 1) * seg, :]
        pad_ref[s * sp + CONV_PAD + seg:(s + 1) * sp, :] = zeros
    first = CONV_PAD - CONV_K // 2
    rows = min(seg, 64)
    for c in range(d // LANES):
        ln = slice(c * LANES, (c + 1) * LANES)
        for q in range(1, SUBLANES):
            sh_ref[q, 0:n - SUBLANES, :] = pad_ref[q:q + n - SUBLANES, ln]
        for s in range(nseg):
            for r in range(seg // rows):
                acc = jnp.zeros((rows, LANES), F32) + dwb_ref[:, ln]
                for k in range(CONV_K):
                    q = (first + k) % SUBLANES
                    base = s * sp + r * rows + (first + k) - q
                    tap = pad_ref[base:base + rows, ln] if q == 0 else sh_ref[q, base:base + rows, :]
                    acc = acc + dww_ref[k:k + 1, ln] * tap
                v_ref[s * seg + r * rows:s * seg + (r + 1) * rows, ln] = acc
    o_ref[0] = _conv_tail(v_ref[...], y_ref[0], mod_ref, lng_ref, lnb_ref, w2_ref, b2_ref,
                          mixg_ref, mixb_ref)


def _dwconv_col_kernel(u_ref, y_ref, mod_ref, dww_ref, dwb_ref, lng_ref, lnb_ref, w2_ref, b2_ref,
                       mixg_ref, mixb_ref, o_ref, pad_ref, v_ref):
    hh, wt, d = u_ref.shape[1:]
    zeros = jnp.zeros((CONV_PAD, wt, d), F32)
    pad_ref[0:CONV_PAD] = zeros
    pad_ref[CONV_PAD:CONV_PAD + hh] = u_ref[0]
    pad_ref[CONV_PAD + hh:CONV_PAD + hh + CONV_PAD] = zeros
    first = CONV_PAD - CONV_K // 2
    rows = 32
    for r in range(hh // rows):
        for c in range(d // LANES):
            ln = slice(c * LANES, (c + 1) * LANES)
            base = first + r * rows
            acc = jnp.zeros((rows, wt, LANES), F32) + dwb_ref[:, ln]
            for k in range(CONV_K):
                acc = acc + dww_ref[k:k + 1, ln] * pad_ref[base + k:base + k + rows, :, ln]
            v_ref[r * rows:(r + 1) * rows, :, ln] = acc
    out = _conv_tail(v_ref[...].reshape(hh * wt, d), y_ref[0].reshape(hh * wt, d), mod_ref,
                     lng_ref, lnb_ref, w2_ref, b2_ref, mixg_ref, mixb_ref)
    o_ref[0] = out.reshape(hh, wt, d)


def _conv_weights_specs(d):
    vec = pl.BlockSpec((1, d), lambda i, j: (0, 0))
    return [
        pl.BlockSpec((CONV_K, d), lambda i, j: (0, 0)),
        vec, vec, vec,
        pl.BlockSpec((d, d), lambda i, j: (0, 0)),
        vec, vec, vec,
    ]


def _dwconv_seq(u, y, mod, weights, seg, nseg):
    bsz, L, d = u.shape
    tl = seg * nseg
    mi = _mod_index(mod)
    tok = pl.BlockSpec((1, tl, d), lambda i, j: (i, j, 0))
    return pl.pallas_call(
        functools.partial(_dwconv_seq_kernel, seg=seg, nseg=nseg),
        grid=(bsz, L // tl),
        in_specs=[tok, tok, pl.BlockSpec((1, MOD_VECS, d), lambda i, j: (mi(i), 0, 0))]
        + _conv_weights_specs(d),
        out_specs=tok,
        out_shape=jax.ShapeDtypeStruct((bsz, L, d), F32),
        scratch_shapes=[pltpu.VMEM((nseg * (seg + 2 * CONV_PAD), d), F32),
                        pltpu.VMEM((SUBLANES, nseg * (seg + 2 * CONV_PAD), LANES), F32),
                        pltpu.VMEM((tl, d), F32)],
        compiler_params=_params("parallel", "parallel"),
        name="conv_dw_seq",
    )(u, y, mod, *weights)


def _dwconv_col(u, y, mod, weights):
    bsz, L, d = u.shape
    hh = L // GRID_W
    u4 = u.reshape(bsz, hh, GRID_W, d)
    y4 = y.reshape(bsz, hh, GRID_W, d)
    mi = _mod_index(mod)
    tok = pl.BlockSpec((1, hh, SUBLANES, d), lambda i, j: (i, 0, j, 0))
    out = pl.pallas_call(
        _dwconv_col_kernel,
        grid=(bsz, GRID_W // SUBLANES),
        in_specs=[tok, tok, pl.BlockSpec((1, MOD_VECS, d), lambda i, j: (mi(i), 0, 0))]
        + _conv_weights_specs(d),
        out_specs=tok,
        out_shape=jax.ShapeDtypeStruct((bsz, hh, GRID_W, d), F32),
        scratch_shapes=[pltpu.VMEM((hh + 2 * CONV_PAD, SUBLANES, d), F32),
                        pltpu.VMEM((hh, SUBLANES, d), F32)],
        compiler_params=_params("parallel", "parallel"),
        name="conv_dw_col",
    )(u4, y4, mod, *weights)
    return out.reshape(bsz, L, d)


def _conformer_layer(y, mod, cw, axis, mixg, mixb):
    pw1_w, pw1_b, dw_w, dw_b, ln_g, ln_b, pw2_w, pw2_b = cw
    bsz, L, d = y.shape
    row = lambda a: a.reshape(1, -1)
    u = _pw1_glu(y, mod, pw1_w.astype(BF16), row(pw1_b), tl=min(L, 1024))
    weights = (dw_w, row(dw_b), row(ln_g), row(ln_b), pw2_w.astype(BF16), row(pw2_b),
               row(mixg), row(mixb))
    if axis == "seq":
        return _dwconv_seq(u, y, mod, weights, seg=L, nseg=1)
    if axis == "row":
        return _dwconv_seq(u, y, mod, weights, seg=GRID_W, nseg=min(8, L // GRID_W))
    return _dwconv_col(u, y, mod, weights)


def _ssd_in_kernel(x_ref, mod_ref, wz_ref, wx_ref, wdt_ref, z_ref, xbc_ref, dt_ref):
    x = x_ref[0]
    h32 = x * (1.0 + mod_ref[0, 1:2, :]) + mod_ref[0, 0:1, :]
    h = h32.astype(BF16)
    z_ref[0] = jnp.dot(h, wz_ref[...], preferred_element_type=F32).astype(z_ref.dtype)
    xbc_ref[0] = jnp.dot(h, wx_ref[...], preferred_element_type=F32).astype(xbc_ref.dtype)
    h_lo = (h32 - h.astype(F32)).astype(BF16)
    ndt = dt_ref.shape[-1]
    hi = jnp.dot(h, wdt_ref[...], preferred_element_type=F32)
    dt_ref[0] = hi[:, :ndt] + hi[:, ndt:] + jnp.dot(h_lo, wdt_ref[:, :ndt], preferred_element_type=F32)


def _ssd_in_proj(y, mod, wz, wx, wdt2, tl):
    bsz, L, d = y.shape
    mi = _mod_index(mod)
    ndt = wdt2.shape[1] // 2
    full = lambda a: pl.BlockSpec(a.shape, lambda i, j: (0, 0))
    return pl.pallas_call(
        _ssd_in_kernel,
        grid=(bsz, L // tl),
        in_specs=[
            pl.BlockSpec((1, tl, d), lambda i, j: (i, j, 0)),
            pl.BlockSpec((1, MOD_VECS, d), lambda i, j: (mi(i), 0, 0)),
            full(wz), full(wx), full(wdt2),
        ],
        out_specs=[
            pl.BlockSpec((1, tl, SSM_INNER), lambda i, j: (i, j, 0)),
            pl.BlockSpec((1, tl, SSM_CONV_DIM), lambda i, j: (i, j, 0)),
            pl.BlockSpec((1, tl, ndt), lambda i, j: (i, j, 0)),
        ],
        out_shape=[
            jax.ShapeDtypeStruct((bsz, L, SSM_INNER), BF16),
            jax.ShapeDtypeStruct((bsz, L, SSM_CONV_DIM), BF16),
            jax.ShapeDtypeStruct((bsz, L, ndt), F32),
        ],
        compiler_params=_params("parallel", "parallel"),
        name="ssd_in_proj",
    )(y, mod, wz, wx, wdt2)


def _ssd_conv_kernel(x_ref, w_ref, b_ref, o_ref, pad_ref):
    L, cb = x_ref.shape[1:]
    zeros = jnp.zeros((SUBLANES, cb), F32)
    pad_ref[0:SUBLANES, :] = zeros
    pad_ref[SUBLANES:SUBLANES + L, :] = x_ref[0].astype(F32)
    pad_ref[SUBLANES + L:2 * SUBLANES + L, :] = zeros
    first = SUBLANES - SSM_CONV_K // 2
    rows = min(L, 256)
    for r in range(L // rows):
        for c in range(cb // LANES):
            ln = slice(c * LANES, (c + 1) * LANES)
            base = first + r * rows
            acc = jnp.zeros((rows, LANES), F32) + b_ref[:, ln]
            for k in range(SSM_CONV_K):
                acc = acc + w_ref[k:k + 1, ln] * pad_ref[base + k:base + k + rows, ln]
            o_ref[0, r * rows:(r + 1) * rows, ln] = _silu(acc).astype(o_ref.dtype)


def _ssd_conv(xbc, w, b):
    bsz, L, cd = xbc.shape
    cb = max(w for w in range(LANES, cd + 1, LANES) if cd % w == 0 and (w * L <= (1 << 20) or w == LANES))
    return pl.pallas_call(
        _ssd_conv_kernel,
        grid=(bsz, cd // cb),
        in_specs=[
            pl.BlockSpec((1, L, cb), lambda i, j: (i, 0, j)),
            pl.BlockSpec((SSM_CONV_K, cb), lambda i, j: (0, j)),
            pl.BlockSpec((1, cb), lambda i, j: (0, j)),
        ],
        out_specs=pl.BlockSpec((1, L, cb), lambda i, j: (i, 0, j)),
        out_shape=jax.ShapeDtypeStruct((bsz, L, cd), BF16),
        scratch_shapes=[pltpu.VMEM((L + 2 * SUBLANES, cb), F32)],
        compiler_params=_params("parallel", "parallel"),
        name="ssd_conv",
    )(xbc, w, b.reshape(1, cd))


def _split2(x):
    hi = x.astype(BF16)
    return jnp.concatenate([hi, (x - hi.astype(F32)).astype(BF16)], axis=1)


def _split3(x):
    hi = x.astype(BF16)
    r1 = x - hi.astype(F32)
    mid = r1.astype(BF16)
    lo = (r1 - mid.astype(F32)).astype(BF16)
    return jnp.concatenate([hi, mid, lo], axis=1)


def _ssd_scan_kernel(xs_ref, b_ref, c_ref, dt_ref, bias_row_ref, alog_row_ref, tri_ref, e3_ref, *rest,
                     has_init):
    init_ref = rest[0] if has_init else None
    y_ref, fin_ref, s_ref = rest[-3:]
    d = pl.program_id(0)
    c = pl.program_id(2)
    q = SSD_CHUNK
    nh = SSM_HEADS
    gw = SSM_INNER // SSM_GROUPS
    hpl = LANES // SSM_HEAD_DIM

    @pl.when(c == 0)
    def _():
        if has_init:
            for i in range(nh // hpl):
                blk = jnp.concatenate([init_ref[0, 0, 0, hpl * i + j] for j in range(hpl)], axis=0)
                s_ref[:, i * LANES:(i + 1) * LANES] = blk.T
        else:
            s_ref[...] = jnp.zeros(s_ref.shape, F32)

    is_f = d == 0
    tri = tri_ref[0]
    mask = tri > 0.5
    dt2 = _softplus(dt_ref[0] + bias_row_ref[...])
    a2 = dt2 * -jnp.exp(alog_row_ref[...])
    dt2T = dt2.T
    c3 = jnp.dot(tri.astype(BF16), _split3(a2), preferred_element_type=F32)
    cum2 = c3[:, :LANES] + c3[:, LANES:2 * LANES] + c3[:, 2 * LANES:]
    cumT2 = cum2.T
    cum = jnp.where(is_f, cum2[:, :nh], cum2[:, nh:2 * nh])
    cumT = jnp.where(is_f, cumT2[:nh], cumT2[nh:2 * nh])
    dtT = jnp.where(is_f, dt2T[:nh], dt2T[nh:2 * nh])
    tot2 = jnp.where(is_f, cum2[q - 1:q], cum2[0:1])
    small = jnp.concatenate([dt2 * jnp.exp(jnp.minimum(tot2 - cum2, 0.0)), jnp.exp(cum2),
                             jnp.broadcast_to(jnp.exp(tot2), (SUBLANES, LANES))], axis=0)
    wide = jnp.dot(_split2(small), e3_ref[0], preferred_element_type=F32)
    w_state = wide[0:q]
    w_off = wide[q:2 * q]
    w_tot = wide[2 * q:2 * q + 1]

    xs_b = xs_ref[0]
    xs = xs_b.astype(F32)
    xdec = (xs * w_state).astype(BF16)
    lane = lax.broadcasted_iota(jnp.int32, (q, SSM_INNER), 1) % LANES
    zero = jnp.zeros((), BF16)
    x_stack = jnp.concatenate([jnp.where(lane < SSM_HEAD_DIM, xs_b, zero),
                               jnp.where(lane >= SSM_HEAD_DIM, xs_b, zero)], axis=0)
    for g in range(SSM_GROUPS):
        gs = slice(g * gw, (g + 1) * gw)
        bg = b_ref[0, :, g * D_STATE:(g + 1) * D_STATE]
        cg = c_ref[0, :, g * D_STATE:(g + 1) * D_STATE]
        cb = lax.dot_general(cg, bg, (((1,), (1,)), ((), ())), preferred_element_type=F32)
        s_g = s_ref[:, gs]
        y_off = jnp.dot(cg, s_g.astype(BF16), preferred_element_type=F32) * w_off[:, gs]
        s_ref[:, gs] = w_tot[:, gs] * s_g + jnp.dot(bg.astype(F32).T.astype(BF16), xdec[:, gs],
                                                    preferred_element_type=F32)
        pairs = gw // LANES
        for p in range(pairs):
            ls = slice(g * gw + p * LANES, g * gw + (p + 1) * LANES)
            ms = []
            for j in range(LANES // SSM_HEAD_DIM):
                h = (g * gw + p * LANES) // SSM_HEAD_DIM + j
                diff = cum[:, h:h + 1] - cumT[h:h + 1, :]
                lm = jnp.exp(jnp.where(mask, diff, -1e30))
                ms.append((cb * lm * dtT[h:h + 1, :]).astype(BF16))
            y_diag = jnp.dot(jnp.concatenate(ms, axis=1), x_stack[:, ls], preferred_element_type=F32)
            y_ref[0, 0, :, ls] = (y_diag + y_off[:, p * LANES:(p + 1) * LANES]).astype(y_ref.dtype)

    @pl.when(c == pl.num_programs(2) - 1)
    def _():
        for i in range(nh // hpl):
            blk = s_ref[:, i * LANES:(i + 1) * LANES].T
            for j in range(hpl):
                fin_ref[0, 0, hpl * i + j] = blk[j * SSM_HEAD_DIM:(j + 1) * SSM_HEAD_DIM]


def _ssd_scan(xbc, dt_raw, dt_bias, a_log, state, k):
    bsz, L, _ = xbc.shape
    cached = (1, 1, SSM_HEADS, SSM_HEAD_DIM, D_STATE)
    nc = L // SSD_CHUNK
    q = SSD_CHUNK
    ndt = LANES
    chunk = lambda d, c: c + d * (nc - 1 - 2 * c)
    r = jnp.arange(q)
    tri = jnp.stack([r[:, None] >= r[None, :], r[:, None] <= r[None, :]]).astype(F32)
    src = jnp.arange(ndt)[None, :, None] - SSM_HEADS * jnp.arange(2)[:, None, None]
    e1 = (src == (jnp.arange(SSM_INNER) // SSM_HEAD_DIM)[None, None, :]).astype(BF16)
    e3 = jnp.concatenate([e1, e1], axis=1)
    pad = lambda a: jnp.pad(a.reshape(-1), (0, ndt - a.size))
    dt_bias, a_log = pad(dt_bias), pad(a_log)
    nblk = SSM_INNER // SSM_BC
    const = lambda shape: pl.BlockSpec(shape, lambda d, b, c: tuple(0 for _ in shape))
    init_specs, init_args = [], []
    if state is not None:
        init_specs = [pl.BlockSpec((1,) + cached, lambda d, b, c: (b, k, d, 0, 0, 0))]
        init_args = [state]
    return pl.pallas_call(
        functools.partial(_ssd_scan_kernel, has_init=state is not None),
        grid=(2, bsz, nc),
        in_specs=[
            pl.BlockSpec((1, q, SSM_INNER), lambda d, b, c: (b, chunk(d, c), 0)),
            pl.BlockSpec((1, q, SSM_BC), lambda d, b, c: (b, chunk(d, c), nblk)),
            pl.BlockSpec((1, q, SSM_BC), lambda d, b, c: (b, chunk(d, c), nblk + 1)),
            pl.BlockSpec((1, q, ndt), lambda d, b, c: (b, chunk(d, c), 0)),
            const((1, ndt)), const((1, ndt)),
            pl.BlockSpec((1, q, q), lambda d, b, c: (d, 0, 0)),
            pl.BlockSpec((1, 2 * ndt, SSM_INNER), lambda d, b, c: (d, 0, 0)),
        ] + init_specs,
        out_specs=[
            pl.BlockSpec((1, 1, q, SSM_INNER), lambda d, b, c: (d, b, chunk(d, c), 0)),
            pl.BlockSpec(cached, lambda d, b, c: (b, d, 0, 0, 0)),
        ],
        out_shape=[
            jax.ShapeDtypeStruct((2, bsz, L, SSM_INNER), BF16),
            jax.ShapeDtypeStruct((bsz, 2) + cached[2:], F32),
        ],
        scratch_shapes=[pltpu.VMEM((D_STATE, SSM_INNER), F32)],
        compiler_params=_params("parallel", "parallel", "arbitrary"),
        name="ssd_scan",
    )(xbc, xbc, xbc, dt_raw, dt_bias.reshape(1, ndt), a_log.reshape(1, ndt), tri, e3, *init_args)


def _ssd_out_kernel(yf_ref, yb_ref, xs_ref, z_ref, res_ref, mod_ref, dskip_ref, nw_ref, wo_ref,
                    mixg_ref, mixb_ref, o_ref):
    y = yf_ref[0, 0].astype(F32) + yb_ref[0, 0].astype(F32) + dskip_ref[...] * xs_ref[0].astype(F32)
    y = y * _silu(z_ref[0].astype(F32))
    gw = SSM_INNER // SSM_GROUPS
    parts = []
    for g in range(SSM_GROUPS):
        yg = y[:, g * gw:(g + 1) * gw]
        ms = jnp.mean(yg * yg, axis=-1, keepdims=True)
        parts.append(yg * lax.rsqrt(ms + LN_EPS))
    yn = jnp.concatenate(parts, axis=1) * nw_ref[...]
    m = jnp.dot(yn.astype(BF16), wo_ref[...], preferred_element_type=F32)
    o_ref[0] = _layer_norm(ALPHA * res_ref[0] + mod_ref[0, 2:3, :] * m, mixg_ref[...], mixb_ref[...])


def _ssd_out(yscan, xbc, z, y, mod, d_skip, norm_w, out_w, mixg, mixb, tl):
    bsz, L, d = y.shape
    mi = _mod_index(mod)
    inner = pl.BlockSpec((1, tl, SSM_INNER), lambda i, j: (i, j, 0))
    vec = lambda n: pl.BlockSpec((1, n), lambda i, j: (0, 0))
    return pl.pallas_call(
        _ssd_out_kernel,
        grid=(bsz, L // tl),
        in_specs=[
            pl.BlockSpec((1, 1, tl, SSM_INNER), lambda i, j: (0, i, j, 0)),
            pl.BlockSpec((1, 1, tl, SSM_INNER), lambda i, j: (1, i, j, 0)),
            inner, inner,
            pl.BlockSpec((1, tl, d), lambda i, j: (i, j, 0)),
            pl.BlockSpec((1, MOD_VECS, d), lambda i, j: (mi(i), 0, 0)),
            vec(SSM_INNER), vec(SSM_INNER),
            pl.BlockSpec((SSM_INNER, d), lambda i, j: (0, 0)),
            vec(d), vec(d),
        ],
        out_specs=pl.BlockSpec((1, tl, d), lambda i, j: (i, j, 0)),
        out_shape=jax.ShapeDtypeStruct((bsz, L, d), F32),
        compiler_params=_params("parallel", "parallel"),
        name="ssd_out",
    )(yscan, yscan, xbc, z, y, mod, jnp.repeat(d_skip, SSM_HEAD_DIM).reshape(1, SSM_INNER),
      norm_w.reshape(1, SSM_INNER), out_w, mixg.reshape(1, d), mixb.reshape(1, d))


def _cast_cols_kernel(x_ref, o_ref):
    o_ref[...] = x_ref[0].astype(o_ref.dtype)


def _cast_cols(w, k, first, count, bw):
    rows = w.shape[1]
    return pl.pallas_call(
        _cast_cols_kernel,
        grid=(count,),
        in_specs=[pl.BlockSpec((1, rows, bw), lambda j: (k, 0, first + j))],
        out_specs=pl.BlockSpec((rows, bw), lambda j: (0, j)),
        out_shape=jax.ShapeDtypeStruct((rows, count * bw), BF16),
        compiler_params=_params("parallel"),
        name="cast_cols",
    )(w)


def _dt_weight_kernel(w_ref, o_ref, *, valid):
    w = w_ref[0]
    lane = lax.broadcasted_iota(jnp.int32, w.shape, 1)
    w = jnp.where(lane < valid, w, 0.0)
    o_ref[...] = _split2(w)


def _dt_weight(w, k, first_col, valid):
    rows = w.shape[1]
    assert first_col % LANES == 0 and valid <= LANES
    return pl.pallas_call(
        functools.partial(_dt_weight_kernel, valid=valid),
        grid=(1,),
        in_specs=[pl.BlockSpec((1, rows, LANES), lambda j: (k, 0, first_col // LANES))],
        out_specs=pl.BlockSpec((rows, 2 * LANES), lambda j: (0, 0)),
        out_shape=jax.ShapeDtypeStruct((rows, 2 * LANES), BF16),
        compiler_params=_params("arbitrary"),
        name="dt_weight",
    )(w)


def _ssd_layer(y, mod, sw, state, k, mixg, mixb):
    (wz, wx, wdt2), conv_w, conv_b, dt_bias, a_log, d_skip, norm_w, out_w = sw
    bsz, L, d = y.shape
    z, xbc, dt_raw = _ssd_in_proj(y, mod, wz, wx, wdt2, tl=min(L, 512))
    xbc = _ssd_conv(xbc, conv_w, conv_b)
    yscan, fin = _ssd_scan(xbc, dt_raw, dt_bias, a_log, state, k)
    out = _ssd_out(yscan, xbc, z, y, mod, d_skip, norm_w, out_w, mixg, mixb, tl=min(L, 512))
    return out, fin


def _moe_prep_kernel(y_ref, mod_ref, rwT_ref, h_ref, affT_ref):
    h32 = y_ref[0] * (1.0 + mod_ref[0, 4:5, :]) + mod_ref[0, 3:4, :]
    h = h32.astype(BF16)
    h_ref[...] = h
    ne = affT_ref.shape[0]
    nt = (((1,), (1,)), ((), ()))
    both = lax.dot_general(rwT_ref[...], h, nt, preferred_element_type=F32)
    h_lo = (h32 - h.astype(F32)).astype(BF16)
    logT = both[:ne] + both[ne:] + lax.dot_general(rwT_ref[:ne], h_lo, nt, preferred_element_type=F32)
    ex = jnp.exp(logT - jnp.max(logT, axis=0, keepdims=True))
    affT_ref[...] = ex / jnp.sum(ex, axis=0, keepdims=True)


def _moe_prep(y, mod, router_w, tl):
    bsz, L, d = y.shape
    mi = _mod_index(mod)
    nt = L // tl
    ne = router_w.shape[1]
    rwT = router_w.T
    rw_hi = rwT.astype(BF16)
    return pl.pallas_call(
        _moe_prep_kernel,
        grid=(bsz, nt),
        in_specs=[
            pl.BlockSpec((1, tl, d), lambda i, j: (i, j, 0)),
            pl.BlockSpec((1, MOD_VECS, d), lambda i, j: (mi(i), 0, 0)),
            pl.BlockSpec((2 * ne, d), lambda i, j: (0, 0)),
        ],
        out_specs=[
            pl.BlockSpec((tl, d), lambda i, j: (i * nt + j, 0)),
            pl.BlockSpec((ne, tl), lambda i, j: (0, i * nt + j)),
        ],
        out_shape=[
            jax.ShapeDtypeStruct((bsz * L, d), BF16),
            jax.ShapeDtypeStruct((ne, bsz * L), F32),
        ],
        compiler_params=_params("parallel", "parallel"),
        name="moe_prep",
    )(y, mod, jnp.concatenate([rw_hi, (rwT - rw_hi.astype(F32)).astype(BF16)], axis=0))


def _moe_select_kernel(aff_ref, upper_ref, gate_ref, *, cap, nb):
    ne, T = aff_ref.shape
    aff = aff_ref[...]
    bits = lax.bitcast_convert_type(aff, jnp.int32)

    def search(i, v):
        cand = v | lax.shift_left(jnp.int32(1), 30 - i)
        cnt = jnp.sum(jnp.where(bits >= cand, 1.0, 0.0), axis=1, keepdims=True)
        return jnp.where(cnt >= cap, cand, v)

    thr = lax.fori_loop(0, 31, search, jnp.zeros((ne, 1), jnp.int32))
    need = cap - jnp.sum(jnp.where(bits > thr, 1.0, 0.0), axis=1, keepdims=True)
    upper = upper_ref[...]
    tie_rank = jnp.zeros((ne, 1), F32)
    for j in range(T // LANES):
        ls = slice(j * LANES, (j + 1) * LANES)
        blk = bits[:, ls]
        eq = jnp.where(blk == thr, 1.0, 0.0)
        eq_incl = jnp.dot(eq.astype(BF16), upper, preferred_element_type=F32)
        sel = (blk > thr) | ((blk == thr) & (eq_incl - eq + tie_rank < need))
        tie_rank = tie_rank + eq_incl[:, LANES - 1:LANES]
        dst = ((j % nb) * (T // LANES // nb) + j // nb) * LANES
        gate_ref[:, dst:dst + LANES] = jnp.where(sel, aff[:, ls], -1.0)


def _moe_slot_kernel(gate_ref, upper_ref, pos_ref, total_ref, *, tb):
    ne, T = gate_ref.shape
    upper = upper_ref[...]
    lane = lax.broadcasted_iota(jnp.int32, (ne, LANES), 1)
    slot = jnp.zeros((ne, 1), F32)
    totals = jnp.zeros((ne, LANES), F32)
    for j in range(T // LANES):
        t0 = j * LANES
        sel = gate_ref[:, t0:t0 + LANES] >= 0.0
        m = jnp.where(sel, 1.0, 0.0)
        m_incl = jnp.dot(m.astype(BF16), upper, preferred_element_type=F32)
        pos_ref[:, t0:t0 + LANES] = jnp.where(sel, m_incl - m + slot, -1.0).astype(jnp.int32)
        slot = slot + m_incl[:, LANES - 1:LANES]
        if (t0 + LANES) % tb == 0:
            totals = jnp.where(lane == t0 // tb, slot, totals)
            slot = jnp.zeros((ne, 1), F32)
    total_ref[...] = totals.astype(jnp.int32)


def _moe_select(affT, cap, tb, nb):
    ne, T = affT.shape
    assert MOE_GROUP == LANES
    r = jnp.arange(LANES)
    upper = (r[:, None] <= r[None, :]).astype(BF16)
    full = pl.BlockSpec((ne, T), lambda i: (0, 0))
    tri = pl.BlockSpec((LANES, LANES), lambda i: (0, 0))
    small = pl.BlockSpec((ne, LANES), lambda i: (0, 0))
    gate = pl.pallas_call(
        functools.partial(_moe_select_kernel, cap=cap, nb=nb),
        grid=(1,),
        in_specs=[full, tri],
        out_specs=full,
        out_shape=jax.ShapeDtypeStruct((ne, T), F32),
        compiler_params=_params("arbitrary"),
        name="moe_select",
    )(affT, upper)
    pos, totals = pl.pallas_call(
        functools.partial(_moe_slot_kernel, tb=tb),
        grid=(1,),
        in_specs=[full, tri],
        out_specs=[full, small],
        out_shape=[jax.ShapeDtypeStruct((ne, T), jnp.int32), jax.ShapeDtypeStruct((ne, LANES), jnp.int32)],
        compiler_params=_params("arbitrary"),
        name="moe_slot",
    )(gate, upper)
    return gate, pos, totals


def _moe_ffn_kernel(tot_ref, h_ref, pos_ref, gate_ref, wg_ref, wu_ref, wd_ref, y_ref, mod_ref, lng_ref,
                    lnb_ref, o_ref, acc_ref):
    b = pl.program_id(0)
    eg = pl.program_id(1)
    rows = MOE_ROWS
    tb, d = acc_ref.shape
    per = pos_ref.shape[1]
    ne = pl.num_programs(1) * per

    @pl.when(eg == 0)
    def _():
        acc_ref[...] = jnp.zeros(acc_ref.shape, F32)

    slot_iota = lax.broadcasted_iota(jnp.int32, (rows, tb), 0)
    for j in range(per):
        def chunk(k, carry, j=j):
            hit = pos_ref[0, j:j + 1, :] == slot_iota + k * rows
            onehot = jnp.where(hit, 1.0, 0.0).astype(BF16)
            x = jnp.dot(onehot, h_ref[...].reshape(tb, d), preferred_element_type=F32).astype(BF16)
            gate = jnp.sum(jnp.where(hit, gate_ref[0, j:j + 1, :], 0.0), axis=1, keepdims=True)
            hid = _silu(jnp.dot(x, wg_ref[0, j], preferred_element_type=F32)) * jnp.dot(
                x, wu_ref[0, j], preferred_element_type=F32)
            yk = jnp.dot(hid.astype(BF16), wd_ref[0, j], preferred_element_type=F32)
            yk = (yk * gate).astype(BF16)
            acc_ref[...] += lax.dot_general(onehot, yk, (((0,), (0,)), ((), ())),
                                            preferred_element_type=F32)
            return carry

        lax.fori_loop(0, (tot_ref[b * ne + eg * per + j] + rows - 1) // rows, chunk, 0)

    @pl.when(eg == pl.num_programs(1) - 1)
    def _():
        nmod = mod_ref.shape[0]
        seg = tb // nmod
        y = y_ref[...].reshape(tb, d)
        for s in range(nmod):
            rs = slice(s * seg, (s + 1) * seg)
            out = _layer_norm(ALPHA * y[rs] + mod_ref[s, 5:6, :] * acc_ref[rs, :], lng_ref[...],
                              lnb_ref[...])
            o_ref[s * seg // MOE_GROUP:(s + 1) * seg // MOE_GROUP] = out.reshape(
                seg // MOE_GROUP, 1, MOE_GROUP, d)


def _moe_ffn(y, h, pos, gate, totals, mod, layer, w_gate, w_up, w_down, g, b, tb):
    bsz, L, d = y.shape
    T = bsz * L
    _, ne, _, ff = w_gate.shape
    per = MOE_EXPERTS_PER_STEP
    nb = T // tb
    ng = tb // MOE_GROUP
    nmod = mod.shape[0]
    assert nmod == 1 or (nmod == bsz and L % (MOE_GROUP * nb) == 0)
    tot = totals[:, :nb].T.reshape(-1)
    row = pl.BlockSpec((1, per, tb), lambda i, e, t: (e, 0, i))
    tok = pl.BlockSpec((ng, 1, MOE_GROUP, d), lambda i, e, t: (0, i, 0, 0))
    vec = pl.BlockSpec((1, d), lambda i, e, t: (0, 0))
    grid_spec = pltpu.PrefetchScalarGridSpec(
        num_scalar_prefetch=1,
        grid=(nb, ne // per),
        in_specs=[
            tok, row, row,
            pl.BlockSpec((1, per, d, ff), lambda i, e, t: (layer, e, 0, 0)),
            pl.BlockSpec((1, per, d, ff), lambda i, e, t: (layer, e, 0, 0)),
            pl.BlockSpec((1, per, ff, d), lambda i, e, t: (layer, e, 0, 0)),
            tok,
            pl.BlockSpec((nmod, MOD_VECS, d), lambda i, e, t: (0, 0, 0)),
            vec, vec,
        ],
        out_specs=tok,
        scratch_shapes=[pltpu.VMEM((tb, d), F32)],
    )
    out = pl.pallas_call(
        _moe_ffn_kernel,
        grid_spec=grid_spec,
        out_shape=jax.ShapeDtypeStruct((ng, nb, MOE_GROUP, d), F32),
        compiler_params=_params("parallel", "arbitrary"),
        name="moe_ffn",
    )(tot, h.reshape(ng, nb, MOE_GROUP, d), pos.reshape(ne // per, per, T), gate.reshape(ne // per, per, T),
      w_gate, w_up, w_down, y.reshape(ng, nb, MOE_GROUP, d), mod, g.reshape(1, d), b.reshape(1, d))
    return out.reshape(bsz, L, d)


def _moe_layer(y, mod, router_w, layer, w_gate, w_up, w_down, g, b):
    bsz, L, d = y.shape
    T = bsz * L
    ne = router_w.shape[1]
    cap = EC_CAPACITY_FACTOR * T // ne
    tb = min(T, MOE_TOKEN_BLOCK)
    h, affT = _moe_prep(y, mod, router_w, tl=min(L, 1024))
    gate, pos, totals = _moe_select(affT, cap, tb, T // tb)
    return _moe_ffn(y, h, pos, gate, totals, mod, layer, w_gate, w_up, w_down, g, b, tb)


def _cast_kernel(x_ref, o_ref):
    o_ref[...] = x_ref[...].astype(o_ref.dtype)


def _cast_bf16(w):
    depth, ne, r, c = w.shape
    blk = pl.BlockSpec((1, 2, r, c), lambda i, j: (i, j, 0, 0))
    return pl.pallas_call(
        _cast_kernel,
        grid=(depth, ne // 2),
        in_specs=[blk],
        out_specs=blk,
        out_shape=jax.ShapeDtypeStruct(w.shape, BF16),
        compiler_params=_params("parallel", "parallel"),
        name="cast_bf16",
    )(w)


def kernel(x_prompt, x_sample, state_ssd, c, c_ctx, ada_w, ada_b, ln_mix_g, ln_mix_b, ln_ffn_g, ln_ffn_b, conv_pw1_w, conv_pw1_b, conv_dw_w, conv_dw_b, conv_ln_g, conv_ln_b, conv_pw2_w, conv_pw2_b, ssd_in_w, ssd_conv_w, ssd_conv_b, ssd_dt_bias, ssd_a_log, ssd_d_skip, ssd_norm_w, ssd_out_w, router_w, moe_w_gate, moe_w_up, moe_w_down):
    d = x_prompt.shape[-1]
    nb_s = x_sample.shape[0]
    cond = jnp.concatenate([c_ctx[None], c, jnp.zeros((MOD_ROWS - 1 - nb_s, d), F32)], axis=0)
    mods = _ada_mod(cond, ada_w, ada_b).reshape(DEPTH, MOD_ROWS, MOD_VECS, d)
    w_gate, w_up, w_down = _cast_bf16(moe_w_gate), _cast_bf16(moe_w_up), _cast_bf16(moe_w_down)
    yp, ys = x_prompt, x_sample
    new_states = []
    for i in range(DEPTH):
        mod_p = mods[i, 0:1]
        mod_s = mods[i, 1:1 + nb_s]
        k = i // 2
        if i % 2 == 0:
            cw = (conv_pw1_w[k], conv_pw1_b[k], conv_dw_w[k], conv_dw_b[k], conv_ln_g[k],
                  conv_ln_b[k], conv_pw2_w[k], conv_pw2_b[k])
            yp = _conformer_layer(yp, mod_p, cw, "seq", ln_mix_g[i], ln_mix_b[i])
            ys = _conformer_layer(ys, mod_s, cw, "row" if k % 2 == 0 else "col", ln_mix_g[i],
                                  ln_mix_b[i])
        else:
            bw = SSM_INNER // 2
            in_split = (_cast_cols(ssd_in_w, k, 0, SSM_INNER // bw, bw),
                        _cast_cols(ssd_in_w, k, SSM_INNER // bw, SSM_CONV_DIM // bw, bw),
                        _dt_weight(ssd_in_w, k, SSM_INNER + SSM_CONV_DIM, 2 * SSM_HEADS))
            sw = (in_split, ssd_conv_w[k], ssd_conv_b[k], ssd_dt_bias[k], ssd_a_log[k],
                  ssd_d_skip[k], ssd_norm_w[k], ssd_out_w[k].astype(BF16))
            yp, fin = _ssd_layer(yp, mod_p, sw, None, k, ln_mix_g[i], ln_mix_b[i])
            ys, _ = _ssd_layer(ys, mod_s, sw, state_ssd, k, ln_mix_g[i], ln_mix_b[i])
            new_states.append(fin.astype(x_prompt.dtype))
        yp = _moe_layer(yp, mod_p, router_w[i], i, w_gate, w_up, w_down, ln_ffn_g[i], ln_ffn_b[i])
        ys = _moe_layer(ys, mod_s, router_w[i], i, w_gate, w_up, w_down, ln_ffn_g[i], ln_ffn_b[i])
    return (yp, ys, jnp.stack(new_states, axis=1))
```

```python
import functools

import jax
import jax.numpy as jnp
from jax import lax
from jax.experimental import pallas as pl
from jax.experimental.pallas import tpu as pltpu

F32 = jnp.float32
BF16 = jnp.bfloat16
HIGHEST = lax.Precision.HIGHEST

D_MODEL = 1024
DEPTH = 4
GRID_W = 64
CONV_K = 31
SSM_INNER = 2 * D_MODEL
SSM_HEAD_DIM = 64
SSM_HEADS = SSM_INNER // SSM_HEAD_DIM
SSM_GROUPS = 4
D_STATE = 128
SSM_CONV_K = 5
SSM_BC = SSM_GROUPS * D_STATE
SSM_CONV_DIM = SSM_INNER + 2 * SSM_BC
SSD_CHUNK = 128
N_EXPERTS = 16
EC_CAPACITY_FACTOR = 2
ALPHA = (2 * DEPTH) ** 0.25
LN_EPS = 1e-5

LANES = 128
SUBLANES = 8
VMEM_LIMIT = 56 * 1024 * 1024
MOD_ROWS = 8
MOD_VECS = 6
CONV_PAD = 16
MOE_TOKEN_BLOCK = 1024
MOE_ROWS = 160
MOE_GROUP = 128
MOE_EXPERTS_PER_STEP = 2


def _params(*sem):
    return pltpu.CompilerParams(dimension_semantics=sem, vmem_limit_bytes=VMEM_LIMIT)


def _layer_norm(x, g, b):
    mu = jnp.mean(x, axis=-1, keepdims=True)
    xc = x - mu
    var = jnp.mean(xc * xc, axis=-1, keepdims=True)
    return xc * lax.rsqrt(var + LN_EPS) * g + b


def _silu(x):
    return x * jax.nn.sigmoid(x)


def _softplus(x):
    return jnp.maximum(x, 0.0) + jnp.log(1.0 + jnp.exp(-jnp.abs(x)))


def _ada_kernel(c_ref, w_ref, b_ref, o_ref):
    x = _silu(c_ref[...])
    o_ref[0] = jnp.dot(x, w_ref[0], precision=HIGHEST, preferred_element_type=F32) + b_ref[0]


def _ada_mod(cond, ada_w, ada_b):
    depth, d, n = ada_w.shape
    tn = 1536
    return pl.pallas_call(
        _ada_kernel,
        grid=(depth, n // tn),
        in_specs=[
            pl.BlockSpec((MOD_ROWS, d), lambda i, j: (0, 0)),
            pl.BlockSpec((1, d, tn), lambda i, j: (i, 0, j)),
            pl.BlockSpec((1, 1, tn), lambda i, j: (i, 0, j)),
        ],
        out_specs=pl.BlockSpec((1, MOD_ROWS, tn), lambda i, j: (i, 0, j)),
        out_shape=jax.ShapeDtypeStruct((depth, MOD_ROWS, n), F32),
        compiler_params=_params("parallel", "parallel"),
        name="ada_mod",
    )(cond, ada_w, ada_b.reshape(depth, 1, n))


def _mod_index(mod):
    if mod.shape[0] == 1:
        return lambda b: 0
    return lambda b: b


def _pw1_glu_kernel(x_ref, mod_ref, w_ref, b_ref, o_ref):
    d = x_ref.shape[-1]
    x = x_ref[0]
    h = (x * (1.0 + mod_ref[0, 1:2, :]) + mod_ref[0, 0:1, :]).astype(BF16)
    u = jnp.dot(h, w_ref[...], preferred_element_type=F32) + b_ref[...]
    o_ref[0] = u[:, :d] * jax.nn.sigmoid(u[:, d:])


def _pw1_glu(x, mod, w, b, tl):
    bsz, L, d = x.shape
    mi = _mod_index(mod)
    return pl.pallas_call(
        _pw1_glu_kernel,
        grid=(bsz, L // tl),
        in_specs=[
            pl.BlockSpec((1, tl, d), lambda i, j: (i, j, 0)),
            pl.BlockSpec((1, MOD_VECS, d), lambda i, j: (mi(i), 0, 0)),
            pl.BlockSpec((d, 2 * d), lambda i, j: (0, 0)),
            pl.BlockSpec((1, 2 * d), lambda i, j: (0, 0)),
        ],
        out_specs=pl.BlockSpec((1, tl, d), lambda i, j: (i, j, 0)),
        out_shape=jax.ShapeDtypeStruct((bsz, L, d), F32),
        compiler_params=_params("parallel", "parallel"),
        name="conv_pw1_glu",
    )(x, mod, w, b)


def _conv_tail(v, y, mod_ref, lng_ref, lnb_ref, w2_ref, b2_ref, mixg_ref, mixb_ref):
    v = _silu(_layer_norm(v, lng_ref[...], lnb_ref[...]))
    m = jnp.dot(v.astype(BF16), w2_ref[...], preferred_element_type=F32) + b2_ref[...]
    return _layer_norm(ALPHA * y + mod_ref[0, 2:3, :] * m, mixg_ref[...], mixb_ref[...])


def _dwconv_seq_kernel(u_ref, y_ref, mod_ref, dww_ref, dwb_ref, lng_ref, lnb_ref, w2_ref, b2_ref,
                       mixg_ref, mixb_ref, o_ref, pad_ref, sh_ref, v_ref, *, seg, nseg):
    d = u_ref.shape[-1]
    sp = seg + 2 * CONV_PAD
    n = nseg * sp
    zeros = jnp.zeros((CONV_PAD, d), F32)
    for s in range(nseg):
        pad_ref[s * sp:s * sp + CONV_PAD, :] = zeros
        pad_ref[s * sp + CONV_PAD:s * sp + CONV_PAD + seg, :] = u_ref[0, s * seg:(s + 1) * seg, :]
        pad_ref[s * sp + CONV_PAD + seg:(s + 1) * sp, :] = zeros
    first = CONV_PAD - CONV_K // 2
    rows = min(seg, 64)
    for c in range(d // LANES):
        ln = slice(c * LANES, (c + 1) * LANES)
        for q in range(1, SUBLANES):
            sh_ref[q, 0:n - SUBLANES, :] = pad_ref[q:q + n - SUBLANES, ln]
        for s in range(nseg):
            for r in range(seg // rows):
                acc = jnp.zeros((rows, LANES), F32) + dwb_ref[:, ln]
                for k in range(CONV_K):
                    q = (first + k) % SUBLANES
                    base = s * sp + r * rows + (first + k) - q
                    tap = pad_ref[base:base + rows, ln] if q == 0 else sh_ref[q, base:base + rows, :]
                    acc = acc + dww_ref[k:k + 1, ln] * tap
                v_ref[s * seg + r * rows:s * seg + (r + 1) * rows, ln] = acc
    o_ref[0] = _conv_tail(v_ref[...], y_ref[0], mod_ref, lng_ref, lnb_ref, w2_ref, b2_ref,
                          mixg_ref, mixb_ref)


def _dwconv_col_kernel(u_ref, y_ref, mod_ref, dww_ref, dwb_ref, lng_ref, lnb_ref, w2_ref, b2_ref,
                       mixg_ref, mixb_ref, o_ref, pad_ref, v_ref):
    hh, wt, d = u_ref.shape[1:]
    zeros = jnp.zeros((CONV_PAD, wt, d), F32)
    pad_ref[0:CONV_PAD] = zeros
    pad_ref[CONV_PAD:CONV_PAD + hh] = u_ref[0]
    pad_ref[CONV_PAD + hh:CONV_PAD + hh + CONV_PAD] = zeros
    first = CONV_PAD - CONV_K // 2
    rows = 32
    for r in range(hh // rows):
        for c in range(d // LANES):
            ln = slice(c * LANES, (c + 1) * LANES)
            base = first + r * rows
            acc = jnp.zeros((rows, wt, LANES), F32) + dwb_ref[:, ln]
            for k in range(CONV_K):
                acc = acc + dww_ref[k:k + 1, ln] * pad_ref[base + k:base + k + rows, :, ln]
            v_ref[r * rows:(r + 1) * rows, :, ln] = acc
    out = _conv_tail(v_ref[...].reshape(hh * wt, d), y_ref[0].reshape(hh * wt, d), mod_ref,
                     lng_ref, lnb_ref, w2_ref, b2_ref, mixg_ref, mixb_ref)
    o_ref[0] = out.reshape(hh, wt, d)


def _conv_weights_specs(d):
    vec = pl.BlockSpec((1, d), lambda i, j: (0, 0))
    return [
        pl.BlockSpec((CONV_K, d), lambda i, j: (0, 0)),
        vec, vec, vec,
        pl.BlockSpec((d, d), lambda i, j: (0, 0)),
        vec, vec, vec,
    ]


def _dwconv_seq(u, y, mod, weights, seg, nseg):
    bsz, L, d = u.shape
    tl = seg * nseg
    mi = _mod_index(mod)
    tok = pl.BlockSpec((1, tl, d), lambda i, j: (i, j, 0))
    return pl.pallas_call(
        functools.partial(_dwconv_seq_kernel, seg=seg, nseg=nseg),
        grid=(bsz, L // tl),
        in_specs=[tok, tok, pl.BlockSpec((1, MOD_VECS, d), lambda i, j: (mi(i), 0, 0))]
        + _conv_weights_specs(d),
        out_specs=tok,
        out_shape=jax.ShapeDtypeStruct((bsz, L, d), F32),
        scratch_shapes=[pltpu.VMEM((nseg * (seg + 2 * CONV_PAD), d), F32),
                        pltpu.VMEM((SUBLANES, nseg * (seg + 2 * CONV_PAD), LANES), F32),
                        pltpu.VMEM((tl, d), F32)],
        compiler_params=_params("parallel", "parallel"),
        name="conv_dw_seq",
    )(u, y, mod, *weights)


def _dwconv_col(u, y, mod, weights):
    bsz, L, d = u.shape
    hh = L // GRID_W
    u4 = u.reshape(bsz, hh, GRID_W, d)
    y4 = y.reshape(bsz, hh, GRID_W, d)
    mi = _mod_index(mod)
    tok = pl.BlockSpec((1, hh, SUBLANES, d), lambda i, j: (i, 0, j, 0))
    out = pl.pallas_call(
        _dwconv_col_kernel,
        grid=(bsz, GRID_W // SUBLANES),
        in_specs=[tok, tok, pl.BlockSpec((1, MOD_VECS, d), lambda i, j: (mi(i), 0, 0))]
        + _conv_weights_specs(d),
        out_specs=tok,
        out_shape=jax.ShapeDtypeStruct((bsz, hh, GRID_W, d), F32),
        scratch_shapes=[pltpu.VMEM((hh + 2 * CONV_PAD, SUBLANES, d), F32),
                        pltpu.VMEM((hh, SUBLANES, d), F32)],
        compiler_params=_params("parallel", "parallel"),
        name="conv_dw_col",
    )(u4, y4, mod, *weights)
    return out.reshape(bsz, L, d)


def _conformer_layer(y, mod, cw, axis, mixg, mixb):
    pw1_w, pw1_b, dw_w, dw_b, ln_g, ln_b, pw2_w, pw2_b = cw
    bsz, L, d = y.shape
    row = lambda a: a.reshape(1, -1)
    u = _pw1_glu(y, mod, pw1_w.astype(BF16), row(pw1_b), tl=min(L, 1024))
    weights = (dw_w, row(dw_b), row(ln_g), row(ln_b), pw2_w.astype(BF16), row(pw2_b),
               row(mixg), row(mixb))
    if axis == "seq":
        return _dwconv_seq(u, y, mod, weights, seg=L, nseg=1)
    if axis == "row":
        return _dwconv_seq(u, y, mod, weights, seg=GRID_W, nseg=min(8, L // GRID_W))
    return _dwconv_col(u, y, mod, weights)


def _ssd_in_kernel(x_ref, mod_ref, wz_ref, wx_ref, wdt_ref, z_ref, xbc_ref, dt_ref):
    x = x_ref[0]
    h32 = x * (1.0 + mod_ref[0, 1:2, :]) + mod_ref[0, 0:1, :]
    h = h32.astype(BF16)
    z_ref[0] = jnp.dot(h, wz_ref[...], preferred_element_type=F32).astype(z_ref.dtype)
    xbc_ref[0] = jnp.dot(h, wx_ref[...], preferred_element_type=F32).astype(xbc_ref.dtype)
    h_lo = (h32 - h.astype(F32)).astype(BF16)
    ndt = dt_ref.shape[-1]
    hi = jnp.dot(h, wdt_ref[...], preferred_element_type=F32)
    dt_ref[0] = hi[:, :ndt] + hi[:, ndt:] + jnp.dot(h_lo, wdt_ref[:, :ndt], preferred_element_type=F32)


def _ssd_in_proj(y, mod, wz, wx, wdt2, tl):
    bsz, L, d = y.shape
    mi = _mod_index(mod)
    ndt = wdt2.shape[1] // 2
    full = lambda a: pl.BlockSpec(a.shape, lambda i, j: (0, 0))
    return pl.pallas_call(
        _ssd_in_kernel,
        grid=(bsz, L // tl),
        in_specs=[
            pl.BlockSpec((1, tl, d), lambda i, j: (i, j, 0)),
            pl.BlockSpec((1, MOD_VECS, d), lambda i, j: (mi(i), 0, 0)),
            full(wz), full(wx), full(wdt2),
        ],
        out_specs=[
            pl.BlockSpec((1, tl, SSM_INNER), lambda i, j: (i, j, 0)),
            pl.BlockSpec((1, tl, SSM_CONV_DIM), lambda i, j: (i, j, 0)),
            pl.BlockSpec((1, tl, ndt), lambda i, j: (i, j, 0)),
        ],
        out_shape=[
            jax.ShapeDtypeStruct((bsz, L, SSM_INNER), BF16),
            jax.ShapeDtypeStruct((bsz, L, SSM_CONV_DIM), BF16),
            jax.ShapeDtypeStruct((bsz, L, ndt), F32),
        ],
        compiler_params=_params("parallel", "parallel"),
        name="ssd_in_proj",
    )(y, mod, wz, wx, wdt2)


def _ssd_conv_kernel(x_ref, w_ref, b_ref, o_ref, pad_ref):
    L, cb = x_ref.shape[1:]
    zeros = jnp.zeros((SUBLANES, cb), F32)
    pad_ref[0:SUBLANES, :] = zeros
    pad_ref[SUBLANES:SUBLANES + L, :] = x_ref[0].astype(F32)
    pad_ref[SUBLANES + L:2 * SUBLANES + L, :] = zeros
    first = SUBLANES - SSM_CONV_K // 2
    rows = min(L, 256)
    for r in range(L // rows):
        for c in range(cb // LANES):
            ln = slice(c * LANES, (c + 1) * LANES)
            base = first + r * rows
            acc = jnp.zeros((rows, LANES), F32) + b_ref[:, ln]
            for k in range(SSM_CONV_K):
                acc = acc + w_ref[k:k + 1, ln] * pad_ref[base + k:base + k + rows, ln]
            o_ref[0, r * rows:(r + 1) * rows, ln] = _silu(acc).astype(o_ref.dtype)


def _ssd_conv(xbc, w, b):
    bsz, L, cd = xbc.shape
    cb = max(w for w in range(LANES, cd + 1, LANES) if cd % w == 0 and (w * L <= (1 << 21) or w == LANES))
    return pl.pallas_call(
        _ssd_conv_kernel,
        grid=(bsz, cd // cb),
        in_specs=[
            pl.BlockSpec((1, L, cb), lambda i, j: (i, 0, j)),
            pl.BlockSpec((SSM_CONV_K, cb), lambda i, j: (0, j)),
            pl.BlockSpec((1, cb), lambda i, j: (0, j)),
        ],
        out_specs=pl.BlockSpec((1, L, cb), lambda i, j: (i, 0, j)),
        out_shape=jax.ShapeDtypeStruct((bsz, L, cd), BF16),
        scratch_shapes=[pltpu.VMEM((L + 2 * SUBLANES, cb), F32)],
        compiler_params=_params("parallel", "parallel"),
        name="ssd_conv",
    )(xbc, w, b.reshape(1, cd))


def _split2(x):
    hi = x.astype(BF16)
    return jnp.concatenate([hi, (x - hi.astype(F32)).astype(BF16)], axis=1)


def _split3(x):
    hi = x.astype(BF16)
    r1 = x - hi.astype(F32)
    mid = r1.astype(BF16)
    lo = (r1 - mid.astype(F32)).astype(BF16)
    return jnp.concatenate([hi, mid, lo], axis=1)


def _ssd_scan_kernel(xs_ref, b_ref, c_ref, dt_ref, bias_row_ref, alog_row_ref, tri_ref, e3_ref, *rest,
                     has_init):
    init_ref = rest[0] if has_init else None
    y_ref, fin_ref, s_ref = rest[-3:]
    d = pl.program_id(0)
    c = pl.program_id(2)
    q = SSD_CHUNK
    nh = SSM_HEADS
    gw = SSM_INNER // SSM_GROUPS
    hpl = LANES // SSM_HEAD_DIM

    @pl.when(c == 0)
    def _():
        if has_init:
            for i in range(nh // hpl):
                blk = jnp.concatenate([init_ref[0, 0, 0, hpl * i + j] for j in range(hpl)], axis=0)
                s_ref[:, i * LANES:(i + 1) * LANES] = blk.T
        else:
            s_ref[...] = jnp.zeros(s_ref.shape, F32)

    is_f = d == 0
    tri = tri_ref[0]
    mask = tri > 0.5
    dt2 = _softplus(dt_ref[0] + bias_row_ref[...])
    a2 = dt2 * -jnp.exp(alog_row_ref[...])
    dt2T = dt2.T
    c3 = jnp.dot(tri.astype(BF16), _split3(a2), preferred_element_type=F32)
    cum2 = c3[:, :LANES] + c3[:, LANES:2 * LANES] + c3[:, 2 * LANES:]
    cumT2 = cum2.T
    cum = jnp.where(is_f, cum2[:, :nh], cum2[:, nh:2 * nh])
    cumT = jnp.where(is_f, cumT2[:nh], cumT2[nh:2 * nh])
    dtT = jnp.where(is_f, dt2T[:nh], dt2T[nh:2 * nh])
    tot2 = jnp.where(is_f, cum2[q - 1:q], cum2[0:1])
    small = jnp.concatenate([dt2 * jnp.exp(jnp.minimum(tot2 - cum2, 0.0)), jnp.exp(cum2),
                             jnp.broadcast_to(jnp.exp(tot2), (SUBLANES, LANES))], axis=0)
    wide = jnp.dot(_split2(small), e3_ref[0], preferred_element_type=F32)
    w_state = wide[0:q]
    w_off = wide[q:2 * q]
    w_tot = wide[2 * q:2 * q + 1]

    xs_b = xs_ref[0]
    xs = xs_b.astype(F32)
    xdec = (xs * w_state).astype(BF16)
    lane = lax.broadcasted_iota(jnp.int32, (q, SSM_INNER), 1) % LANES
    zero = jnp.zeros((), BF16)
    x_stack = jnp.concatenate([jnp.where(lane < SSM_HEAD_DIM, xs_b, zero),
                               jnp.where(lane >= SSM_HEAD_DIM, xs_b, zero)], axis=0)
    for g in range(SSM_GROUPS):
        gs = slice(g * gw, (g + 1) * gw)
        bg = b_ref[0, :, g * D_STATE:(g + 1) * D_STATE]
        cg = c_ref[0, :, g * D_STATE:(g + 1) * D_STATE]
        cb = lax.dot_general(cg, bg, (((1,), (1,)), ((), ())), preferred_element_type=F32)
        s_g = s_ref[:, gs]
        y_off = jnp.dot(cg, s_g.astype(BF16), preferred_element_type=F32) * w_off[:, gs]
        s_ref[:, gs] = w_tot[:, gs] * s_g + jnp.dot(bg.astype(F32).T.astype(BF16), xdec[:, gs],
                                                    preferred_element_type=F32)
        pairs = gw // LANES
        for p in range(pairs):
            ls = slice(g * gw + p * LANES, g * gw + (p + 1) * LANES)
            ms = []
            for j in range(LANES // SSM_HEAD_DIM):
                h = (g * gw + p * LANES) // SSM_HEAD_DIM + j
                diff = cum[:, h:h + 1] - cumT[h:h + 1, :]
                lm = jnp.exp(jnp.where(mask, diff, -1e30))
                ms.append((cb * lm * dtT[h:h + 1, :]).astype(BF16))
            y_diag = jnp.dot(jnp.concatenate(ms, axis=1), x_stack[:, ls], preferred_element_type=F32)
            y_ref[0, 0, :, ls] = (y_diag + y_off[:, p * LANES:(p + 1) * LANES]).astype(y_ref.dtype)

    @pl.when(c == pl.num_programs(2) - 1)
    def _():
        for i in range(nh // hpl):
            blk = s_ref[:, i * LANES:(i + 1) * LANES].T
            for j in range(hpl):
                fin_ref[0, 0, hpl * i + j] = blk[j * SSM_HEAD_DIM:(j + 1) * SSM_HEAD_DIM]


def _ssd_scan(xbc, dt_raw, dt_bias, a_log, state, k):
    bsz, L, _ = xbc.shape
    cached = (1, 1, SSM_HEADS, SSM_HEAD_DIM, D_STATE)
    nc = L // SSD_CHUNK
    q = SSD_CHUNK
    ndt = LANES
    chunk = lambda d, c: c + d * (nc - 1 - 2 * c)
    r = jnp.arange(q)
    tri = jnp.stack([r[:, None] >= r[None, :], r[:, None] <= r[None, :]]).astype(F32)
    src = jnp.arange(ndt)[None, :, None] - SSM_HEADS * jnp.arange(2)[:, None, None]
    e1 = (src == (jnp.arange(SSM_INNER) // SSM_HEAD_DIM)[None, None, :]).astype(BF16)
    e3 = jnp.concatenate([e1, e1], axis=1)
    pad = lambda a: jnp.pad(a.reshape(-1), (0, ndt - a.size))
    dt_bias, a_log = pad(dt_bias), pad(a_log)
    nblk = SSM_INNER // SSM_BC
    const = lambda shape: pl.BlockSpec(shape, lambda d, b, c: tuple(0 for _ in shape))
    init_specs, init_args = [], []
    if state is not None:
        init_specs = [pl.BlockSpec((1,) + cached, lambda d, b, c: (b, k, d, 0, 0, 0))]
        init_args = [state]
    return pl.pallas_call(
        functools.partial(_ssd_scan_kernel, has_init=state is not None),
        grid=(2, bsz, nc),
        in_specs=[
            pl.BlockSpec((1, q, SSM_INNER), lambda d, b, c: (b, chunk(d, c), 0)),
            pl.BlockSpec((1, q, SSM_BC), lambda d, b, c: (b, chunk(d, c), nblk)),
            pl.BlockSpec((1, q, SSM_BC), lambda d, b, c: (b, chunk(d, c), nblk + 1)),
            pl.BlockSpec((1, q, ndt), lambda d, b, c: (b, chunk(d, c), 0)),
            const((1, ndt)), const((1, ndt)),
            pl.BlockSpec((1, q, q), lambda d, b, c: (d, 0, 0)),
            pl.BlockSpec((1, 2 * ndt, SSM_INNER), lambda d, b, c: (d, 0, 0)),
        ] + init_specs,
        out_specs=[
            pl.BlockSpec((1, 1, q, SSM_INNER), lambda d, b, c: (d, b, chunk(d, c), 0)),
            pl.BlockSpec(cached, lambda d, b, c: (b, d, 0, 0, 0)),
        ],
        out_shape=[
            jax.ShapeDtypeStruct((2, bsz, L, SSM_INNER), BF16),
            jax.ShapeDtypeStruct((bsz, 2) + cached[2:], F32),
        ],
        scratch_shapes=[pltpu.VMEM((D_STATE, SSM_INNER), F32)],
        compiler_params=_params("parallel", "parallel", "arbitrary"),
        name="ssd_scan",
    )(xbc, xbc, xbc, dt_raw, dt_bias.reshape(1, ndt), a_log.reshape(1, ndt), tri, e3, *init_args)


def _ssd_out_kernel(yf_ref, yb_ref, xs_ref, z_ref, res_ref, mod_ref, dskip_ref, nw_ref, wo_ref,
                    mixg_ref, mixb_ref, o_ref):
    y = yf_ref[0, 0].astype(F32) + yb_ref[0, 0].astype(F32) + dskip_ref[...] * xs_ref[0].astype(F32)
    y = y * _silu(z_ref[0].astype(F32))
    gw = SSM_INNER // SSM_GROUPS
    parts = []
    for g in range(SSM_GROUPS):
        yg = y[:, g * gw:(g + 1) * gw]
        ms = jnp.mean(yg * yg, axis=-1, keepdims=True)
        parts.append(yg * lax.rsqrt(ms + LN_EPS))
    yn = jnp.concatenate(parts, axis=1) * nw_ref[...]
    m = jnp.dot(yn.astype(BF16), wo_ref[...], preferred_element_type=F32)
    o_ref[0] = _layer_norm(ALPHA * res_ref[0] + mod_ref[0, 2:3, :] * m, mixg_ref[...], mixb_ref[...])


def _ssd_out(yscan, xbc, z, y, mod, d_skip, norm_w, out_w, mixg, mixb, tl):
    bsz, L, d = y.shape
    mi = _mod_index(mod)
    inner = pl.BlockSpec((1, tl, SSM_INNER), lambda i, j: (i, j, 0))
    vec = lambda n: pl.BlockSpec((1, n), lambda i, j: (0, 0))
    return pl.pallas_call(
        _ssd_out_kernel,
        grid=(bsz, L // tl),
        in_specs=[
            pl.BlockSpec((1, 1, tl, SSM_INNER), lambda i, j: (0, i, j, 0)),
            pl.BlockSpec((1, 1, tl, SSM_INNER), lambda i, j: (1, i, j, 0)),
            inner, inner,
            pl.BlockSpec((1, tl, d), lambda i, j: (i, j, 0)),
            pl.BlockSpec((1, MOD_VECS, d), lambda i, j: (mi(i), 0, 0)),
            vec(SSM_INNER), vec(SSM_INNER),
            pl.BlockSpec((SSM_INNER, d), lambda i, j: (0, 0)),
            vec(d), vec(d),
        ],
        out_specs=pl.BlockSpec((1, tl, d), lambda i, j: (i, j, 0)),
        out_shape=jax.ShapeDtypeStruct((bsz, L, d), F32),
        compiler_params=_params("parallel", "parallel"),
        name="ssd_out",
    )(yscan, yscan, xbc, z, y, mod, jnp.repeat(d_skip, SSM_HEAD_DIM).reshape(1, SSM_INNER),
      norm_w.reshape(1, SSM_INNER), out_w, mixg.reshape(1, d), mixb.reshape(1, d))


def _cast_cols_kernel(x_ref, o_ref):
    o_ref[...] = x_ref[0].astype(o_ref.dtype)


def _cast_cols(w, k, first, count, bw):
    rows = w.shape[1]
    return pl.pallas_call(
        _cast_cols_kernel,
        grid=(count,),
        in_specs=[pl.BlockSpec((1, rows, bw), lambda j: (k, 0, first + j))],
        out_specs=pl.BlockSpec((rows, bw), lambda j: (0, j)),
        out_shape=jax.ShapeDtypeStruct((rows, count * bw), BF16),
        compiler_params=_params("parallel"),
        name="cast_cols",
    )(w)


def _dt_weight_kernel(w_ref, o_ref, *, valid):
    w = w_ref[0]
    lane = lax.broadcasted_iota(jnp.int32, w.shape, 1)
    w = jnp.where(lane < valid, w, 0.0)
    o_ref[...] = _split2(w)


def _dt_weight(w, k, first_col, valid):
    rows = w.shape[1]
    assert first_col % LANES == 0 and valid <= LANES
    return pl.pallas_call(
        functools.partial(_dt_weight_kernel, valid=valid),
        grid=(1,),
        in_specs=[pl.BlockSpec((1, rows, LANES), lambda j: (k, 0, first_col // LANES))],
        out_specs=pl.BlockSpec((rows, 2 * LANES), lambda j: (0, 0)),
        out_shape=jax.ShapeDtypeStruct((rows, 2 * LANES), BF16),
        compiler_params=_params("arbitrary"),
        name="dt_weight",
    )(w)


def _ssd_layer(y, mod, sw, state, k, mixg, mixb):
    (wz, wx, wdt2), conv_w, conv_b, dt_bias, a_log, d_skip, norm_w, out_w = sw
    bsz, L, d = y.shape
    z, xbc, dt_raw = _ssd_in_proj(y, mod, wz, wx, wdt2, tl=min(L, 512))
    xbc = _ssd_conv(xbc, conv_w, conv_b)
    yscan, fin = _ssd_scan(xbc, dt_raw, dt_bias, a_log, state, k)
    out = _ssd_out(yscan, xbc, z, y, mod, d_skip, norm_w, out_w, mixg, mixb, tl=min(L, 512))
    return out, fin


def _moe_prep_kernel(y_ref, mod_ref, rwT_ref, h_ref, affT_ref):
    h32 = y_ref[0] * (1.0 + mod_ref[0, 4:5, :]) + mod_ref[0, 3:4, :]
    h = h32.astype(BF16)
    h_ref[...] = h
    ne = affT_ref.shape[0]
    nt = (((1,), (1,)), ((), ()))
    both = lax.dot_general(rwT_ref[...], h, nt, preferred_element_type=F32)
    h_lo = (h32 - h.astype(F32)).astype(BF16)
    logT = both[:ne] + both[ne:] + lax.dot_general(rwT_ref[:ne], h_lo, nt, preferred_element_type=F32)
    ex = jnp.exp(logT - jnp.max(logT, axis=0, keepdims=True))
    affT_ref[...] = ex / jnp.sum(ex, axis=0, keepdims=True)


def _moe_prep(y, mod, router_w, tl):
    bsz, L, d = y.shape
    mi = _mod_index(mod)
    nt = L // tl
    ne = router_w.shape[1]
    rwT = router_w.T
    rw_hi = rwT.astype(BF16)
    return pl.pallas_call(
        _moe_prep_kernel,
        grid=(bsz, nt),
        in_specs=[
            pl.BlockSpec((1, tl, d), lambda i, j: (i, j, 0)),
            pl.BlockSpec((1, MOD_VECS, d), lambda i, j: (mi(i), 0, 0)),
            pl.BlockSpec((2 * ne, d), lambda i, j: (0, 0)),
        ],
        out_specs=[
            pl.BlockSpec((tl, d), lambda i, j: (i * nt + j, 0)),
            pl.BlockSpec((ne, tl), lambda i, j: (0, i * nt + j)),
        ],
        out_shape=[
            jax.ShapeDtypeStruct((bsz * L, d), BF16),
            jax.ShapeDtypeStruct((ne, bsz * L), F32),
        ],
        compiler_params=_params("parallel", "parallel"),
        name="moe_prep",
    )(y, mod, jnp.concatenate([rw_hi, (rwT - rw_hi.astype(F32)).astype(BF16)], axis=0))


def _moe_select_kernel(aff_ref, upper_ref, gate_ref, *, cap, nb):
    ne, T = aff_ref.shape
    aff = aff_ref[...]
    bits = lax.bitcast_convert_type(aff, jnp.int32)

    def search(i, v):
        cand = v | lax.shift_left(jnp.int32(1), 30 - i)
        cnt = jnp.sum(jnp.where(bits >= cand, 1.0, 0.0), axis=1, keepdims=True)
        return jnp.where(cnt >= cap, cand, v)

    thr = lax.fori_loop(0, 31, search, jnp.zeros((ne, 1), jnp.int32))
    need = cap - jnp.sum(jnp.where(bits > thr, 1.0, 0.0), axis=1, keepdims=True)
    upper = upper_ref[...]
    tie_rank = jnp.zeros((ne, 1), F32)
    for j in range(T // LANES):
        ls = slice(j * LANES, (j + 1) * LANES)
        blk = bits[:, ls]
        eq = jnp.where(blk == thr, 1.0, 0.0)
        eq_incl = jnp.dot(eq.astype(BF16), upper, preferred_element_type=F32)
        sel = (blk > thr) | ((blk == thr) & (eq_incl - eq + tie_rank < need))
        tie_rank = tie_rank + eq_incl[:, LANES - 1:LANES]
        dst = ((j % nb) * (T // LANES // nb) + j // nb) * LANES
        gate_ref[:, dst:dst + LANES] = jnp.where(sel, aff[:, ls], -1.0)


def _moe_slot_kernel(gate_ref, upper_ref, pos_ref, total_ref, *, tb):
    ne, T = gate_ref.shape
    upper = upper_ref[...]
    lane = lax.broadcasted_iota(jnp.int32, (ne, LANES), 1)
    slot = jnp.zeros((ne, 1), F32)
    totals = jnp.zeros((ne, LANES), F32)
    for j in range(T // LANES):
        t0 = j * LANES
        sel = gate_ref[:, t0:t0 + LANES] >= 0.0
        m = jnp.where(sel, 1.0, 0.0)
        m_incl = jnp.dot(m.astype(BF16), upper, preferred_element_type=F32)
        pos_ref[:, t0:t0 + LANES] = jnp.where(sel, m_incl - m + slot, -1.0).astype(jnp.int32)
        slot = slot + m_incl[:, LANES - 1:LANES]
        if (t0 + LANES) % tb == 0:
            totals = jnp.where(lane == t0 // tb, slot, totals)
            slot = jnp.zeros((ne, 1), F32)
    total_ref[...] = totals.astype(jnp.int32)


def _moe_select(affT, cap, tb, nb):
    ne, T = affT.shape
    assert MOE_GROUP == LANES
    r = jnp.arange(LANES)
    upper = (r[:, None] <= r[None, :]).astype(BF16)
    full = pl.BlockSpec((ne, T), lambda i: (0, 0))
    tri = pl.BlockSpec((LANES, LANES), lambda i: (0, 0))
    small = pl.BlockSpec((ne, LANES), lambda i: (0, 0))
    gate = pl.pallas_call(
        functools.partial(_moe_select_kernel, cap=cap, nb=nb),
        grid=(1,),
        in_specs=[full, tri],
        out_specs=full,
        out_shape=jax.ShapeDtypeStruct((ne, T), F32),
        compiler_params=_params("arbitrary"),
        name="moe_select",
    )(affT, upper)
    pos, totals = pl.pallas_call(
        functools.partial(_moe_slot_kernel, tb=tb),
        grid=(1,),
        in_specs=[full, tri],
        out_specs=[full, small],
        out_shape=[jax.ShapeDtypeStruct((ne, T), jnp.int32), jax.ShapeDtypeStruct((ne, LANES), jnp.int32)],
        compiler_params=_params("arbitrary"),
        name="moe_slot",
    )(gate, upper)
    return gate, pos, totals


def _moe_ffn_kernel(tot_ref, h_ref, pos_ref, gate_ref, wg_ref, wu_ref, wd_ref, y_ref, mod_ref, lng_ref,
                    lnb_ref, o_ref, acc_ref):
    b = pl.program_id(0)
    eg = pl.program_id(1)
    rows = MOE_ROWS
    tb, d = acc_ref.shape
    per = pos_ref.shape[1]
    ne = pl.num_programs(1) * per

    @pl.when(eg == 0)
    def _():
        acc_ref[...] = jnp.zeros(acc_ref.shape, F32)

    slot_iota = lax.broadcasted_iota(jnp.int32, (rows, tb), 0)
    for j in range(per):
        def chunk(k, carry, j=j):
            hit = pos_ref[0, j:j + 1, :] == slot_iota + k * rows
            onehot = jnp.where(hit, 1.0, 0.0).astype(BF16)
            x = jnp.dot(onehot, h_ref[...].reshape(tb, d), preferred_element_type=F32).astype(BF16)
            gate = jnp.sum(jnp.where(hit, gate_ref[0, j:j + 1, :], 0.0), axis=1, keepdims=True)
            hid = _silu(jnp.dot(x, wg_ref[0, j], preferred_element_type=F32)) * jnp.dot(
                x, wu_ref[0, j], preferred_element_type=F32)
            yk = jnp.dot(hid.astype(BF16), wd_ref[0, j], preferred_element_type=F32)
            yk = (yk * gate).astype(BF16)
            acc_ref[...] += lax.dot_general(onehot, yk, (((0,), (0,)), ((), ())),
                                            preferred_element_type=F32)
            return carry

        lax.fori_loop(0, (tot_ref[b * ne + eg * per + j] + rows - 1) // rows, chunk, 0)

    @pl.when(eg == pl.num_programs(1) - 1)
    def _():
        nmod = mod_ref.shape[0]
        seg = tb // nmod
        y = y_ref[...].reshape(tb, d)
        for s in range(nmod):
            rs = slice(s * seg, (s + 1) * seg)
            out = _layer_norm(ALPHA * y[rs] + mod_ref[s, 5:6, :] * acc_ref[rs, :], lng_ref[...],
                              lnb_ref[...])
            o_ref[s * seg // MOE_GROUP:(s + 1) * seg // MOE_GROUP] = out.reshape(
                seg // MOE_GROUP, 1, MOE_GROUP, d)


def _moe_ffn(y, h, pos, gate, totals, mod, layer, w_gate, w_up, w_down, g, b, tb):
    bsz, L, d = y.shape
    T = bsz * L
    _, ne, _, ff = w_gate.shape
    per = MOE_EXPERTS_PER_STEP
    nb = T // tb
    ng = tb // MOE_GROUP
    nmod = mod.shape[0]
    assert nmod == 1 or (nmod == bsz and L % (MOE_GROUP * nb) == 0)
    tot = totals[:, :nb].T.reshape(-1)
    row = pl.BlockSpec((1, per, tb), lambda i, e, t: (e, 0, i))
    tok = pl.BlockSpec((ng, 1, MOE_GROUP, d), lambda i, e, t: (0, i, 0, 0))
    vec = pl.BlockSpec((1, d), lambda i, e, t: (0, 0))
    grid_spec = pltpu.PrefetchScalarGridSpec(
        num_scalar_prefetch=1,
        grid=(nb, ne // per),
        in_specs=[
            tok, row, row,
            pl.BlockSpec((1, per, d, ff), lambda i, e, t: (layer, e, 0, 0)),
            pl.BlockSpec((1, per, d, ff), lambda i, e, t: (layer, e, 0, 0)),
            pl.BlockSpec((1, per, ff, d), lambda i, e, t: (layer, e, 0, 0)),
            tok,
            pl.BlockSpec((nmod, MOD_VECS, d), lambda i, e, t: (0, 0, 0)),
            vec, vec,
        ],
        out_specs=tok,
        scratch_shapes=[pltpu.VMEM((tb, d), F32)],
    )
    out = pl.pallas_call(
        _moe_ffn_kernel,
        grid_spec=grid_spec,
        out_shape=jax.ShapeDtypeStruct((ng, nb, MOE_GROUP, d), F32),
        compiler_params=_params("parallel", "arbitrary"),
        name="moe_ffn",
    )(tot, h.reshape(ng, nb, MOE_GROUP, d), pos.reshape(ne // per, per, T), gate.reshape(ne // per, per, T),
      w_gate, w_up, w_down, y.reshape(ng, nb, MOE_GROUP, d), mod, g.reshape(1, d), b.reshape(1, d))
    return out.reshape(bsz, L, d)


def _moe_layer(y, mod, router_w, layer, w_gate, w_up, w_down, g, b):
    bsz, L, d = y.shape
    T = bsz * L
    ne = router_w.shape[1]
    cap = EC_CAPACITY_FACTOR * T // ne
    tb = min(T, MOE_TOKEN_BLOCK)
    h, affT = _moe_prep(y, mod, router_w, tl=min(L, 1024))
    gate, pos, totals = _moe_select(affT, cap, tb, T // tb)
    return _moe_ffn(y, h, pos, gate, totals, mod, layer, w_gate, w_up, w_down, g, b, tb)


def _cast_kernel(x_ref, o_ref):
    o_ref[...] = x_ref[...].astype(o_ref.dtype)


def _cast_bf16(w):
    depth, ne, r, c = w.shape
    blk = pl.BlockSpec((1, 4, r, c), lambda i, j: (i, j, 0, 0))
    return pl.pallas_call(
        _cast_kernel,
        grid=(depth, ne // 4),
        in_specs=[blk],
        out_specs=blk,
        out_shape=jax.ShapeDtypeStruct(w.shape, BF16),
        compiler_params=_params("parallel", "parallel"),
        name="cast_bf16",
    )(w)


def kernel(x_prompt, x_sample, state_ssd, c, c_ctx, ada_w, ada_b, ln_mix_g, ln_mix_b, ln_ffn_g, ln_ffn_b, conv_pw1_w, conv_pw1_b, conv_dw_w, conv_dw_b, conv_ln_g, conv_ln_b, conv_pw2_w, conv_pw2_b, ssd_in_w, ssd_conv_w, ssd_conv_b, ssd_dt_bias, ssd_a_log, ssd_d_skip, ssd_norm_w, ssd_out_w, router_w, moe_w_gate, moe_w_up, moe_w_down):
    d = x_prompt.shape[-1]
    nb_s = x_sample.shape[0]
    cond = jnp.concatenate([c_ctx[None], c, jnp.zeros((MOD_ROWS - 1 - nb_s, d), F32)], axis=0)
    mods = _ada_mod(cond, ada_w, ada_b).reshape(DEPTH, MOD_ROWS, MOD_VECS, d)
    w_gate, w_up, w_down = _cast_bf16(moe_w_gate), _cast_bf16(moe_w_up), _cast_bf16(moe_w_down)
    yp, ys = x_prompt, x_sample
    new_states = []
    for i in range(DEPTH):
        mod_p = mods[i, 0:1]
        mod_s = mods[i, 1:1 + nb_s]
        k = i // 2
        if i % 2 == 0:
            cw = (conv_pw1_w[k], conv_pw1_b[k], conv_dw_w[k], conv_dw_b[k], conv_ln_g[k],
                  conv_ln_b[k], conv_pw2_w[k], conv_pw2_b[k])
            yp = _conformer_layer(yp, mod_p, cw, "seq", ln_mix_g[i], ln_mix_b[i])
            ys = _conformer_layer(ys, mod_s, cw, "row" if k % 2 == 0 else "col", ln_mix_g[i],
                                  ln_mix_b[i])
        else:
            bw = SSM_INNER // 2
            in_split = (_cast_cols(ssd_in_w, k, 0, SSM_INNER // bw, bw),
                        _cast_cols(ssd_in_w, k, SSM_INNER // bw, SSM_CONV_DIM // bw, bw),
                        _dt_weight(ssd_in_w, k, SSM_INNER + SSM_CONV_DIM, 2 * SSM_HEADS))
            sw = (in_split, ssd_conv_w[k], ssd_conv_b[k], ssd_dt_bias[k], ssd_a_log[k],
                  ssd_d_skip[k], ssd_norm_w[k], ssd_out_w[k].astype(BF16))
            yp, fin = _ssd_layer(yp, mod_p, sw, None, k, ln_mix_g[i], ln_mix_b[i])
            ys, _ = _ssd_layer(ys, mod_s, sw, state_ssd, k, ln_mix_g[i], ln_mix_b[i])
            new_states.append(fin.astype(x_prompt.dtype))
        yp = _moe_layer(yp, mod_p, router_w[i], i, w_gate, w_up, w_down, ln_ffn_g[i], ln_ffn_b[i])
        ys = _moe_layer(ys, mod_s, router_w[i], i, w_gate, w_up, w_down, ln_ffn_g[i], ln_ffn_b[i])
    return (yp, ys, jnp.stack(new_states, axis=1))
```
